```python
import math
import jax, jax.numpy as jnp
from jax import lax
import numpy as np

D_MODEL = 1024
BATCH = 8
SEQ = 2048
DEPTH = 1
DEC_BATCH = 128
DEC_SEQ = 8
PAST_LEN = 16384
PAGE_SIZE = 128

D_MIX = D_MODEL
GLA_WIDTH = D_MIX // 2
GLA_HEADS = 4
GLA_DK = GLA_WIDTH // 2 // GLA_HEADS
GLA_DV = GLA_WIDTH // GLA_HEADS
GLA_RANK = 16
GLA_TAU = 16.0
GLA_CHUNK = 64
GMLP_WIDTH = D_MIX - GLA_WIDTH
GMLP_HEADS = 4
GMLP_DH = GMLP_WIDTH // GMLP_HEADS
GMLP_CHUNK = 128
QK_WIDTH = GLA_HEADS * GLA_DK
N_IN = 2 * QK_WIDTH + 2 * GLA_WIDTH + GLA_RANK + 2 * GMLP_WIDTH
N_EXPERTS = 64
TOP_K = 8
N_GROUPS = 8
TOPK_GROUPS = 4
D_EXPERT = 256
D_SHARED = 256
ROUTE_SCALE = 2.5
ALPHA = (2.0 * DEPTH) ** 0.25
BETA = (8.0 * DEPTH) ** -0.25

kernel_name = 'hymba_gla_gmlp_moe_deepnorm_step'


def layernorm(x, g, b, eps=1e-5):
    xf = x.astype(jnp.float32)
    mu = xf.mean(-1, keepdims=True)
    var = jnp.square(xf - mu).mean(-1, keepdims=True)
    return ((xf - mu) * lax.rsqrt(var + eps) * g + b).astype(x.dtype)


def head_rmsnorm(x, g, eps=1e-6):
    xf = x.astype(jnp.float32)
    return xf * lax.rsqrt(jnp.square(xf).mean(-1, keepdims=True) + eps) * g


def gla_scan(q, k, v, log_a, s0):
    B, T, H, DK = q.shape
    DV = v.shape[-1]
    C = math.gcd(T, GLA_CHUNK)
    n = T // C

    def blocks(a):
        return jnp.moveaxis(a.astype(jnp.float32).reshape(B, n, C, H, a.shape[-1]), 1, 0)

    causal = jnp.tril(jnp.ones((C, C), dtype=bool))[None, :, :, None, None]

    def step(S, inp):
        qc, kc, vc, ac = inp
        b = jnp.cumsum(ac, axis=1)
        diff = b[:, :, None] - b[:, None, :]
        decay = jnp.exp(jnp.where(causal, diff, -jnp.inf))
        scores = jnp.einsum('bihd,bjhd,bijhd->bhij', qc, kc, decay)
        o = (jnp.einsum('bhij,bjhv->bihv', scores, vc)
             + jnp.einsum('bihd,bhdv->bihv', qc * jnp.exp(b), S))
        b_last = b[:, -1]
        S = (jnp.exp(b_last)[..., None] * S
             + jnp.einsum('bjhd,bjhv->bhdv', kc * jnp.exp(b_last[:, None] - b), vc))
        return S, o

    S, o = lax.scan(step, s0.astype(jnp.float32), (blocks(q), blocks(k), blocks(v), blocks(log_a)))
    return jnp.moveaxis(o, 0, 1).reshape(B, T, H, DV), S


def gmlp_spatial(v, w_s, b_s):
    B, T, H, D = v.shape
    C = min(T, GMLP_CHUNK)
    n = T // C
    w = jnp.tril(w_s[:, :C, :C])
    out = jnp.einsum('hts,bnshd->bnthd', w, v.reshape(B, n, C, H, D))
    out = out + b_s[:, :C].T[None, None, :, :, None]
    return out.reshape(B, T, H, D)


def mixer(x, s0, w_in, w_a_up, b_a, gla_norm_g, gmlp_ln_g, gmlp_ln_b, w_s, b_s, gmlp_norm_g, w_out):
    B, T, _ = x.shape
    z = x @ w_in
    o1 = QK_WIDTH
    o2 = o1 + QK_WIDTH
    o3 = o2 + GLA_WIDTH
    o4 = o3 + GLA_WIDTH
    o5 = o4 + GLA_RANK
    o6 = o5 + GMLP_WIDTH
    q, k, v_a, g, a_lr, u, v_b = jnp.split(z, [o1, o2, o3, o4, o5, o6], axis=-1)
    log_a = jax.nn.log_sigmoid((a_lr @ w_a_up + b_a).astype(jnp.float32)) / GLA_TAU
    o_gla, s_new = gla_scan(
        q.reshape(B, T, GLA_HEADS, GLA_DK) * (GLA_DK ** -0.5),
        k.reshape(B, T, GLA_HEADS, GLA_DK),
        v_a.reshape(B, T, GLA_HEADS, GLA_DV),
        log_a.reshape(B, T, GLA_HEADS, GLA_DK),
        s0)
    o_gla = head_rmsnorm(o_gla, gla_norm_g) * jax.nn.silu(
        g.reshape(B, T, GLA_HEADS, GLA_DV).astype(jnp.float32))
    u = jax.nn.gelu(u)
    vg = layernorm(jax.nn.gelu(v_b), gmlp_ln_g, gmlp_ln_b)
    sgu = gmlp_spatial(vg.reshape(B, T, GMLP_HEADS, GMLP_DH), w_s, b_s)
    o_gmlp = head_rmsnorm(u.reshape(B, T, GMLP_HEADS, GMLP_DH) * sgu, gmlp_norm_g)
    merged = jnp.concatenate([o_gla.reshape(B, T, GLA_WIDTH),
                              o_gmlp.reshape(B, T, GMLP_WIDTH)], axis=-1).astype(x.dtype)
    return merged @ w_out, s_new.astype(s0.dtype), vg


def moe(h, w_router, router_bias, w_gate, w_up, w_down, ws_gate, ws_up, ws_down):
    B, T, D = h.shape
    t = h.reshape(B * T, D)
    N = t.shape[0]
    s = jax.nn.sigmoid((t @ w_router).astype(jnp.float32))
    sb = s + router_bias.astype(jnp.float32)
    grp = sb.reshape(N, N_GROUPS, N_EXPERTS // N_GROUPS)
    group_score = lax.top_k(grp, 2)[0].sum(-1)
    _, gidx = lax.top_k(group_score, TOPK_GROUPS)
    gmask = jnp.any(gidx[..., None] == jnp.arange(N_GROUPS), axis=-2)
    emask = jnp.repeat(gmask, N_EXPERTS // N_GROUPS, axis=-1)
    _, eidx = lax.top_k(jnp.where(emask, sb, -jnp.inf), TOP_K)
    sel = jnp.take_along_axis(s, eidx, axis=-1)
    wts = sel / sel.sum(-1, keepdims=True) * ROUTE_SCALE
    combine = jnp.zeros((N, N_EXPERTS), jnp.float32).at[jnp.arange(N)[:, None], eidx].set(wts)
    hid = jax.nn.silu(jnp.einsum('nd,edf->nef', t, w_gate)) * jnp.einsum('nd,edf->nef', t, w_up)
    hid = hid * combine[..., None].astype(hid.dtype)
    routed = jnp.einsum('nef,efd->nd', hid, w_down)
    shared = (jax.nn.silu(t @ ws_gate) * (t @ ws_up)) @ ws_down
    return (routed + shared).reshape(B, T, D)


def layer(x, s0, w_in, w_a_up, b_a, gla_norm_g, gmlp_ln_g, gmlp_ln_b, w_s, b_s, gmlp_norm_g,
          w_out, ln1_g, ln1_b, w_router, router_bias, w_gate, w_up, w_down,
          ws_gate, ws_up, ws_down, ln2_g, ln2_b):
    m, s_new, v_rows = mixer(x, s0, w_in, w_a_up, b_a, gla_norm_g, gmlp_ln_g, gmlp_ln_b,
                             w_s, b_s, gmlp_norm_g, w_out)
    h = layernorm(ALPHA * x + m, ln1_g, ln1_b)
    f = moe(h, w_router, router_bias, w_gate, w_up, w_down, ws_gate, ws_up, ws_down)
    y = layernorm(ALPHA * h + f, ln2_g, ln2_b)
    return y, s_new, v_rows


def setup_inputs(seed: int = 0) -> dict:
    key = jax.random.key(seed)
    ks = jax.random.split(key, 32)
    f32 = jnp.float32
    nrm = lambda k, shape, scale: jax.random.normal(k, shape, f32) * scale
    L = DEPTH
    col_scale = jnp.concatenate([
        jnp.ones((2 * QK_WIDTH,), f32), jnp.full((GLA_WIDTH,), BETA, f32),
        jnp.ones((GLA_WIDTH + GLA_RANK,), f32), jnp.full((GMLP_WIDTH,), BETA, f32),
        jnp.ones((GMLP_WIDTH,), f32)])
    return {
        'x_prompt': nrm(ks[0], (BATCH, SEQ, D_MODEL), 1.0),
        'x_sample': nrm(ks[1], (DEC_BATCH, DEC_SEQ, D_MODEL), 1.0),
        'state_gla': nrm(ks[2], (L, DEC_BATCH, GLA_HEADS, GLA_DK, GLA_DV), 0.5),
        'w_in': nrm(ks[3], (L, D_MODEL, N_IN), D_MODEL ** -0.5) * col_scale,
        'w_a_up': nrm(ks[4], (L, GLA_RANK, QK_WIDTH), GLA_RANK ** -0.5),
        'b_a': nrm(ks[5], (L, QK_WIDTH), 0.1),
        'gla_norm_g': 1.0 + nrm(ks[6], (L, GLA_HEADS, GLA_DV), 0.1),
        'gmlp_ln_g': 1.0 + nrm(ks[7], (L, GMLP_WIDTH), 0.1),
        'gmlp_ln_b': nrm(ks[8], (L, GMLP_WIDTH), 0.1),
        'w_s': nrm(ks[9], (L, GMLP_HEADS, GMLP_CHUNK, GMLP_CHUNK), GMLP_CHUNK ** -0.5),
        'b_s': 1.0 + nrm(ks[10], (L, GMLP_HEADS, GMLP_CHUNK), 0.1),
        'gmlp_norm_g': 1.0 + nrm(ks[11], (L, GMLP_HEADS, GMLP_DH), 0.1),
        'w_out': nrm(ks[12], (L, D_MIX, D_MODEL), D_MIX ** -0.5 * BETA),
        'ln1_g': 1.0 + nrm(ks[13], (L, D_MODEL), 0.1),
        'ln1_b': nrm(ks[14], (L, D_MODEL), 0.1),
        'w_router': nrm(ks[15], (L, D_MODEL, N_EXPERTS), D_MODEL ** -0.5),
        'router_bias': nrm(ks[16], (L, N_EXPERTS), 0.01),
        'w_gate': nrm(ks[17], (L, N_EXPERTS, D_MODEL, D_EXPERT), D_MODEL ** -0.5),
        'w_up': nrm(ks[18], (L, N_EXPERTS, D_MODEL, D_EXPERT), D_MODEL ** -0.5),
        'w_down': nrm(ks[19], (L, N_EXPERTS, D_EXPERT, D_MODEL), D_EXPERT ** -0.5 * BETA),
        'ws_gate': nrm(ks[20], (L, D_MODEL, D_SHARED), D_MODEL ** -0.5),
        'ws_up': nrm(ks[21], (L, D_MODEL, D_SHARED), D_MODEL ** -0.5),
        'ws_down': nrm(ks[22], (L, D_SHARED, D_MODEL), D_SHARED ** -0.5 * BETA),
        'ln2_g': 1.0 + nrm(ks[23], (L, D_MODEL), 0.1),
        'ln2_b': nrm(ks[24], (L, D_MODEL), 0.1),
    }


def reference(x_prompt, x_sample, state_gla, w_in, w_a_up, b_a, gla_norm_g, gmlp_ln_g, gmlp_ln_b,
              w_s, b_s, gmlp_norm_g, w_out, ln1_g, ln1_b, w_router, router_bias, w_gate, w_up,
              w_down, ws_gate, ws_up, ws_down, ln2_g, ln2_b):
    hp, hs = x_prompt, x_sample
    gla_p, gla_s, v_s = [], [], []
    for l in range(DEPTH):
        lw = (w_in[l], w_a_up[l], b_a[l], gla_norm_g[l], gmlp_ln_g[l], gmlp_ln_b[l], w_s[l], b_s[l],
              gmlp_norm_g[l], w_out[l], ln1_g[l], ln1_b[l], w_router[l], router_bias[l], w_gate[l],
              w_up[l], w_down[l], ws_gate[l], ws_up[l], ws_down[l], ln2_g[l], ln2_b[l])
        s0_prompt = jnp.zeros((hp.shape[0], GLA_HEADS, GLA_DK, GLA_DV), state_gla.dtype)
        hp, sp, _ = layer(hp, s0_prompt, *lw)
        hs, ss, vrows = layer(hs, state_gla[l], *lw)
        gla_p.append(sp)
        gla_s.append(ss)
        v_s.append(vrows)
    state_gla_prompt = jnp.stack(gla_p)
    state_gla_sample = jnp.stack(gla_s)
    state_gmlp_v_sample = jnp.stack(v_s)
    return (hp, hs, state_gla_prompt, state_gla_sample, state_gmlp_v_sample)
```

```python
import functools
import math

import jax
import jax.numpy as jnp
from jax import lax
from jax.experimental import pallas as pl
from jax.experimental.pallas import tpu as pltpu

F32 = jnp.float32
BF16 = jnp.bfloat16

D_MODEL = 1024
DEPTH = 1
GLA_WIDTH = 512
GLA_HEADS = 4
GLA_DK = 64
GLA_DV = 128
GLA_RANK = 16
GLA_TAU = 16.0
GMLP_WIDTH = 512
GMLP_HEADS = 4
GMLP_DH = 128
GMLP_CHUNK = 128
QK_WIDTH = GLA_HEADS * GLA_DK
N_EXPERTS = 64
TOP_K = 8
N_GROUPS = 8
GROUP_SIZE = N_EXPERTS // N_GROUPS
TOPK_GROUPS = 4
D_EXPERT = 256
D_SHARED = 256
ROUTE_SCALE = 2.5
ALPHA = (2.0 * DEPTH) ** 0.25

LANES = 128

C_Q = 0
C_K = C_Q + QK_WIDTH
C_VA = C_K + QK_WIDTH
C_G = C_VA + GLA_WIDTH
C_U = C_G + GLA_WIDTH
C_VB = C_U + GMLP_WIDTH
C_ALR = C_VB + GMLP_WIDTH
N_PROJ = C_ALR + LANES

CHUNK = 128
PROMPT_TILE = 256
SAMPLE_SEQS = 32
MOE_TILE = 1024
VMEM_LIMIT = 56 * 1024 * 1024


def _dot(a, b):
    return jnp.dot(a, b, preferred_element_type=F32)


def _dot_nt(a, b):
    return lax.dot_general(a, b, (((1,), (1,)), ((), ())), preferred_element_type=F32)


def _dot_tn(a, b):
    return lax.dot_general(a, b, (((0,), (0,)), ((), ())), preferred_element_type=F32)


def _shr(x, d):
    assert d & (d - 1) == 0
    return lax.shift_right_logical(x, d.bit_length() - 1)


def _split_dot(m01, x):
    hi = x.astype(BF16)
    lo = (x - hi.astype(F32)).astype(BF16)
    return _dot(m01, hi) + _dot(m01, lo)


def _split_dot_r(x, m01):
    hi = x.astype(BF16)
    lo = (x - hi.astype(F32)).astype(BF16)
    return _dot(hi, m01) + _dot(lo, m01)


def _sigmoid(x):
    return 1.0 / (1.0 + jnp.exp(-x))


def _silu(x):
    return x * _sigmoid(x)


def _gelu(x):
    c = math.sqrt(2.0 / math.pi)
    return x * (0.5 * (1.0 + jnp.tanh(c * (x + 0.044715 * (x * x * x)))))


def _log_sigmoid(x):
    return -(jnp.maximum(-x, 0.0) + jnp.log1p(jnp.exp(-jnp.abs(x))))


def _layernorm(x, g, b, eps=1e-5):
    mu = jnp.mean(x, axis=-1, keepdims=True)
    xc = x - mu
    var = jnp.mean(xc * xc, axis=-1, keepdims=True)
    return xc * lax.rsqrt(var + eps) * g + b


def _rmsnorm(x, g, eps=1e-6):
    return x * lax.rsqrt(jnp.mean(x * x, axis=-1, keepdims=True) + eps) * g


def _gmlp_heads(u, vg, w_mix, bias, gmg_ref, merged_ref, rows):
    vgb = vg.astype(BF16)
    for h in range(GMLP_HEADS):
        cs = slice(h * GMLP_DH, (h + 1) * GMLP_DH)
        sgu = _dot(w_mix(h), vgb[:, cs]) + bias[:, cs]
        y = _rmsnorm(u[:, cs] * sgu, gmg_ref[h:h + 1, :])
        merged_ref[rows, GLA_WIDTH + h * GMLP_DH:GLA_WIDTH + (h + 1) * GMLP_DH] = y.astype(BF16)


def _out_proj_ln(x, merged_ref, wout_ref, ln1g_ref, ln1b_ref):
    m = _dot(merged_ref[...], wout_ref[...])
    return _layernorm(ALPHA * x + m, ln1g_ref[...], ln1b_ref[...])


def _mixer_prompt_kernel(x_ref, win_ref, waup_ref, ba_ref, glag_ref, lng_ref, lnb_ref, ws_ref, bsf_ref,
                         gmg_ref, wout_ref, ln1g_ref, ln1b_ref,
                         h_ref, state_ref,
                         z_ref, merged_ref, st_ref):
    t = pl.program_id(1)

    @pl.when(t == 0)
    def _():
        st_ref[...] = jnp.zeros_like(st_ref)

    x = x_ref[0]
    z_ref[...] = _dot(x.astype(BF16), win_ref[...])

    row_i = lax.broadcasted_iota(jnp.int32, (CHUNK, CHUNK), 0)
    col_i = lax.broadcasted_iota(jnp.int32, (CHUNK, CHUNK), 1)
    causal = row_i >= col_i
    tri = jnp.where(causal, 1.0, 0.0).astype(BF16)
    lane = lax.broadcasted_iota(jnp.int32, (1, LANES), 1)
    head_lanes = [lane < GLA_DK, lane >= GLA_DK]
    mid = CHUNK // 2 - 1

    for c in range(PROMPT_TILE // CHUNK):
        rows = slice(c * CHUNK, (c + 1) * CHUNK)
        a_pre = _dot(z_ref[rows, C_ALR:C_ALR + LANES].astype(BF16), waup_ref[...]) + ba_ref[...]
        log_a = _log_sigmoid(a_pre) * (1.0 / GLA_TAU)
        b = _split_dot(tri, log_a)
        b_mid = b[mid:mid + 1, :]
        b_last = b[CHUNK - 1:CHUNK, :]
        q = z_ref[rows, C_Q:C_Q + QK_WIDTH] * (GLA_DK ** -0.5)
        k = z_ref[rows, C_K:C_K + QK_WIDTH]
        q_in = (q * jnp.exp(b - b_mid)).astype(BF16)
        k_in = (k * jnp.exp(b_mid - b)).astype(BF16)
        q_st = (q * jnp.exp(b)).astype(BF16)
        k_st = (k * jnp.exp(b_last - b)).astype(BF16)
        d_last = jnp.exp(b_last)
        va = z_ref[rows, C_VA:C_VA + GLA_WIDTH].astype(BF16)
        for h in range(GLA_HEADS):
            ps = slice((h // 2) * LANES, (h // 2 + 1) * LANES)
            vs = slice(h * GLA_DV, (h + 1) * GLA_DV)
            hm = head_lanes[h % 2]
            zero = jnp.zeros((), BF16)
            a = _dot_nt(jnp.where(hm, q_in[:, ps], zero), k_in[:, ps])
            a = jnp.where(causal, a, 0.0).astype(BF16)
            st = st_ref[h]
            o = _dot(a, va[:, vs]) + _dot_nt(jnp.where(hm, q_st[:, ps], zero), st.astype(BF16))
            upd = _dot_tn(va[:, vs], jnp.where(hm, k_st[:, ps], zero))
            st_ref[h] = st * d_last[:, ps] + upd
            gate = z_ref[rows, C_G + h * GLA_DV:C_G + (h + 1) * GLA_DV]
            o = _rmsnorm(o, glag_ref[h:h + 1, :]) * _silu(gate)
            merged_ref[rows, vs] = o.astype(BF16)
        u = _gelu(z_ref[rows, C_U:C_U + GMLP_WIDTH])
        vg = _layernorm(_gelu(z_ref[rows, C_VB:C_VB + GMLP_WIDTH]), lng_ref[...], lnb_ref[...])
        _gmlp_heads(u, vg, lambda h: jnp.where(causal, ws_ref[h], 0.0).astype(BF16), bsf_ref[...],
                    gmg_ref, merged_ref, rows)

    h_ref[0] = _out_proj_ln(x, merged_ref, wout_ref, ln1g_ref, ln1b_ref)

    @pl.when(t == pl.num_programs(1) - 1)
    def _():
        for h in range(GLA_HEADS):
            lo = (h % 2) * GLA_DK
            state_ref[0, h] = st_ref[h].T[lo:lo + GLA_DK, :]


def _mixer_sample_kernel(seq_len, x_ref, s0_ref, win_ref, wkt_ref, walrt_ref, waup_ref, waupt_ref, ba_ref, bac_ref,
                         glag_ref, lng_ref, lnb_ref, wst_ref, bsf_ref, gmg_ref, wout_ref, ln1g_ref, ln1b_ref,
                         h_ref, snew_ref, vg_ref,
                         merged_ref):
    n = SAMPLE_SEQS * seq_len
    x = x_ref[...]
    xb = x.astype(BF16)
    z = _dot(xb, win_ref[...])

    ti = lax.broadcasted_iota(jnp.int32, (n, n), 0)
    tj = lax.broadcasted_iota(jnp.int32, (n, n), 1)
    same = _shr(ti, seq_len) == _shr(tj, seq_len)
    causal = jnp.logical_and(same, ti >= tj)
    tri = jnp.where(causal, 1.0, 0.0).astype(BF16)
    tri_t = jnp.where(jnp.logical_and(same, ti <= tj), 1.0, 0.0).astype(BF16)
    same01 = jnp.where(same, 1.0, 0.0).astype(BF16)

    a_pre = _dot(z[:, C_ALR:C_ALR + LANES].astype(BF16), waup_ref[...]) + ba_ref[...]
    log_a = _log_sigmoid(a_pre) * (1.0 / GLA_TAU)
    b = _split_dot(tri, log_a)
    q = z[:, C_Q:C_Q + QK_WIDTH] * (GLA_DK ** -0.5)
    k = z[:, C_K:C_K + QK_WIDTH]
    q_in = (q * jnp.exp(b)).astype(BF16)
    k_in = (k * jnp.exp(-b)).astype(BF16)
    va = z[:, C_VA:C_VA + GLA_WIDTH].astype(BF16)

    k_t = _dot_nt(wkt_ref[...], xb)
    alr_t = _dot_nt(walrt_ref[...], xb)
    a_pre_t = _dot(waupt_ref[...], alr_t.astype(BF16)) + bac_ref[...]
    log_a_t = _log_sigmoid(a_pre_t) * (1.0 / GLA_TAU)
    b_t = _split_dot_r(log_a_t, tri_t)
    tot_t = _split_dot_r(log_a_t, same01)
    k_st_t = k_t * jnp.exp(tot_t - b_t)
    d_t = jnp.exp(tot_t)

    lane = lax.broadcasted_iota(jnp.int32, (1, LANES), 1)
    head_lanes = [lane < GLA_DK, lane >= GLA_DK]
    nrow = SAMPLE_SEQS * GLA_DK
    r_seq = _shr(lax.broadcasted_iota(jnp.int32, (nrow, n), 0), GLA_DK)
    c_seq = _shr(lax.broadcasted_iota(jnp.int32, (nrow, n), 1), seq_len)
    c_first = (lax.broadcasted_iota(jnp.int32, (nrow, n), 1) & (seq_len - 1)) == 0
    blk = r_seq == c_seq
    blk_first = jnp.logical_and(blk, c_first)
    q_seq = _shr(lax.broadcasted_iota(jnp.int32, (n, nrow), 0), seq_len)
    q_col = _shr(lax.broadcasted_iota(jnp.int32, (n, nrow), 1), GLA_DK)
    blk_q = q_seq == q_col

    for h in range(GLA_HEADS):
        ps = slice((h // 2) * LANES, (h // 2 + 1) * LANES)
        ds_ = slice(h * GLA_DK, (h + 1) * GLA_DK)
        vs = slice(h * GLA_DV, (h + 1) * GLA_DV)
        hm = head_lanes[h % 2]
        zero = jnp.zeros((), BF16)
        a = _dot_nt(jnp.where(hm, q_in[:, ps], zero), k_in[:, ps])
        a = jnp.where(causal, a, 0.0).astype(BF16)
        s0 = s0_ref[:, h].reshape(nrow, GLA_DV)
        q_h = q_in[:, ds_]
        q_bd = jnp.where(blk_q, jnp.concatenate([q_h] * SAMPLE_SEQS, axis=1), zero)
        o = _dot(a, va[:, vs]) + _dot(q_bd, s0.astype(BF16))
        k_bd = jnp.where(blk, jnp.concatenate([k_st_t[ds_, :]] * SAMPLE_SEQS, axis=0), 0.0).astype(BF16)
        upd = _dot(k_bd, va[:, vs])
        d_bd = jnp.where(blk_first, jnp.concatenate([d_t[ds_, :]] * SAMPLE_SEQS, axis=0), 0.0)
        d_col = jnp.sum(d_bd, axis=1, keepdims=True)
        snew_ref[:, h] = (s0 * d_col + upd).reshape(SAMPLE_SEQS, GLA_DK, GLA_DV)
        gate = z[:, C_G + h * GLA_DV:C_G + (h + 1) * GLA_DV]
        o = _rmsnorm(o, glag_ref[h:h + 1, :]) * _silu(gate)
        merged_ref[:, vs] = o.astype(BF16)

    u = _gelu(z[:, C_U:C_U + GMLP_WIDTH])
    vg = _layernorm(_gelu(z[:, C_VB:C_VB + GMLP_WIDTH]), lng_ref[...], lnb_ref[...])
    vg_ref[...] = vg
    _gmlp_heads(u, vg, lambda h: jnp.where(causal, wst_ref[h], 0.0).astype(BF16), bsf_ref[...],
                gmg_ref, merged_ref, slice(None))
    h_ref[...] = _out_proj_ln(x, merged_ref, wout_ref, ln1g_ref, ln1b_ref)


def _route(hb, wrt_ref, rbias_ref):
    tm = hb.shape[0]
    s = _sigmoid(_dot_nt(wrt_ref[...], hb))
    sb = s + rbias_ref[...]
    neg = jnp.float32(-jnp.inf)
    sub = lax.broadcasted_iota(jnp.int32, (GROUP_SIZE, tm), 0)
    gscore = []
    for g in range(N_GROUPS):
        blk = sb[g * GROUP_SIZE:(g + 1) * GROUP_SIZE, :]
        m1 = jnp.max(blk, axis=0, keepdims=True)
        i1 = jnp.min(jnp.where(blk == m1, sub, GROUP_SIZE), axis=0, keepdims=True)
        m2 = jnp.max(jnp.where(sub == i1, neg, blk), axis=0, keepdims=True)
        gscore.append(m1 + m2)
    gsel = []
    for g in range(N_GROUPS):
        rank = jnp.zeros((1, tm), jnp.int32)
        for o in range(N_GROUPS):
            if o == g:
                continue
            ahead = (gscore[o] >= gscore[g]) if o < g else (gscore[o] > gscore[g])
            rank = rank + ahead.astype(jnp.int32)
        gsel.append(rank < TOPK_GROUPS)
    v = jnp.concatenate(
        [jnp.where(gsel[g], sb[g * GROUP_SIZE:(g + 1) * GROUP_SIZE, :], neg) for g in range(N_GROUPS)], axis=0)
    eidx = lax.broadcasted_iota(jnp.int32, (N_EXPERTS, tm), 0)
    sel = jnp.zeros((N_EXPERTS, tm), F32)
    for _ in range(TOP_K):
        m = jnp.max(v, axis=0, keepdims=True)
        first = jnp.min(jnp.where(v == m, eidx, N_EXPERTS), axis=0, keepdims=True)
        hit = eidx == first
        sel = jnp.where(hit, 1.0, sel)
        v = jnp.where(hit, neg, v)
    wsel = s * sel
    return wsel / jnp.sum(wsel, axis=0, keepdims=True) * ROUTE_SCALE


def _moe_kernel(h_ref, wrt_ref, rbias_ref, wg_ref, wu_ref, wd_ref, wsg_ref, wsu_ref, wsd_ref, ln2g_ref, ln2b_ref,
                y_ref,
                hb_ref, comb_ref, acc_ref):
    e = pl.program_id(1)

    @pl.when(e == 0)
    def _():
        hb = h_ref[...].astype(BF16)
        hb_ref[...] = hb
        comb = _route(hb, wrt_ref, rbias_ref)
        comb = jnp.concatenate([comb, jnp.zeros_like(comb)], axis=0)
        comb_ref[...] = comb.T
        shared = _silu(_dot(hb, wsg_ref[...])) * _dot(hb, wsu_ref[...])
        acc_ref[...] = _dot(shared.astype(BF16), wsd_ref[...])

    hb = hb_ref[...]
    lane = lax.broadcasted_iota(jnp.int32, (1, LANES), 1)
    col = jnp.sum(jnp.where(lane == e, comb_ref[...], 0.0), axis=1, keepdims=True)
    hid = _silu(_dot(hb, wg_ref[0])) * _dot(hb, wu_ref[0])
    hid = hid * col
    acc_ref[...] += _dot(hid.astype(BF16), wd_ref[0])

    @pl.when(e == pl.num_programs(1) - 1)
    def _():
        y_ref[...] = _layernorm(ALPHA * h_ref[...] + acc_ref[...], ln2g_ref[...], ln2b_ref[...])


def _full(shape):
    return pl.BlockSpec(shape, lambda *_: (0,) * len(shape))


def _mixer_prompt(x, wts):
    bsz, seq, _ = x.shape
    n_t = seq // PROMPT_TILE
    weights = [wts[k] for k in ("win", "waup", "ba", "glag", "lng", "lnb", "ws", "bsf_prompt", "gmg", "wout",
                                "ln1g", "ln1b")]
    return pl.pallas_call(
        _mixer_prompt_kernel,
        grid=(bsz, n_t),
        in_specs=[pl.BlockSpec((1, PROMPT_TILE, D_MODEL), lambda b, t: (b, t, 0))]
        + [_full(w.shape) for w in weights],
        out_specs=[pl.BlockSpec((1, PROMPT_TILE, D_MODEL), lambda b, t: (b, t, 0)),
                   pl.BlockSpec((1, GLA_HEADS, GLA_DK, GLA_DV), lambda b, t: (b, 0, 0, 0))],
        out_shape=[jax.ShapeDtypeStruct((bsz, seq, D_MODEL), F32),
                   jax.ShapeDtypeStruct((bsz, GLA_HEADS, GLA_DK, GLA_DV), F32)],
        scratch_shapes=[pltpu.VMEM((PROMPT_TILE, N_PROJ), F32),
                        pltpu.VMEM((PROMPT_TILE, D_MODEL), BF16),
                        pltpu.VMEM((GLA_HEADS, GLA_DV, LANES), F32)],
        compiler_params=pltpu.CompilerParams(dimension_semantics=("arbitrary", "arbitrary"),
                                             vmem_limit_bytes=VMEM_LIMIT),
        name="mixer_prompt",
    )(x, *weights)


def _mixer_sample(x, s0, wts):
    bsz, seq_len, _ = x.shape
    n = SAMPLE_SEQS * seq_len
    x2 = x.reshape(bsz * seq_len, D_MODEL)
    weights = [wts[k] for k in ("win", "wkt", "walrt", "waup", "waupt", "ba", "bac", "glag", "lng", "lnb",
                                "ws_sample", "bsf_sample", "gmg", "wout", "ln1g", "ln1b")]
    state_spec = pl.BlockSpec((SAMPLE_SEQS, GLA_HEADS, GLA_DK, GLA_DV), lambda i: (i, 0, 0, 0))
    h, s_new, vg = pl.pallas_call(
        functools.partial(_mixer_sample_kernel, seq_len),
        grid=(bsz // SAMPLE_SEQS,),
        in_specs=[pl.BlockSpec((n, D_MODEL), lambda i: (i, 0)), state_spec] + [_full(w.shape) for w in weights],
        out_specs=[pl.BlockSpec((n, D_MODEL), lambda i: (i, 0)), state_spec,
                   pl.BlockSpec((n, GMLP_WIDTH), lambda i: (i, 0))],
        out_shape=[jax.ShapeDtypeStruct((bsz * seq_len, D_MODEL), F32),
                   jax.ShapeDtypeStruct(s0.shape, F32),
                   jax.ShapeDtypeStruct((bsz * seq_len, GMLP_WIDTH), F32)],
        scratch_shapes=[pltpu.VMEM((n, D_MODEL), BF16)],
        compiler_params=pltpu.CompilerParams(dimension_semantics=("arbitrary",), vmem_limit_bytes=VMEM_LIMIT),
        name="mixer_sample",
    )(x2, s0, *weights)
    return h, s_new, vg.reshape(bsz, seq_len, GMLP_WIDTH)


def _moe(h, wts):
    n = h.shape[0]
    tm = min(MOE_TILE, n)
    weights = [wts[k] for k in ("wsg", "wsu", "wsd", "ln2g", "ln2b")]
    return pl.pallas_call(
        _moe_kernel,
        grid=(n // tm, N_EXPERTS),
        in_specs=[pl.BlockSpec((tm, D_MODEL), lambda i, e: (i, 0)),
                  _full(wts["wrt"].shape), _full(wts["rbias"].shape),
                  pl.BlockSpec((1, D_MODEL, D_EXPERT), lambda i, e: (e, 0, 0)),
                  pl.BlockSpec((1, D_MODEL, D_EXPERT), lambda i, e: (e, 0, 0)),
                  pl.BlockSpec((1, D_EXPERT, D_MODEL), lambda i, e: (e, 0, 0))]
        + [_full(w.shape) for w in weights],
        out_specs=pl.BlockSpec((tm, D_MODEL), lambda i, e: (i, 0)),
        out_shape=jax.ShapeDtypeStruct((n, D_MODEL), F32),
        scratch_shapes=[pltpu.VMEM((tm, D_MODEL), BF16),
                        pltpu.VMEM((tm, LANES), F32),
                        pltpu.VMEM((tm, D_MODEL), F32)],
        compiler_params=pltpu.CompilerParams(dimension_semantics=("arbitrary", "arbitrary"),
                                             vmem_limit_bytes=VMEM_LIMIT),
        name="moe",
    )(h, wts["wrt"], wts["rbias"], wts["wg"], wts["wu"], wts["wd"], *weights)


def _prep_weights(seq_len, w_in, w_a_up, b_a, gla_norm_g, gmlp_ln_g, gmlp_ln_b, w_s, b_s, gmlp_norm_g, w_out,
                  ln1_g, ln1_b, w_router, router_bias, w_gate, w_up, w_down, ws_gate, ws_up, ws_down, ln2_g, ln2_b):
    o1 = QK_WIDTH
    o2 = o1 + QK_WIDTH
    o3 = o2 + GLA_WIDTH
    o4 = o3 + GLA_WIDTH
    o5 = o4 + GLA_RANK
    o6 = o5 + GMLP_WIDTH
    wq, wk, wva, wg_, walr, wu_, wvb = jnp.split(w_in, [o1, o2, o3, o4, o5, o6], axis=-1)
    walr_p = jnp.pad(walr, ((0, 0), (0, LANES - GLA_RANK)))
    waup_p = jnp.pad(w_a_up, ((0, LANES - GLA_RANK), (0, 0)))
    row = lambda a: a.reshape(1, -1)
    reps = (SAMPLE_SEQS * seq_len) // seq_len
    ws_small = w_s[:, :seq_len, :seq_len]
    return {
        "win": jnp.concatenate([wq, wk, wva, wg_, wu_, wvb, walr_p], axis=-1).astype(BF16),
        "wkt": wk.T.astype(BF16),
        "walrt": walr_p.T.astype(BF16),
        "waup": waup_p.astype(BF16),
        "waupt": waup_p.T.astype(BF16),
        "ba": row(b_a), "bac": b_a.reshape(-1, 1),
        "glag": gla_norm_g, "lng": row(gmlp_ln_g), "lnb": row(gmlp_ln_b),
        "ws": w_s,
        "ws_sample": jnp.tile(ws_small, (1, reps, reps)),
        "bsf_prompt": jnp.repeat(b_s[:, :GMLP_CHUNK].T, GMLP_DH, axis=1),
        "bsf_sample": jnp.tile(jnp.repeat(b_s[:, :seq_len].T, GMLP_DH, axis=1), (reps, 1)),
        "gmg": gmlp_norm_g,
        "wout": w_out.astype(BF16),
        "ln1g": row(ln1_g), "ln1b": row(ln1_b),
        "wrt": w_router.T.astype(BF16), "rbias": router_bias.reshape(-1, 1),
        "wg": w_gate.astype(BF16), "wu": w_up.astype(BF16), "wd": w_down.astype(BF16),
        "wsg": ws_gate.astype(BF16), "wsu": ws_up.astype(BF16), "wsd": ws_down.astype(BF16),
        "ln2g": row(ln2_g), "ln2b": row(ln2_b),
    }


def kernel(x_prompt, x_sample, state_gla, w_in, w_a_up, b_a, gla_norm_g, gmlp_ln_g, gmlp_ln_b, w_s, b_s,
           gmlp_norm_g, w_out, ln1_g, ln1_b, w_router, router_bias, w_gate, w_up, w_down, ws_gate, ws_up,
           ws_down, ln2_g, ln2_b):
    assert x_prompt.shape[1] % PROMPT_TILE == 0 and x_sample.shape[0] % SAMPLE_SEQS == 0
    assert x_sample.shape[1] <= GMLP_CHUNK and w_in.shape[0] == DEPTH
    bsz, seq, _ = x_prompt.shape
    dbsz, dseq, _ = x_sample.shape
    hp, hs = x_prompt, x_sample
    gla_p, gla_s, v_s = [], [], []
    for l in range(DEPTH):
        wts = _prep_weights(dseq, w_in[l], w_a_up[l], b_a[l], gla_norm_g[l], gmlp_ln_g[l], gmlp_ln_b[l], w_s[l],
                            b_s[l], gmlp_norm_g[l], w_out[l], ln1_g[l], ln1_b[l], w_router[l], router_bias[l],
                            w_gate[l], w_up[l], w_down[l], ws_gate[l], ws_up[l], ws_down[l], ln2_g[l], ln2_b[l])
        h_p, sp = _mixer_prompt(hp, wts)
        h_s, ss, vrows = _mixer_sample(hs, state_gla[l], wts)
        hp = _moe(h_p.reshape(bsz * seq, D_MODEL), wts).reshape(bsz, seq, D_MODEL)
        hs = _moe(h_s, wts).reshape(dbsz, dseq, D_MODEL)
        gla_p.append(sp)
        gla_s.append(ss)
        v_s.append(vrows)
    return (hp, hs, jnp.stack(gla_p), jnp.stack(gla_s), jnp.stack(v_s))
```

```python
import functools
import math

import jax
import jax.numpy as jnp
from jax import lax
from jax.experimental import pallas as pl
from jax.experimental.pallas import tpu as pltpu

F32 = jnp.float32
BF16 = jnp.bfloat16

D_MODEL = 1024
DEPTH = 1
GLA_WIDTH = 512
GLA_HEADS = 4
GLA_DK = 64
GLA_DV = 128
GLA_RANK = 16
GLA_TAU = 16.0
GMLP_WIDTH = 512
GMLP_HEADS = 4
GMLP_DH = 128
GMLP_CHUNK = 128
QK_WIDTH = GLA_HEADS * GLA_DK
N_EXPERTS = 64
TOP_K = 8
N_GROUPS = 8
GROUP_SIZE = N_EXPERTS // N_GROUPS
TOPK_GROUPS = 4
D_EXPERT = 256
D_SHARED = 256
ROUTE_SCALE = 2.5
ALPHA = (2.0 * DEPTH) ** 0.25

LANES = 128

C_Q = 0
C_K = C_Q + QK_WIDTH
C_VA = C_K + QK_WIDTH
C_G = C_VA + GLA_WIDTH
C_U = C_G + GLA_WIDTH
C_VB = C_U + GMLP_WIDTH
C_ALR = C_VB + GMLP_WIDTH
N_PROJ = C_ALR + LANES

CHUNK = 128
PROMPT_TILE = 256
SAMPLE_SEQS = 32
VMEM_LIMIT = 56 * 1024 * 1024

BLOCK = 256
SEG = 16
TILE_CHUNKS = 16
TILE_ROWS = TILE_CHUNKS * SEG
BLOCK_ROWS_USED = BLOCK * TOP_K + N_EXPERTS * (SEG - 1)
BLOCK_ROWS = -(-BLOCK_ROWS_USED // TILE_ROWS) * TILE_ROWS
DUMP_BLOCKS = -(-(N_EXPERTS * TILE_ROWS) // BLOCK_ROWS)
GATHER_ROWS = 512
META_W = 32
META_EXPERT = TILE_CHUNKS
META_FIRST = TILE_CHUNKS + 1
META_NREAL = TILE_CHUNKS + 2


def _dot(a, b):
    return jnp.dot(a, b, preferred_element_type=F32)


def _dot_nt(a, b):
    return lax.dot_general(a, b, (((1,), (1,)), ((), ())), preferred_element_type=F32)


def _dot_tn(a, b):
    return lax.dot_general(a, b, (((0,), (0,)), ((), ())), preferred_element_type=F32)


def _shr(x, d):
    assert d & (d - 1) == 0
    return lax.shift_right_logical(x, d.bit_length() - 1)


def _split_dot(m01, x):
    hi = x.astype(BF16)
    lo = (x - hi.astype(F32)).astype(BF16)
    return _dot(m01, hi) + _dot(m01, lo)


def _split_dot_r(x, m01):
    hi = x.astype(BF16)
    lo = (x - hi.astype(F32)).astype(BF16)
    return _dot(hi, m01) + _dot(lo, m01)


def _sigmoid(x):
    return 1.0 / (1.0 + jnp.exp(-x))


def _silu(x):
    return x * _sigmoid(x)


def _gelu(x):
    c = math.sqrt(2.0 / math.pi)
    return x * (0.5 * (1.0 + jnp.tanh(c * (x + 0.044715 * (x * x * x)))))


def _log_sigmoid(x):
    return -(jnp.maximum(-x, 0.0) + jnp.log1p(jnp.exp(-jnp.abs(x))))


def _layernorm(x, g, b, eps=1e-5):
    mu = jnp.mean(x, axis=-1, keepdims=True)
    xc = x - mu
    var = jnp.mean(xc * xc, axis=-1, keepdims=True)
    return xc * lax.rsqrt(var + eps) * g + b


def _rmsnorm(x, g, eps=1e-6):
    return x * lax.rsqrt(jnp.mean(x * x, axis=-1, keepdims=True) + eps) * g


def _gmlp_heads(u, vg, w_mix, bias, gmg_ref, merged_ref, rows):
    vgb = vg.astype(BF16)
    for h in range(GMLP_HEADS):
        cs = slice(h * GMLP_DH, (h + 1) * GMLP_DH)
        sgu = _dot(w_mix(h), vgb[:, cs]) + bias[:, cs]
        y = _rmsnorm(u[:, cs] * sgu, gmg_ref[h:h + 1, :])
        merged_ref[rows, GLA_WIDTH + h * GMLP_DH:GLA_WIDTH + (h + 1) * GMLP_DH] = y.astype(BF16)


def _out_proj_ln(x, merged_ref, wout_ref, ln1g_ref, ln1b_ref):
    m = _dot(merged_ref[...], wout_ref[...])
    return _layernorm(ALPHA * x + m, ln1g_ref[...], ln1b_ref[...])


def _mixer_prompt_kernel(x_ref, win_ref, waup_ref, ba_ref, glag_ref, lng_ref, lnb_ref, ws_ref, bsf_ref,
                         gmg_ref, wout_ref, ln1g_ref, ln1b_ref,
                         h_ref, state_ref,
                         z_ref, merged_ref, st_ref):
    t = pl.program_id(1)

    @pl.when(t == 0)
    def _():
        st_ref[...] = jnp.zeros_like(st_ref)

    x = x_ref[0]
    z_ref[...] = _dot(x.astype(BF16), win_ref[...])

    row_i = lax.broadcasted_iota(jnp.int32, (CHUNK, CHUNK), 0)
    col_i = lax.broadcasted_iota(jnp.int32, (CHUNK, CHUNK), 1)
    causal = row_i >= col_i
    tri = jnp.where(causal, 1.0, 0.0).astype(BF16)
    lane = lax.broadcasted_iota(jnp.int32, (1, LANES), 1)
    head_lanes = [lane < GLA_DK, lane >= GLA_DK]
    mid = CHUNK // 2 - 1

    for c in range(PROMPT_TILE // CHUNK):
        rows = slice(c * CHUNK, (c + 1) * CHUNK)
        a_pre = _dot(z_ref[rows, C_ALR:C_ALR + LANES].astype(BF16), waup_ref[...]) + ba_ref[...]
        log_a = _log_sigmoid(a_pre) * (1.0 / GLA_TAU)
        b = _split_dot(tri, log_a)
        b_mid = b[mid:mid + 1, :]
        b_last = b[CHUNK - 1:CHUNK, :]
        q = z_ref[rows, C_Q:C_Q + QK_WIDTH] * (GLA_DK ** -0.5)
        k = z_ref[rows, C_K:C_K + QK_WIDTH]
        q_in = (q * jnp.exp(b - b_mid)).astype(BF16)
        k_in = (k * jnp.exp(b_mid - b)).astype(BF16)
        q_st = (q * jnp.exp(b)).astype(BF16)
        k_st = (k * jnp.exp(b_last - b)).astype(BF16)
        d_last = jnp.exp(b_last)
        va = z_ref[rows, C_VA:C_VA + GLA_WIDTH].astype(BF16)
        for h in range(GLA_HEADS):
            ps = slice((h // 2) * LANES, (h // 2 + 1) * LANES)
            vs = slice(h * GLA_DV, (h + 1) * GLA_DV)
            hm = head_lanes[h % 2]
            zero = jnp.zeros((), BF16)
            a = _dot_nt(jnp.where(hm, q_in[:, ps], zero), k_in[:, ps])
            a = jnp.where(causal, a, 0.0).astype(BF16)
            st = st_ref[h]
            o = _dot(a, va[:, vs]) + _dot_nt(jnp.where(hm, q_st[:, ps], zero), st.astype(BF16))
            upd = _dot_tn(va[:, vs], jnp.where(hm, k_st[:, ps], zero))
            st_ref[h] = st * d_last[:, ps] + upd
            gate = z_ref[rows, C_G + h * GLA_DV:C_G + (h + 1) * GLA_DV]
            o = _rmsnorm(o, glag_ref[h:h + 1, :]) * _silu(gate)
            merged_ref[rows, vs] = o.astype(BF16)
        u = _gelu(z_ref[rows, C_U:C_U + GMLP_WIDTH])
        vg = _layernorm(_gelu(z_ref[rows, C_VB:C_VB + GMLP_WIDTH]), lng_ref[...], lnb_ref[...])
        _gmlp_heads(u, vg, lambda h: jnp.where(causal, ws_ref[h], 0.0).astype(BF16), bsf_ref[...],
                    gmg_ref, merged_ref, rows)

    h_ref[0] = _out_proj_ln(x, merged_ref, wout_ref, ln1g_ref, ln1b_ref)

    @pl.when(t == pl.num_programs(1) - 1)
    def _():
        for h in range(GLA_HEADS):
            lo = (h % 2) * GLA_DK
            state_ref[0, h] = st_ref[h].T[lo:lo + GLA_DK, :]


def _mixer_sample_kernel(seq_len, x_ref, s0_ref, win_ref, wkt_ref, walrt_ref, waup_ref, waupt_ref, ba_ref, bac_ref,
                         glag_ref, lng_ref, lnb_ref, wst_ref, bsf_ref, gmg_ref, wout_ref, ln1g_ref, ln1b_ref,
                         h_ref, snew_ref, vg_ref,
                         merged_ref):
    n = SAMPLE_SEQS * seq_len
    x = x_ref[...]
    xb = x.astype(BF16)
    z = _dot(xb, win_ref[...])

    ti = lax.broadcasted_iota(jnp.int32, (n, n), 0)
    tj = lax.broadcasted_iota(jnp.int32, (n, n), 1)
    same = _shr(ti, seq_len) == _shr(tj, seq_len)
    causal = jnp.logical_and(same, ti >= tj)
    tri = jnp.where(causal, 1.0, 0.0).astype(BF16)
    tri_t = jnp.where(jnp.logical_and(same, ti <= tj), 1.0, 0.0).astype(BF16)
    same01 = jnp.where(same, 1.0, 0.0).astype(BF16)

    a_pre = _dot(z[:, C_ALR:C_ALR + LANES].astype(BF16), waup_ref[...]) + ba_ref[...]
    log_a = _log_sigmoid(a_pre) * (1.0 / GLA_TAU)
    b = _split_dot(tri, log_a)
    q = z[:, C_Q:C_Q + QK_WIDTH] * (GLA_DK ** -0.5)
    k = z[:, C_K:C_K + QK_WIDTH]
    q_in = (q * jnp.exp(b)).astype(BF16)
    k_in = (k * jnp.exp(-b)).astype(BF16)
    va = z[:, C_VA:C_VA + GLA_WIDTH].astype(BF16)

    k_t = _dot_nt(wkt_ref[...], xb)
    alr_t = _dot_nt(walrt_ref[...], xb)
    a_pre_t = _dot(waupt_ref[...], alr_t.astype(BF16)) + bac_ref[...]
    log_a_t = _log_sigmoid(a_pre_t) * (1.0 / GLA_TAU)
    b_t = _split_dot_r(log_a_t, tri_t)
    tot_t = _split_dot_r(log_a_t, same01)
    k_st_t = k_t * jnp.exp(tot_t - b_t)
    d_t = jnp.exp(tot_t)

    lane = lax.broadcasted_iota(jnp.int32, (1, LANES), 1)
    head_lanes = [lane < GLA_DK, lane >= GLA_DK]
    nrow = SAMPLE_SEQS * GLA_DK
    r_seq = _shr(lax.broadcasted_iota(jnp.int32, (nrow, n), 0), GLA_DK)
    c_seq = _shr(lax.broadcasted_iota(jnp.int32, (nrow, n), 1), seq_len)
    c_first = (lax.broadcasted_iota(jnp.int32, (nrow, n), 1) & (seq_len - 1)) == 0
    blk = r_seq == c_seq
    blk_first = jnp.logical_and(blk, c_first)
    q_seq = _shr(lax.broadcasted_iota(jnp.int32, (n, nrow), 0), seq_len)
    q_col = _shr(lax.broadcasted_iota(jnp.int32, (n, nrow), 1), GLA_DK)
    blk_q = q_seq == q_col

    for h in range(GLA_HEADS):
        ps = slice((h // 2) * LANES, (h // 2 + 1) * LANES)
        ds_ = slice(h * GLA_DK, (h + 1) * GLA_DK)
        vs = slice(h * GLA_DV, (h + 1) * GLA_DV)
        hm = head_lanes[h % 2]
        zero = jnp.zeros((), BF16)
        a = _dot_nt(jnp.where(hm, q_in[:, ps], zero), k_in[:, ps])
        a = jnp.where(causal, a, 0.0).astype(BF16)
        s0 = s0_ref[:, h].reshape(nrow, GLA_DV)
        q_h = q_in[:, ds_]
        q_bd = jnp.where(blk_q, jnp.concatenate([q_h] * SAMPLE_SEQS, axis=1), zero)
        o = _dot(a, va[:, vs]) + _dot(q_bd, s0.astype(BF16))
        k_bd = jnp.where(blk, jnp.concatenate([k_st_t[ds_, :]] * SAMPLE_SEQS, axis=0), 0.0).astype(BF16)
        upd = _dot(k_bd, va[:, vs])
        d_bd = jnp.where(blk_first, jnp.concatenate([d_t[ds_, :]] * SAMPLE_SEQS, axis=0), 0.0)
        d_col = jnp.sum(d_bd, axis=1, keepdims=True)
        snew_ref[:, h] = (s0 * d_col + upd).reshape(SAMPLE_SEQS, GLA_DK, GLA_DV)
        gate = z[:, C_G + h * GLA_DV:C_G + (h + 1) * GLA_DV]
        o = _rmsnorm(o, glag_ref[h:h + 1, :]) * _silu(gate)
        merged_ref[:, vs] = o.astype(BF16)

    u = _gelu(z[:, C_U:C_U + GMLP_WIDTH])
    vg = _layernorm(_gelu(z[:, C_VB:C_VB + GMLP_WIDTH]), lng_ref[...], lnb_ref[...])
    vg_ref[...] = vg
    _gmlp_heads(u, vg, lambda h: jnp.where(causal, wst_ref[h], 0.0).astype(BF16), bsf_ref[...],
                gmg_ref, merged_ref, slice(None))
    h_ref[...] = _out_proj_ln(x, merged_ref, wout_ref, ln1g_ref, ln1b_ref)


def _route(hb, wrt_ref, rbias_ref):
    tm = hb.shape[0]
    s = _sigmoid(_dot_nt(wrt_ref[...], hb))
    sb = s + rbias_ref[...]
    neg = jnp.float32(-jnp.inf)
    sub = lax.broadcasted_iota(jnp.int32, (GROUP_SIZE, tm), 0)
    gscore = []
    for g in range(N_GROUPS):
        blk = sb[g * GROUP_SIZE:(g + 1) * GROUP_SIZE, :]
        m1 = jnp.max(blk, axis=0, keepdims=True)
        i1 = jnp.min(jnp.where(blk == m1, sub, GROUP_SIZE), axis=0, keepdims=True)
        m2 = jnp.max(jnp.where(sub == i1, neg, blk), axis=0, keepdims=True)
        gscore.append(m1 + m2)
    gsel = []
    for g in range(N_GROUPS):
        rank = jnp.zeros((1, tm), jnp.int32)
        for o in range(N_GROUPS):
            if o == g:
                continue
            ahead = (gscore[o] >= gscore[g]) if o < g else (gscore[o] > gscore[g])
            rank = rank + ahead.astype(jnp.int32)
        gsel.append(rank < TOPK_GROUPS)
    v = jnp.concatenate(
        [jnp.where(gsel[g], sb[g * GROUP_SIZE:(g + 1) * GROUP_SIZE, :], neg) for g in range(N_GROUPS)], axis=0)
    eidx = lax.broadcasted_iota(jnp.int32, (N_EXPERTS, tm), 0)
    sel = jnp.zeros((N_EXPERTS, tm), F32)
    hits = []
    for _ in range(TOP_K):
        m = jnp.max(v, axis=0, keepdims=True)
        first = jnp.min(jnp.where(v == m, eidx, N_EXPERTS), axis=0, keepdims=True)
        hit = eidx == first
        hits.append(hit)
        sel = jnp.where(hit, 1.0, sel)
        v = jnp.where(hit, neg, v)
    wsel = s * sel
    return wsel / jnp.sum(wsel, axis=0, keepdims=True) * ROUTE_SCALE, sel, hits


def _slot_onehot(pos_rows, val_rows, n_rows):
    tm = pos_rows[0].shape[-1]
    r = lax.broadcasted_iota(jnp.int32, (n_rows, tm), 0).astype(F32)
    out = jnp.zeros((n_rows, tm), F32)
    for p, v in zip(pos_rows, val_rows):
        out = out + jnp.where(r == p, v, 0.0)
    return out


def _dispatch_kernel(n_prompt_blocks, n_blocks, hp_ref, hs_ref, wrt_ref, rbias_ref, x_ref, pos_ref, wt_ref,
                     seg_ref):
    i = pl.program_id(0)

    @pl.when(i < n_blocks)
    def _():
        _dispatch_block(i < n_prompt_blocks, hp_ref, hs_ref, wrt_ref, rbias_ref, x_ref, pos_ref, wt_ref, seg_ref)

    @pl.when(i >= n_blocks)
    def _():
        x_ref[...] = jnp.zeros_like(x_ref)
        pos_ref[...] = jnp.zeros_like(pos_ref)
        wt_ref[...] = jnp.zeros_like(wt_ref)
        seg_ref[...] = jnp.zeros_like(seg_ref)


def _dispatch_block(is_prompt, hp_ref, hs_ref, wrt_ref, rbias_ref, x_ref, pos_ref, wt_ref, seg_ref):
    tm = BLOCK
    hb = jnp.where(is_prompt, hp_ref[...], hs_ref[...]).astype(BF16)
    comb, sel, hits = _route(hb, wrt_ref, rbias_ref)
    ti = lax.broadcasted_iota(jnp.int32, (tm, tm), 0)
    tj = lax.broadcasted_iota(jnp.int32, (tm, tm), 1)
    before = jnp.where(ti < tj, 1.0, 0.0).astype(BF16)
    rank = _dot(sel.astype(BF16), before)
    cnt = jnp.sum(sel, axis=1, keepdims=True)
    seg = jnp.floor((cnt + (SEG - 1.0)) * (1.0 / SEG)) * SEG
    ei = lax.broadcasted_iota(jnp.int32, (N_EXPERTS, N_EXPERTS), 0)
    ej = lax.broadcasted_iota(jnp.int32, (N_EXPERTS, N_EXPERTS), 1)
    below = jnp.where(ej < ei, 1.0, 0.0).astype(BF16)
    seg_b = jnp.broadcast_to(seg, (N_EXPERTS, LANES))
    start = _dot(below, seg_b.astype(BF16))[:, 0:1]
    pos_all = start + rank
    pos_rows = [jnp.sum(jnp.where(h, pos_all, 0.0), axis=0, keepdims=True) for h in hits]
    wt_rows = [jnp.sum(jnp.where(h, comb, 0.0), axis=0, keepdims=True) for h in hits]
    ones = jnp.ones((1, tm), F32)
    gather = _slot_onehot(pos_rows, [ones] * TOP_K, BLOCK_ROWS).astype(BF16)
    for c in range(BLOCK_ROWS // GATHER_ROWS):
        rs = slice(c * GATHER_ROWS, (c + 1) * GATHER_ROWS)
        x_ref[rs, :] = _dot(gather[rs, :], hb).astype(BF16)
    pos_ref[0] = jnp.concatenate(pos_rows, axis=0)
    wt_ref[0] = jnp.concatenate(wt_rows, axis=0)
    seg_ref[0] = seg_b


def _ffn_kernel(dump_base, tile_expert_ref, n_tiles_ref, meta_ref, meta_next_ref, x_hbm, wg_ref, wu_ref, wd_ref,
                o_hbm, ibuf, obuf, wgb, wub, wdb, in_sem, out_sem):
    del tile_expert_ref
    i = pl.program_id(0)
    n_tiles = n_tiles_ref[0]

    def in_copy(m_ref, j, slot):
        src = pl.multiple_of(m_ref[0, 0, j], SEG)
        return pltpu.make_async_copy(x_hbm.at[pl.ds(src, SEG)], ibuf.at[slot, pl.ds(j * SEG, SEG)], in_sem.at[slot])

    def out_copy(j, slot):
        src = meta_ref[0, 0, j]
        dump = dump_base + (meta_ref[0, 0, META_EXPERT] * TILE_CHUNKS + j) * SEG
        dst = pl.multiple_of(jnp.where(j < meta_ref[0, 0, META_NREAL], src, dump), SEG)
        return pltpu.make_async_copy(obuf.at[slot, pl.ds(j * SEG, SEG)], o_hbm.at[pl.ds(dst, SEG)], out_sem.at[slot])

    def out_wait(slot):
        for j in range(TILE_CHUNKS):
            pltpu.make_async_copy(obuf.at[slot, pl.ds(j * SEG, SEG)], o_hbm.at[pl.ds(0, SEG)], out_sem.at[slot]).wait()

    @pl.when(i < n_tiles)
    def _():
        slot = lax.rem(i, 2)

        @pl.when(i == 0)
        def _():
            for j in range(TILE_CHUNKS):
                in_copy(meta_ref, j, 0).start()

        @pl.when(i + 1 < n_tiles)
        def _():
            for j in range(TILE_CHUNKS):
                in_copy(meta_next_ref, j, 1 - slot).start()

        for j in range(TILE_CHUNKS):
            in_copy(meta_ref, j, slot).wait()

        @pl.when(meta_ref[0, 0, META_FIRST] == 1)
        def _():
            wgb[...] = wg_ref[0].astype(BF16)
            wub[...] = wu_ref[0].astype(BF16)
            wdb[...] = wd_ref[0].astype(BF16)

        @pl.when(i >= 2)
        def _():
            out_wait(slot)

        x = ibuf[slot]
        hid = _silu(_dot(x, wgb[...])) * _dot(x, wub[...])
        obuf[slot] = _dot(hid.astype(BF16), wdb[...]).astype(BF16)
        for j in range(TILE_CHUNKS):
            out_copy(j, slot).start()

        @pl.when(i == n_tiles - 1)
        def _():
            out_wait(slot)

            @pl.when(i >= 1)
            def _():
                out_wait(1 - slot)


def _combine_kernel(h_ref, o_ref, pos_ref, wt_ref, wsg_ref, wsu_ref, wsd_ref, ln2g_ref, ln2b_ref, y_ref):
    h = h_ref[...]
    hb = h.astype(BF16)
    pos_rows = [pos_ref[0, k:k + 1, :] for k in range(TOP_K)]
    wt_rows = [wt_ref[0, k:k + 1, :] for k in range(TOP_K)]
    scatter = _slot_onehot(pos_rows, wt_rows, BLOCK_ROWS).astype(BF16)
    routed = _dot_tn(scatter, o_ref[...])
    shared = _dot((_silu(_dot(hb, wsg_ref[...])) * _dot(hb, wsu_ref[...])).astype(BF16), wsd_ref[...])
    y_ref[...] = _layernorm(ALPHA * h + (routed + shared), ln2g_ref[...], ln2b_ref[...])


def _full(shape):
    return pl.BlockSpec(shape, lambda *_: (0,) * len(shape))


def _mixer_prompt(x, wts):
    bsz, seq, _ = x.shape
    n_t = seq // PROMPT_TILE
    weights = [wts[k] for k in ("win", "waup", "ba", "glag", "lng", "lnb", "ws", "bsf_prompt", "gmg", "wout",
                                "ln1g", "ln1b")]
    return pl.pallas_call(
        _mixer_prompt_kernel,
        grid=(bsz, n_t),
        in_specs=[pl.BlockSpec((1, PROMPT_TILE, D_MODEL), lambda b, t: (b, t, 0))]
        + [_full(w.shape) for w in weights],
        out_specs=[pl.BlockSpec((1, PROMPT_TILE, D_MODEL), lambda b, t: (b, t, 0)),
                   pl.BlockSpec((1, GLA_HEADS, GLA_DK, GLA_DV), lambda b, t: (b, 0, 0, 0))],
        out_shape=[jax.ShapeDtypeStruct((bsz, seq, D_MODEL), F32),
                   jax.ShapeDtypeStruct((bsz, GLA_HEADS, GLA_DK, GLA_DV), F32)],
        scratch_shapes=[pltpu.VMEM((PROMPT_TILE, N_PROJ), F32),
                        pltpu.VMEM((PROMPT_TILE, D_MODEL), BF16),
                        pltpu.VMEM((GLA_HEADS, GLA_DV, LANES), F32)],
        compiler_params=pltpu.CompilerParams(dimension_semantics=("arbitrary", "arbitrary"),
                                             vmem_limit_bytes=VMEM_LIMIT),
        name="mixer_prompt",
    )(x, *weights)


def _mixer_sample(x, s0, wts):
    bsz, seq_len, _ = x.shape
    n = SAMPLE_SEQS * seq_len
    x2 = x.reshape(bsz * seq_len, D_MODEL)
    weights = [wts[k] for k in ("win", "wkt", "walrt", "waup", "waupt", "ba", "bac", "glag", "lng", "lnb",
                                "ws_sample", "bsf_sample", "gmg", "wout", "ln1g", "ln1b")]
    state_spec = pl.BlockSpec((SAMPLE_SEQS, GLA_HEADS, GLA_DK, GLA_DV), lambda i: (i, 0, 0, 0))
    h, s_new, vg = pl.pallas_call(
        functools.partial(_mixer_sample_kernel, seq_len),
        grid=(bsz // SAMPLE_SEQS,),
        in_specs=[pl.BlockSpec((n, D_MODEL), lambda i: (i, 0)), state_spec] + [_full(w.shape) for w in weights],
        out_specs=[pl.BlockSpec((n, D_MODEL), lambda i: (i, 0)), state_spec,
                   pl.BlockSpec((n, GMLP_WIDTH), lambda i: (i, 0))],
        out_shape=[jax.ShapeDtypeStruct((bsz * seq_len, D_MODEL), F32),
                   jax.ShapeDtypeStruct(s0.shape, F32),
                   jax.ShapeDtypeStruct((bsz * seq_len, GMLP_WIDTH), F32)],
        scratch_shapes=[pltpu.VMEM((n, D_MODEL), BF16)],
        compiler_params=pltpu.CompilerParams(dimension_semantics=("arbitrary",), vmem_limit_bytes=VMEM_LIMIT),
        name="mixer_sample",
    )(x2, s0, *weights)
    return h, s_new, vg.reshape(bsz, seq_len, GMLP_WIDTH)


def _dispatch(h_p, h_s, wts):
    npb, nsb = h_p.shape[0] // BLOCK, h_s.shape[0] // BLOCK
    nb_real = npb + nsb
    nb = nb_real + DUMP_BLOCKS
    slot_spec = pl.BlockSpec((1, TOP_K, BLOCK), lambda i: (i, 0, 0))
    return pl.pallas_call(
        functools.partial(_dispatch_kernel, npb, nb_real),
        grid=(nb,),
        in_specs=[pl.BlockSpec((BLOCK, D_MODEL), lambda i: (jnp.minimum(i, npb - 1), 0)),
                  pl.BlockSpec((BLOCK, D_MODEL), lambda i: (jnp.clip(i - npb, 0, nsb - 1), 0)),
                  _full(wts["wrt"].shape), _full(wts["rbias"].shape)],
        out_specs=[pl.BlockSpec((BLOCK_ROWS, D_MODEL), lambda i: (i, 0)), slot_spec, slot_spec,
                   pl.BlockSpec((1, N_EXPERTS, LANES), lambda i: (i, 0, 0))],
        out_shape=[jax.ShapeDtypeStruct((nb * BLOCK_ROWS, D_MODEL), BF16),
                   jax.ShapeDtypeStruct((nb, TOP_K, BLOCK), F32),
                   jax.ShapeDtypeStruct((nb, TOP_K, BLOCK), F32),
                   jax.ShapeDtypeStruct((nb, N_EXPERTS, LANES), F32)],
        compiler_params=pltpu.CompilerParams(dimension_semantics=("arbitrary",), vmem_limit_bytes=VMEM_LIMIT),
        name="moe_dispatch",
    )(h_p, h_s, wts["wrt"], wts["rbias"])


def _max_tiles(nb):
    return (nb * (BLOCK_ROWS_USED // SEG) + N_EXPERTS * (TILE_CHUNKS - 1)) // TILE_CHUNKS + 1


def _plan(seg_lanes, nb):
    nt_max = _max_tiles(nb)
    seg = seg_lanes[:nb, :, 0].astype(jnp.int32)
    start = jnp.cumsum(seg, axis=1) - seg
    nch = (seg // SEG).T
    real = jnp.sum(nch, axis=1)
    padded = -(-real // TILE_CHUNKS) * TILE_CHUNKS
    ext = jnp.concatenate([nch, (padded - real)[:, None]], axis=1).reshape(-1)
    ends = jnp.cumsum(ext)
    q = jnp.arange(nt_max * TILE_CHUNKS, dtype=jnp.int32)
    idx = jnp.sum((ends[None, :] <= q[:, None]).astype(jnp.int32), axis=1)
    idx = jnp.minimum(idx, ext.shape[0] - 1)
    e, b = idx // (nb + 1), idx % (nb + 1)
    c = q - (ends - ext)[idx]
    is_real = b < nb
    bb = jnp.minimum(b, nb - 1)
    src = jnp.where(is_real, bb * BLOCK_ROWS + start[bb, e] + c * SEG, BLOCK_ROWS_USED)
    src = src.reshape(nt_max, TILE_CHUNKS)
    tile_expert = e.reshape(nt_max, TILE_CHUNKS)[:, 0]
    n_real = jnp.sum(is_real.reshape(nt_max, TILE_CHUNKS).astype(jnp.int32), axis=1)
    first_tile = (jnp.cumsum(padded) - padded) // TILE_CHUNKS
    first = (jnp.arange(nt_max, dtype=jnp.int32) == first_tile[tile_expert]).astype(jnp.int32)
    meta = jnp.concatenate([src, tile_expert[:, None], first[:, None], n_real[:, None],
                            jnp.zeros((nt_max, META_W - TILE_CHUNKS - 3), jnp.int32)], axis=1)
    n_tiles = (jnp.sum(padded) // TILE_CHUNKS).reshape(1)
    return meta.astype(jnp.int32).reshape(nt_max, 1, META_W), tile_expert.astype(jnp.int32), n_tiles


def _ffn(x_rows, nb, meta, tile_expert, n_tiles, w_gate, w_up, w_down):
    nt_max = meta.shape[0]
    meta_spec = lambda off: pl.BlockSpec(
        (1, 1, META_W), lambda i, te, nt: (jnp.minimum(i + off, nt_max - 1), 0, 0), memory_space=pltpu.SMEM)
    w_spec = lambda shape: pl.BlockSpec((1,) + shape, lambda i, te, nt: (te[i], 0, 0))
    assert x_rows.shape[0] - nb * BLOCK_ROWS >= N_EXPERTS * TILE_ROWS
    return pl.pallas_call(
        functools.partial(_ffn_kernel, nb * BLOCK_ROWS),
        grid_spec=pltpu.PrefetchScalarGridSpec(
            num_scalar_prefetch=2,
            grid=(nt_max,),
            in_specs=[meta_spec(0), meta_spec(1), pl.BlockSpec(memory_space=pl.ANY),
                      w_spec((D_MODEL, D_EXPERT)), w_spec((D_MODEL, D_EXPERT)), w_spec((D_EXPERT, D_MODEL))],
            out_specs=pl.BlockSpec(memory_space=pl.ANY),
            scratch_shapes=[pltpu.VMEM((2, TILE_ROWS, D_MODEL), BF16),
                            pltpu.VMEM((2, TILE_ROWS, D_MODEL), BF16),
                            pltpu.VMEM((D_MODEL, D_EXPERT), BF16),
                            pltpu.VMEM((D_MODEL, D_EXPERT), BF16),
                            pltpu.VMEM((D_EXPERT, D_MODEL), BF16),
                            pltpu.SemaphoreType.DMA((2,)),
                            pltpu.SemaphoreType.DMA((2,))]),
        out_shape=jax.ShapeDtypeStruct(x_rows.shape, BF16),
        input_output_aliases={4: 0},
        compiler_params=pltpu.CompilerParams(dimension_semantics=("arbitrary",), vmem_limit_bytes=VMEM_LIMIT),
        name="moe_ffn",
    )(tile_expert, n_tiles, meta, meta, x_rows, w_gate, w_up, w_down)


def _combine(h, o_rows, pos, wt, block_off, wts):
    nblk = h.shape[0] // BLOCK
    weights = [wts[k] for k in ("wsg", "wsu", "wsd", "ln2g", "ln2b")]
    slot_spec = pl.BlockSpec((1, TOP_K, BLOCK), lambda i: (i + block_off, 0, 0))
    return pl.pallas_call(
        _combine_kernel,
        grid=(nblk,),
        in_specs=[pl.BlockSpec((BLOCK, D_MODEL), lambda i: (i, 0)),
                  pl.BlockSpec((BLOCK_ROWS, D_MODEL), lambda i: (i + block_off, 0)),
                  slot_spec, slot_spec] + [_full(w.shape) for w in weights],
        out_specs=pl.BlockSpec((BLOCK, D_MODEL), lambda i: (i, 0)),
        out_shape=jax.ShapeDtypeStruct(h.shape, F32),
        compiler_params=pltpu.CompilerParams(dimension_semantics=("arbitrary",), vmem_limit_bytes=VMEM_LIMIT),
        name="moe_combine",
    )(h, o_rows, pos, wt, *weights)


def _moe(h_p, h_s, w_gate, w_up, w_down, wts):
    nb = (h_p.shape[0] + h_s.shape[0]) // BLOCK
    x_rows, pos, wt, seg_lanes = _dispatch(h_p, h_s, wts)
    meta, tile_expert, n_tiles = _plan(seg_lanes, nb)
    o_rows = _ffn(x_rows, nb, meta, tile_expert, n_tiles, w_gate, w_up, w_down)
    y_p = _combine(h_p, o_rows, pos, wt, 0, wts)
    y_s = _combine(h_s, o_rows, pos, wt, h_p.shape[0] // BLOCK, wts)
    return y_p, y_s


def _prep_weights(seq_len, w_in, w_a_up, b_a, gla_norm_g, gmlp_ln_g, gmlp_ln_b, w_s, b_s, gmlp_norm_g, w_out,
                  ln1_g, ln1_b, w_router, router_bias, ws_gate, ws_up, ws_down, ln2_g, ln2_b):
    o1 = QK_WIDTH
    o2 = o1 + QK_WIDTH
    o3 = o2 + GLA_WIDTH
    o4 = o3 + GLA_WIDTH
    o5 = o4 + GLA_RANK
    o6 = o5 + GMLP_WIDTH
    wq, wk, wva, wg_, walr, wu_, wvb = jnp.split(w_in, [o1, o2, o3, o4, o5, o6], axis=-1)
    walr_p = jnp.pad(walr, ((0, 0), (0, LANES - GLA_RANK)))
    waup_p = jnp.pad(w_a_up, ((0, LANES - GLA_RANK), (0, 0)))
    row = lambda a: a.reshape(1, -1)
    reps = (SAMPLE_SEQS * seq_len) // seq_len
    ws_small = w_s[:, :seq_len, :seq_len]
    return {
        "win": jnp.concatenate([wq, wk, wva, wg_, wu_, wvb, walr_p], axis=-1).astype(BF16),
        "wkt": wk.T.astype(BF16),
        "walrt": walr_p.T.astype(BF16),
        "waup": waup_p.astype(BF16),
        "waupt": waup_p.T.astype(BF16),
        "ba": row(b_a), "bac": b_a.reshape(-1, 1),
        "glag": gla_norm_g, "lng": row(gmlp_ln_g), "lnb": row(gmlp_ln_b),
        "ws": w_s,
        "ws_sample": jnp.tile(ws_small, (1, reps, reps)),
        "bsf_prompt": jnp.repeat(b_s[:, :GMLP_CHUNK].T, GMLP_DH, axis=1),
        "bsf_sample": jnp.tile(jnp.repeat(b_s[:, :seq_len].T, GMLP_DH, axis=1), (reps, 1)),
        "gmg": gmlp_norm_g,
        "wout": w_out.astype(BF16),
        "ln1g": row(ln1_g), "ln1b": row(ln1_b),
        "wrt": w_router.T.astype(BF16), "rbias": router_bias.reshape(-1, 1),
        "wsg": ws_gate.astype(BF16), "wsu": ws_up.astype(BF16), "wsd": ws_down.astype(BF16),
        "ln2g": row(ln2_g), "ln2b": row(ln2_b),
    }


def kernel(x_prompt, x_sample, state_gla, w_in, w_a_up, b_a, gla_norm_g, gmlp_ln_g, gmlp_ln_b, w_s, b_s,
           gmlp_norm_g, w_out, ln1_g, ln1_b, w_router, router_bias, w_gate, w_up, w_down, ws_gate, ws_up,
           ws_down, ln2_g, ln2_b):
    assert x_prompt.shape[1] % PROMPT_TILE == 0 and x_sample.shape[0] % SAMPLE_SEQS == 0
    assert x_sample.shape[1] <= GMLP_CHUNK and w_in.shape[0] == DEPTH
    assert (x_sample.shape[0] * x_sample.shape[1]) % BLOCK == 0 and PROMPT_TILE % BLOCK == 0
    bsz, seq, _ = x_prompt.shape
    dbsz, dseq, _ = x_sample.shape
    hp, hs = x_prompt, x_sample
    gla_p, gla_s, v_s = [], [], []
    for l in range(DEPTH):
        wts = _prep_weights(dseq, w_in[l], w_a_up[l], b_a[l], gla_norm_g[l], gmlp_ln_g[l], gmlp_ln_b[l], w_s[l],
                            b_s[l], gmlp_norm_g[l], w_out[l], ln1_g[l], ln1_b[l], w_router[l], router_bias[l],
                            ws_gate[l], ws_up[l], ws_down[l], ln2_g[l], ln2_b[l])
        h_p, sp = _mixer_prompt(hp, wts)
        h_s, ss, vrows = _mixer_sample(hs, state_gla[l], wts)
        y_p, y_s = _moe(h_p.reshape(bsz * seq, D_MODEL), h_s, w_gate[l], w_up[l], w_down[l], wts)
        hp = y_p.reshape(bsz, seq, D_MODEL)
        hs = y_s.reshape(dbsz, dseq, D_MODEL)
        gla_p.append(sp)
        gla_s.append(ss)
        v_s.append(vrows)
    return (hp, hs, jnp.stack(gla_p), jnp.stack(gla_s), jnp.stack(v_s))
```

```python
import functools
import math

import jax
import jax.numpy as jnp
from jax import lax
from jax.experimental import pallas as pl
from jax.experimental.pallas import tpu as pltpu

F32 = jnp.float32
BF16 = jnp.bfloat16

D_MODEL = 1024
DEPTH = 1
GLA_WIDTH = 512
GLA_HEADS = 4
GLA_DK = 64
GLA_DV = 128
GLA_RANK = 16
GLA_TAU = 16.0
GMLP_WIDTH = 512
GMLP_HEADS = 4
GMLP_DH = 128
GMLP_CHUNK = 128
QK_WIDTH = GLA_HEADS * GLA_DK
N_EXPERTS = 64
TOP_K = 8
N_GROUPS = 8
GROUP_SIZE = N_EXPERTS // N_GROUPS
TOPK_GROUPS = 4
D_EXPERT = 256
D_SHARED = 256
ROUTE_SCALE = 2.5
ALPHA = (2.0 * DEPTH) ** 0.25

LANES = 128

C_Q = 0
C_K = C_Q + QK_WIDTH
C_VA = C_K + QK_WIDTH
C_G = C_VA + GLA_WIDTH
C_U = C_G + GLA_WIDTH
C_VB = C_U + GMLP_WIDTH
C_ALR = C_VB + GMLP_WIDTH
N_PROJ = C_ALR + LANES

CHUNK = 128
PROMPT_TILE = 256
SAMPLE_SEQS = 32
VMEM_LIMIT = 56 * 1024 * 1024

BLOCK = 256
SEG = 16
TILE_CHUNKS = 16
TILE_ROWS = TILE_CHUNKS * SEG
BLOCK_ROWS_USED = BLOCK * TOP_K + N_EXPERTS * (SEG - 1)
BLOCK_ROWS = -(-BLOCK_ROWS_USED // TILE_ROWS) * TILE_ROWS
DUMP_BLOCKS = -(-(N_EXPERTS * TILE_ROWS) // BLOCK_ROWS)
GATHER_ROWS = 512
IN_BUFS = 3


def _dot(a, b):
    return jnp.dot(a, b, preferred_element_type=F32)


def _dot_nt(a, b):
    return lax.dot_general(a, b, (((1,), (1,)), ((), ())), preferred_element_type=F32)


def _dot_tn(a, b):
    return lax.dot_general(a, b, (((0,), (0,)), ((), ())), preferred_element_type=F32)


def _shr(x, d):
    assert d & (d - 1) == 0
    return lax.shift_right_logical(x, d.bit_length() - 1)


def _split_dot(m01, x):
    hi = x.astype(BF16)
    lo = (x - hi.astype(F32)).astype(BF16)
    return _dot(m01, hi) + _dot(m01, lo)


def _split_dot_r(x, m01):
    hi = x.astype(BF16)
    lo = (x - hi.astype(F32)).astype(BF16)
    return _dot(hi, m01) + _dot(lo, m01)


def _sigmoid(x):
    return 1.0 / (1.0 + jnp.exp(-x))


def _silu(x):
    return x * _sigmoid(x)


def _gelu(x):
    c = math.sqrt(2.0 / math.pi)
    return x * (0.5 * (1.0 + jnp.tanh(c * (x + 0.044715 * (x * x * x)))))


def _log_sigmoid(x):
    return -(jnp.maximum(-x, 0.0) + jnp.log1p(jnp.exp(-jnp.abs(x))))


def _layernorm(x, g, b, eps=1e-5):
    mu = jnp.mean(x, axis=-1, keepdims=True)
    xc = x - mu
    var = jnp.mean(xc * xc, axis=-1, keepdims=True)
    return xc * lax.rsqrt(var + eps) * g + b


def _rmsnorm(x, g, eps=1e-6):
    return x * lax.rsqrt(jnp.mean(x * x, axis=-1, keepdims=True) + eps) * g


def _gmlp_heads(u, vg, w_mix, bias, gmg_ref, merged_ref, rows):
    vgb = vg.astype(BF16)
    for h in range(GMLP_HEADS):
        cs = slice(h * GMLP_DH, (h + 1) * GMLP_DH)
        sgu = _dot(w_mix(h), vgb[:, cs]) + bias[:, cs]
        y = _rmsnorm(u[:, cs] * sgu, gmg_ref[h:h + 1, :])
        merged_ref[rows, GLA_WIDTH + h * GMLP_DH:GLA_WIDTH + (h + 1) * GMLP_DH] = y.astype(BF16)


def _out_proj_ln(x, merged_ref, wout_ref, ln1g_ref, ln1b_ref):
    m = _dot(merged_ref[...], wout_ref[...])
    return _layernorm(ALPHA * x + m, ln1g_ref[...], ln1b_ref[...])


def _mixer_prompt_kernel(x_ref, win_ref, waup_ref, ba_ref, glag_ref, lng_ref, lnb_ref, ws_ref, bsf_ref,
                         gmg_ref, wout_ref, ln1g_ref, ln1b_ref,
                         h_ref, state_ref,
                         z_ref, merged_ref, st_ref):
    t = pl.program_id(1)

    @pl.when(t == 0)
    def _():
        st_ref[...] = jnp.zeros_like(st_ref)

    x = x_ref[0]
    z_ref[...] = _dot(x.astype(BF16), win_ref[...])

    row_i = lax.broadcasted_iota(jnp.int32, (CHUNK, CHUNK), 0)
    col_i = lax.broadcasted_iota(jnp.int32, (CHUNK, CHUNK), 1)
    causal = row_i >= col_i
    tri = jnp.where(causal, 1.0, 0.0).astype(BF16)
    lane = lax.broadcasted_iota(jnp.int32, (1, LANES), 1)
    head_lanes = [lane < GLA_DK, lane >= GLA_DK]
    mid = CHUNK // 2 - 1

    for c in range(PROMPT_TILE // CHUNK):
        rows = slice(c * CHUNK, (c + 1) * CHUNK)
        a_pre = _dot(z_ref[rows, C_ALR:C_ALR + LANES].astype(BF16), waup_ref[...]) + ba_ref[...]
        log_a = _log_sigmoid(a_pre) * (1.0 / GLA_TAU)
        b = _split_dot(tri, log_a)
        b_mid = b[mid:mid + 1, :]
        b_last = b[CHUNK - 1:CHUNK, :]
        q = z_ref[rows, C_Q:C_Q + QK_WIDTH] * (GLA_DK ** -0.5)
        k = z_ref[rows, C_K:C_K + QK_WIDTH]
        q_in = (q * jnp.exp(b - b_mid)).astype(BF16)
        k_in = (k * jnp.exp(b_mid - b)).astype(BF16)
        q_st = (q * jnp.exp(b)).astype(BF16)
        k_st = (k * jnp.exp(b_last - b)).astype(BF16)
        d_last = jnp.exp(b_last)
        va = z_ref[rows, C_VA:C_VA + GLA_WIDTH].astype(BF16)
        for h in range(GLA_HEADS):
            ps = slice((h // 2) * LANES, (h // 2 + 1) * LANES)
            vs = slice(h * GLA_DV, (h + 1) * GLA_DV)
            hm = head_lanes[h % 2]
            zero = jnp.zeros((), BF16)
            a = _dot_nt(jnp.where(hm, q_in[:, ps], zero), k_in[:, ps])
            a = jnp.where(causal, a, 0.0).astype(BF16)
            st = st_ref[h]
            o = _dot(a, va[:, vs]) + _dot_nt(jnp.where(hm, q_st[:, ps], zero), st.astype(BF16))
            upd = _dot_tn(va[:, vs], jnp.where(hm, k_st[:, ps], zero))
            st_ref[h] = st * d_last[:, ps] + upd
            gate = z_ref[rows, C_G + h * GLA_DV:C_G + (h + 1) * GLA_DV]
            o = _rmsnorm(o, glag_ref[h:h + 1, :]) * _silu(gate)
            merged_ref[rows, vs] = o.astype(BF16)
        u = _gelu(z_ref[rows, C_U:C_U + GMLP_WIDTH])
        vg = _layernorm(_gelu(z_ref[rows, C_VB:C_VB + GMLP_WIDTH]), lng_ref[...], lnb_ref[...])
        _gmlp_heads(u, vg, lambda h: jnp.where(causal, ws_ref[h], 0.0).astype(BF16), bsf_ref[...],
                    gmg_ref, merged_ref, rows)

    h_ref[0] = _out_proj_ln(x, merged_ref, wout_ref, ln1g_ref, ln1b_ref)

    @pl.when(t == pl.num_programs(1) - 1)
    def _():
        for h in range(GLA_HEADS):
            lo = (h % 2) * GLA_DK
            state_ref[0, h] = st_ref[h].T[lo:lo + GLA_DK, :]


def _mixer_sample_kernel(seq_len, x_ref, s0_ref, win_ref, wkt_ref, walrt_ref, waup_ref, waupt_ref, ba_ref, bac_ref,
                         glag_ref, lng_ref, lnb_ref, wst_ref, bsf_ref, gmg_ref, wout_ref, ln1g_ref, ln1b_ref,
                         h_ref, snew_ref, vg_ref,
                         merged_ref):
    n = SAMPLE_SEQS * seq_len
    x = x_ref[...]
    xb = x.astype(BF16)
    z = _dot(xb, win_ref[...])

    ti = lax.broadcasted_iota(jnp.int32, (n, n), 0)
    tj = lax.broadcasted_iota(jnp.int32, (n, n), 1)
    same = _shr(ti, seq_len) == _shr(tj, seq_len)
    causal = jnp.logical_and(same, ti >= tj)
    tri = jnp.where(causal, 1.0, 0.0).astype(BF16)
    tri_t = jnp.where(jnp.logical_and(same, ti <= tj), 1.0, 0.0).astype(BF16)
    same01 = jnp.where(same, 1.0, 0.0).astype(BF16)

    a_pre = _dot(z[:, C_ALR:C_ALR + LANES].astype(BF16), waup_ref[...]) + ba_ref[...]
    log_a = _log_sigmoid(a_pre) * (1.0 / GLA_TAU)
    b = _split_dot(tri, log_a)
    q = z[:, C_Q:C_Q + QK_WIDTH] * (GLA_DK ** -0.5)
    k = z[:, C_K:C_K + QK_WIDTH]
    q_in = (q * jnp.exp(b)).astype(BF16)
    k_in = (k * jnp.exp(-b)).astype(BF16)
    va = z[:, C_VA:C_VA + GLA_WIDTH].astype(BF16)

    k_t = _dot_nt(wkt_ref[...], xb)
    alr_t = _dot_nt(walrt_ref[...], xb)
    a_pre_t = _dot(waupt_ref[...], alr_t.astype(BF16)) + bac_ref[...]
    log_a_t = _log_sigmoid(a_pre_t) * (1.0 / GLA_TAU)
    b_t = _split_dot_r(log_a_t, tri_t)
    tot_t = _split_dot_r(log_a_t, same01)
    k_st_t = k_t * jnp.exp(tot_t - b_t)
    d_t = jnp.exp(tot_t)

    lane = lax.broadcasted_iota(jnp.int32, (1, LANES), 1)
    head_lanes = [lane < GLA_DK, lane >= GLA_DK]
    nrow = SAMPLE_SEQS * GLA_DK
    r_seq = _shr(lax.broadcasted_iota(jnp.int32, (nrow, n), 0), GLA_DK)
    c_seq = _shr(lax.broadcasted_iota(jnp.int32, (nrow, n), 1), seq_len)
    c_first = (lax.broadcasted_iota(jnp.int32, (nrow, n), 1) & (seq_len - 1)) == 0
    blk = r_seq == c_seq
    blk_first = jnp.logical_and(blk, c_first)
    q_seq = _shr(lax.broadcasted_iota(jnp.int32, (n, nrow), 0), seq_len)
    q_col = _shr(lax.broadcasted_iota(jnp.int32, (n, nrow), 1), GLA_DK)
    blk_q = q_seq == q_col

    for h in range(GLA_HEADS):
        ps = slice((h // 2) * LANES, (h // 2 + 1) * LANES)
        ds_ = slice(h * GLA_DK, (h + 1) * GLA_DK)
        vs = slice(h * GLA_DV, (h + 1) * GLA_DV)
        hm = head_lanes[h % 2]
        zero = jnp.zeros((), BF16)
        a = _dot_nt(jnp.where(hm, q_in[:, ps], zero), k_in[:, ps])
        a = jnp.where(causal, a, 0.0).astype(BF16)
        s0 = s0_ref[:, h].reshape(nrow, GLA_DV)
        q_h = q_in[:, ds_]
        q_bd = jnp.where(blk_q, jnp.concatenate([q_h] * SAMPLE_SEQS, axis=1), zero)
        o = _dot(a, va[:, vs]) + _dot(q_bd, s0.astype(BF16))
        k_bd = jnp.where(blk, jnp.concatenate([k_st_t[ds_, :]] * SAMPLE_SEQS, axis=0), 0.0).astype(BF16)
        upd = _dot(k_bd, va[:, vs])
        d_bd = jnp.where(blk_first, jnp.concatenate([d_t[ds_, :]] * SAMPLE_SEQS, axis=0), 0.0)
        d_col = jnp.sum(d_bd, axis=1, keepdims=True)
        snew_ref[:, h] = (s0 * d_col + upd).reshape(SAMPLE_SEQS, GLA_DK, GLA_DV)
        gate = z[:, C_G + h * GLA_DV:C_G + (h + 1) * GLA_DV]
        o = _rmsnorm(o, glag_ref[h:h + 1, :]) * _silu(gate)
        merged_ref[:, vs] = o.astype(BF16)

    u = _gelu(z[:, C_U:C_U + GMLP_WIDTH])
    vg = _layernorm(_gelu(z[:, C_VB:C_VB + GMLP_WIDTH]), lng_ref[...], lnb_ref[...])
    vg_ref[...] = vg
    _gmlp_heads(u, vg, lambda h: jnp.where(causal, wst_ref[h], 0.0).astype(BF16), bsf_ref[...],
                gmg_ref, merged_ref, slice(None))
    h_ref[...] = _out_proj_ln(x, merged_ref, wout_ref, ln1g_ref, ln1b_ref)


def _route(hb, wrt_ref, rbias_ref):
    tm = hb.shape[0]
    s = _sigmoid(_dot_nt(wrt_ref[...], hb))
    sb = s + rbias_ref[...]
    neg = jnp.float32(-jnp.inf)
    sub = lax.broadcasted_iota(jnp.int32, (GROUP_SIZE, tm), 0)
    gscore = []
    for g in range(N_GROUPS):
        blk = sb[g * GROUP_SIZE:(g + 1) * GROUP_SIZE, :]
        m1 = jnp.max(blk, axis=0, keepdims=True)
        i1 = jnp.min(jnp.where(blk == m1, sub, GROUP_SIZE), axis=0, keepdims=True)
        m2 = jnp.max(jnp.where(sub == i1, neg, blk), axis=0, keepdims=True)
        gscore.append(m1 + m2)
    gsel = []
    for g in range(N_GROUPS):
        rank = jnp.zeros((1, tm), jnp.int32)
        for o in range(N_GROUPS):
            if o == g:
                continue
            ahead = (gscore[o] >= gscore[g]) if o < g else (gscore[o] > gscore[g])
            rank = rank + ahead.astype(jnp.int32)
        gsel.append(rank < TOPK_GROUPS)
    v = jnp.concatenate(
        [jnp.where(gsel[g], sb[g * GROUP_SIZE:(g + 1) * GROUP_SIZE, :], neg) for g in range(N_GROUPS)], axis=0)
    eidx = lax.broadcasted_iota(jnp.int32, (N_EXPERTS, tm), 0)
    sel = jnp.zeros((N_EXPERTS, tm), F32)
    hits = []
    for _ in range(TOP_K):
        m = jnp.max(v, axis=0, keepdims=True)
        first = jnp.min(jnp.where(v == m, eidx, N_EXPERTS), axis=0, keepdims=True)
        hit = eidx == first
        hits.append(hit)
        sel = jnp.where(hit, 1.0, sel)
        v = jnp.where(hit, neg, v)
    wsel = s * sel
    return wsel / jnp.sum(wsel, axis=0, keepdims=True) * ROUTE_SCALE, sel, hits


def _slot_onehot(pos_rows, val_rows, n_rows):
    tm = pos_rows[0].shape[-1]
    r = lax.broadcasted_iota(jnp.int32, (n_rows, tm), 0).astype(F32)
    out = jnp.zeros((n_rows, tm), F32)
    for p, v in zip(pos_rows, val_rows):
        out = out + jnp.where(r == p, v, 0.0)
    return out


def _dispatch_kernel(n_prompt_blocks, n_blocks, hp_ref, hs_ref, wrt_ref, rbias_ref, x_ref, pos_ref, wt_ref,
                     seg_ref):
    i = pl.program_id(0)

    @pl.when(i < n_blocks)
    def _():
        _dispatch_block(i < n_prompt_blocks, hp_ref, hs_ref, wrt_ref, rbias_ref, x_ref, pos_ref, wt_ref, seg_ref)

    @pl.when(i >= n_blocks)
    def _():
        x_ref[...] = jnp.zeros_like(x_ref)
        pos_ref[...] = jnp.zeros_like(pos_ref)
        wt_ref[...] = jnp.zeros_like(wt_ref)
        seg_ref[...] = jnp.zeros_like(seg_ref)


def _dispatch_block(is_prompt, hp_ref, hs_ref, wrt_ref, rbias_ref, x_ref, pos_ref, wt_ref, seg_ref):
    tm = BLOCK
    hb = jnp.where(is_prompt, hp_ref[...], hs_ref[...]).astype(BF16)
    comb, sel, hits = _route(hb, wrt_ref, rbias_ref)
    ti = lax.broadcasted_iota(jnp.int32, (tm, tm), 0)
    tj = lax.broadcasted_iota(jnp.int32, (tm, tm), 1)
    before = jnp.where(ti < tj, 1.0, 0.0).astype(BF16)
    rank = _dot(sel.astype(BF16), before)
    cnt = jnp.sum(sel, axis=1, keepdims=True)
    seg = jnp.floor((cnt + (SEG - 1.0)) * (1.0 / SEG)) * SEG
    ei = lax.broadcasted_iota(jnp.int32, (N_EXPERTS, N_EXPERTS), 0)
    ej = lax.broadcasted_iota(jnp.int32, (N_EXPERTS, N_EXPERTS), 1)
    below = jnp.where(ej < ei, 1.0, 0.0).astype(BF16)
    seg_b = jnp.broadcast_to(seg, (N_EXPERTS, LANES))
    start = _dot(below, seg_b.astype(BF16))[:, 0:1]
    pos_all = start + rank
    pos_rows = [jnp.sum(jnp.where(h, pos_all, 0.0), axis=0, keepdims=True) for h in hits]
    wt_rows = [jnp.sum(jnp.where(h, comb, 0.0), axis=0, keepdims=True) for h in hits]
    ones = jnp.ones((1, tm), F32)
    gather = _slot_onehot(pos_rows, [ones] * TOP_K, BLOCK_ROWS).astype(BF16)
    for c in range(BLOCK_ROWS // GATHER_ROWS):
        rs = slice(c * GATHER_ROWS, (c + 1) * GATHER_ROWS)
        x_ref[rs, :] = _dot(gather[rs, :], hb).astype(BF16)
    pos_ref[0] = jnp.concatenate(pos_rows, axis=0)
    wt_ref[0] = jnp.concatenate(wt_rows, axis=0)
    seg_ref[0] = seg_b


def _ffn_kernel(dump_base, nt_max, tile_expert_ref, tile_in_expert_ref, n_chunks_ref, n_tiles_ref,
                tab0_ref, tab1_ref, tab2_ref, x_hbm, wg_ref, wu_ref, wd_ref,
                o_hbm, ibuf, obuf, wgb, wub, wdb, in_sem, out_sem):
    i = pl.program_id(0)
    n_tiles = n_tiles_ref[0]
    tabs = (tab0_ref, tab1_ref, tab2_ref)

    def in_copy(ahead, j, slot):
        first = tile_in_expert_ref[jnp.minimum(i + ahead, nt_max - 1)] * TILE_CHUNKS
        src = pl.multiple_of(tabs[ahead][0, 0, first + j], SEG)
        return pltpu.make_async_copy(x_hbm.at[pl.ds(src, SEG)], ibuf.at[slot, pl.ds(j * SEG, SEG)], in_sem.at[slot])

    def out_copy(j, slot):
        e = tile_expert_ref[i]
        first = tile_in_expert_ref[i] * TILE_CHUNKS
        is_real = first + j < n_chunks_ref[e]
        dump = dump_base + (e * TILE_CHUNKS + j) * SEG
        dst = pl.multiple_of(jnp.where(is_real, tab0_ref[0, 0, first + j], dump), SEG)
        return pltpu.make_async_copy(obuf.at[slot, pl.ds(j * SEG, SEG)], o_hbm.at[pl.ds(dst, SEG)], out_sem.at[slot])

    def out_wait(slot):
        for j in range(TILE_CHUNKS):
            pltpu.make_async_copy(obuf.at[slot, pl.ds(j * SEG, SEG)], o_hbm.at[pl.ds(0, SEG)], out_sem.at[slot]).wait()

    @pl.when(i < n_tiles)
    def _():
        islot = lax.rem(i, IN_BUFS)
        oslot = lax.rem(i, 2)

        @pl.when(i == 0)
        def _():
            for j in range(TILE_CHUNKS):
                in_copy(0, j, 0).start()

            @pl.when(n_tiles > 1)
            def _():
                for j in range(TILE_CHUNKS):
                    in_copy(1, j, 1).start()

        @pl.when(i + 2 < n_tiles)
        def _():
            nslot = lax.rem(i + 2, IN_BUFS)
            for j in range(TILE_CHUNKS):
                in_copy(2, j, nslot).start()

        for j in range(TILE_CHUNKS):
            in_copy(0, j, islot).wait()

        @pl.when(tile_in_expert_ref[i] == 0)
        def _():
            wgb[...] = wg_ref[0].astype(BF16)
            wub[...] = wu_ref[0].astype(BF16)
            wdb[...] = wd_ref[0].astype(BF16)

        @pl.when(i >= 2)
        def _():
            out_wait(oslot)

        x = ibuf[islot]
        hid = _silu(_dot(x, wgb[...])) * _dot(x, wub[...])
        obuf[oslot] = _dot(hid.astype(BF16), wdb[...]).astype(BF16)
        for j in range(TILE_CHUNKS):
            out_copy(j, oslot).start()

        @pl.when(i == n_tiles - 1)
        def _():
            out_wait(oslot)

            @pl.when(i >= 1)
            def _():
                out_wait(1 - oslot)


def _combine_kernel(h_ref, o_ref, pos_ref, wt_ref, wsg_ref, wsu_ref, wsd_ref, ln2g_ref, ln2b_ref, y_ref):
    h = h_ref[...]
    hb = h.astype(BF16)
    pos_rows = [pos_ref[0, k:k + 1, :] for k in range(TOP_K)]
    wt_rows = [wt_ref[0, k:k + 1, :] for k in range(TOP_K)]
    scatter = _slot_onehot(pos_rows, wt_rows, BLOCK_ROWS).astype(BF16)
    routed = _dot_tn(scatter, o_ref[...])
    shared = _dot((_silu(_dot(hb, wsg_ref[...])) * _dot(hb, wsu_ref[...])).astype(BF16), wsd_ref[...])
    y_ref[...] = _layernorm(ALPHA * h + (routed + shared), ln2g_ref[...], ln2b_ref[...])


def _full(shape):
    return pl.BlockSpec(shape, lambda *_: (0,) * len(shape))


def _mixer_prompt(x, wts):
    bsz, seq, _ = x.shape
    n_t = seq // PROMPT_TILE
    weights = [wts[k] for k in ("win", "waup", "ba", "glag", "lng", "lnb", "ws", "bsf_prompt", "gmg", "wout",
                                "ln1g", "ln1b")]
    return pl.pallas_call(
        _mixer_prompt_kernel,
        grid=(bsz, n_t),
        in_specs=[pl.BlockSpec((1, PROMPT_TILE, D_MODEL), lambda b, t: (b, t, 0))]
        + [_full(w.shape) for w in weights],
        out_specs=[pl.BlockSpec((1, PROMPT_TILE, D_MODEL), lambda b, t: (b, t, 0)),
                   pl.BlockSpec((1, GLA_HEADS, GLA_DK, GLA_DV), lambda b, t: (b, 0, 0, 0))],
        out_shape=[jax.ShapeDtypeStruct((bsz, seq, D_MODEL), F32),
                   jax.ShapeDtypeStruct((bsz, GLA_HEADS, GLA_DK, GLA_DV), F32)],
        scratch_shapes=[pltpu.VMEM((PROMPT_TILE, N_PROJ), F32),
                        pltpu.VMEM((PROMPT_TILE, D_MODEL), BF16),
                        pltpu.VMEM((GLA_HEADS, GLA_DV, LANES), F32)],
        compiler_params=pltpu.CompilerParams(dimension_semantics=("arbitrary", "arbitrary"),
                                             vmem_limit_bytes=VMEM_LIMIT),
        name="mixer_prompt",
    )(x, *weights)


def _mixer_sample(x, s0, wts):
    bsz, seq_len, _ = x.shape
    n = SAMPLE_SEQS * seq_len
    x2 = x.reshape(bsz * seq_len, D_MODEL)
    weights = [wts[k] for k in ("win", "wkt", "walrt", "waup", "waupt", "ba", "bac", "glag", "lng", "lnb",
                                "ws_sample", "bsf_sample", "gmg", "wout", "ln1g", "ln1b")]
    state_spec = pl.BlockSpec((SAMPLE_SEQS, GLA_HEADS, GLA_DK, GLA_DV), lambda i: (i, 0, 0, 0))
    h, s_new, vg = pl.pallas_call(
        functools.partial(_mixer_sample_kernel, seq_len),
        grid=(bsz // SAMPLE_SEQS,),
        in_specs=[pl.BlockSpec((n, D_MODEL), lambda i: (i, 0)), state_spec] + [_full(w.shape) for w in weights],
        out_specs=[pl.BlockSpec((n, D_MODEL), lambda i: (i, 0)), state_spec,
                   pl.BlockSpec((n, GMLP_WIDTH), lambda i: (i, 0))],
        out_shape=[jax.ShapeDtypeStruct((bsz * seq_len, D_MODEL), F32),
                   jax.ShapeDtypeStruct(s0.shape, F32),
                   jax.ShapeDtypeStruct((bsz * seq_len, GMLP_WIDTH), F32)],
        scratch_shapes=[pltpu.VMEM((n, D_MODEL), BF16)],
        compiler_params=pltpu.CompilerParams(dimension_semantics=("arbitrary",), vmem_limit_bytes=VMEM_LIMIT),
        name="mixer_sample",
    )(x2, s0, *weights)
    return h, s_new, vg.reshape(bsz, seq_len, GMLP_WIDTH)


def _dispatch(h_p, h_s, wts):
    npb, nsb = h_p.shape[0] // BLOCK, h_s.shape[0] // BLOCK
    nb_real = npb + nsb
    nb = nb_real + DUMP_BLOCKS
    slot_spec = pl.BlockSpec((1, TOP_K, BLOCK), lambda i: (i, 0, 0))
    return pl.pallas_call(
        functools.partial(_dispatch_kernel, npb, nb_real),
        grid=(nb,),
        in_specs=[pl.BlockSpec((BLOCK, D_MODEL), lambda i: (jnp.minimum(i, npb - 1), 0)),
                  pl.BlockSpec((BLOCK, D_MODEL), lambda i: (jnp.clip(i - npb, 0, nsb - 1), 0)),
                  _full(wts["wrt"].shape), _full(wts["rbias"].shape)],
        out_specs=[pl.BlockSpec((BLOCK_ROWS, D_MODEL), lambda i: (i, 0)), slot_spec, slot_spec,
                   pl.BlockSpec((1, N_EXPERTS, LANES), lambda i: (i, 0, 0))],
        out_shape=[jax.ShapeDtypeStruct((nb * BLOCK_ROWS, D_MODEL), BF16),
                   jax.ShapeDtypeStruct((nb, TOP_K, BLOCK), F32),
                   jax.ShapeDtypeStruct((nb, TOP_K, BLOCK), F32),
                   jax.ShapeDtypeStruct((nb, N_EXPERTS, LANES), F32)],
        compiler_params=pltpu.CompilerParams(dimension_semantics=("arbitrary",), vmem_limit_bytes=VMEM_LIMIT),
        name="moe_dispatch",
    )(h_p, h_s, wts["wrt"], wts["rbias"])


def _max_tiles(nb):
    return (nb * (BLOCK_ROWS_USED // SEG) + N_EXPERTS * (TILE_CHUNKS - 1)) // TILE_CHUNKS + 1


def _max_expert_chunks(nb):
    most = nb * (BLOCK // SEG) + TILE_CHUNKS - 1
    return -(-most // LANES) * LANES


def _plan(seg_lanes, nb):
    nt_max = _max_tiles(nb)
    max_chunks = _max_expert_chunks(nb)
    seg = seg_lanes[:nb, :, 0].astype(jnp.int32)
    start = jnp.cumsum(seg, axis=1) - seg
    nch = seg // SEG
    ends = jnp.cumsum(nch, axis=0)
    n_chunks = ends[-1]
    f = jnp.arange(nb, dtype=jnp.int32)[:, None] * BLOCK_ROWS + start - SEG * (ends - nch)
    q = jnp.arange(max_chunks, dtype=jnp.int32)
    passed = (ends[None, :-1, :] <= q[:, None, None]).astype(jnp.int32)
    rows = SEG * q[:, None] + f[0][None, :] + jnp.sum(passed * (f[1:] - f[:-1])[None], axis=1)
    table = jnp.where(q[:, None] < n_chunks[None, :], rows, BLOCK_ROWS_USED).T
    n_tiles_e = -(-n_chunks // TILE_CHUNKS)
    tile_ends = jnp.cumsum(n_tiles_e)
    t = jnp.arange(nt_max, dtype=jnp.int32)
    done = (tile_ends[None, :] <= t[:, None]).astype(jnp.int32)
    tile_expert = jnp.minimum(jnp.sum(done, axis=1), N_EXPERTS - 1)
    tile_in_expert = t - jnp.sum(done * n_tiles_e[None, :], axis=1)
    return (table.astype(jnp.int32).reshape(N_EXPERTS, 1, max_chunks), tile_expert.astype(jnp.int32),
            tile_in_expert.astype(jnp.int32), n_chunks.astype(jnp.int32), tile_ends[-1:].astype(jnp.int32))


def _ffn(x_rows, nb, table, tile_expert, tile_in_expert, n_chunks, n_tiles, w_gate, w_up, w_down):
    nt_max = tile_expert.shape[0]
    max_chunks = table.shape[-1]
    tab_spec = lambda ahead: pl.BlockSpec(
        (1, 1, max_chunks), lambda i, te, *_: (te[jnp.minimum(i + ahead, nt_max - 1)], 0, 0),
        memory_space=pltpu.SMEM)
    w_spec = lambda shape: pl.BlockSpec((1,) + shape, lambda i, te, *_: (te[i], 0, 0))
    assert x_rows.shape[0] - nb * BLOCK_ROWS >= N_EXPERTS * TILE_ROWS
    return pl.pallas_call(
        functools.partial(_ffn_kernel, nb * BLOCK_ROWS, nt_max),
        grid_spec=pltpu.PrefetchScalarGridSpec(
            num_scalar_prefetch=4,
            grid=(nt_max,),
            in_specs=[tab_spec(0), tab_spec(1), tab_spec(2), pl.BlockSpec(memory_space=pl.ANY),
                      w_spec((D_MODEL, D_EXPERT)), w_spec((D_MODEL, D_EXPERT)), w_spec((D_EXPERT, D_MODEL))],
            out_specs=pl.BlockSpec(memory_space=pl.ANY),
            scratch_shapes=[pltpu.VMEM((IN_BUFS, TILE_ROWS, D_MODEL), BF16),
                            pltpu.VMEM((2, TILE_ROWS, D_MODEL), BF16),
                            pltpu.VMEM((D_MODEL, D_EXPERT), BF16),
                            pltpu.VMEM((D_MODEL, D_EXPERT), BF16),
                            pltpu.VMEM((D_EXPERT, D_MODEL), BF16),
                            pltpu.SemaphoreType.DMA((IN_BUFS,)),
                            pltpu.SemaphoreType.DMA((2,))]),
        out_shape=jax.ShapeDtypeStruct(x_rows.shape, BF16),
        input_output_aliases={7: 0},
        compiler_params=pltpu.CompilerParams(dimension_semantics=("arbitrary",), vmem_limit_bytes=VMEM_LIMIT),
        name="moe_ffn",
    )(tile_expert, tile_in_expert, n_chunks, n_tiles, table, table, table, x_rows, w_gate, w_up, w_down)


def _combine(h, o_rows, pos, wt, block_off, wts):
    nblk = h.shape[0] // BLOCK
    weights = [wts[k] for k in ("wsg", "wsu", "wsd", "ln2g", "ln2b")]
    slot_spec = pl.BlockSpec((1, TOP_K, BLOCK), lambda i: (i + block_off, 0, 0))
    return pl.pallas_call(
        _combine_kernel,
        grid=(nblk,),
        in_specs=[pl.BlockSpec((BLOCK, D_MODEL), lambda i: (i, 0)),
                  pl.BlockSpec((BLOCK_ROWS, D_MODEL), lambda i: (i + block_off, 0)),
                  slot_spec, slot_spec] + [_full(w.shape) for w in weights],
        out_specs=pl.BlockSpec((BLOCK, D_MODEL), lambda i: (i, 0)),
        out_shape=jax.ShapeDtypeStruct(h.shape, F32),
        compiler_params=pltpu.CompilerParams(dimension_semantics=("arbitrary",), vmem_limit_bytes=VMEM_LIMIT),
        name="moe_combine",
    )(h, o_rows, pos, wt, *weights)


def _moe(h_p, h_s, w_gate, w_up, w_down, wts):
    nb = (h_p.shape[0] + h_s.shape[0]) // BLOCK
    x_rows, pos, wt, seg_lanes = _dispatch(h_p, h_s, wts)
    o_rows = _ffn(x_rows, nb, *_plan(seg_lanes, nb), w_gate, w_up, w_down)
    y_p = _combine(h_p, o_rows, pos, wt, 0, wts)
    y_s = _combine(h_s, o_rows, pos, wt, h_p.shape[0] // BLOCK, wts)
    return y_p, y_s


def _prep_weights(seq_len, w_in, w_a_up, b_a, gla_norm_g, gmlp_ln_g, gmlp_ln_b, w_s, b_s, gmlp_norm_g, w_out,
                  ln1_g, ln1_b, w_router, router_bias, ws_gate, ws_up, ws_down, ln2_g, ln2_b):
    o1 = QK_WIDTH
    o2 = o1 + QK_WIDTH
    o3 = o2 + GLA_WIDTH
    o4 = o3 + GLA_WIDTH
    o5 = o4 + GLA_RANK
    o6 = o5 + GMLP_WIDTH
    wq, wk, wva, wg_, walr, wu_, wvb = jnp.split(w_in, [o1, o2, o3, o4, o5, o6], axis=-1)
    walr_p = jnp.pad(walr, ((0, 0), (0, LANES - GLA_RANK)))
    waup_p = jnp.pad(w_a_up, ((0, LANES - GLA_RANK), (0, 0)))
    row = lambda a: a.reshape(1, -1)
    reps = (SAMPLE_SEQS * seq_len) // seq_len
    ws_small = w_s[:, :seq_len, :seq_len]
    return {
        "win": jnp.concatenate([wq, wk, wva, wg_, wu_, wvb, walr_p], axis=-1).astype(BF16),
        "wkt": wk.T.astype(BF16),
        "walrt": walr_p.T.astype(BF16),
        "waup": waup_p.astype(BF16),
        "waupt": waup_p.T.astype(BF16),
        "ba": row(b_a), "bac": b_a.reshape(-1, 1),
        "glag": gla_norm_g, "lng": row(gmlp_ln_g), "lnb": row(gmlp_ln_b),
        "ws": w_s,
        "ws_sample": jnp.tile(ws_small, (1, reps, reps)),
        "bsf_prompt": jnp.repeat(b_s[:, :GMLP_CHUNK].T, GMLP_DH, axis=1),
        "bsf_sample": jnp.tile(jnp.repeat(b_s[:, :seq_len].T, GMLP_DH, axis=1), (reps, 1)),
        "gmg": gmlp_norm_g,
        "wout": w_out.astype(BF16),
        "ln1g": row(ln1_g), "ln1b": row(ln1_b),
        "wrt": w_router.T.astype(BF16), "rbias": router_bias.reshape(-1, 1),
        "wsg": ws_gate.astype(BF16), "wsu": ws_up.astype(BF16), "wsd": ws_down.astype(BF16),
        "ln2g": row(ln2_g), "ln2b": row(ln2_b),
    }


def kernel(x_prompt, x_sample, state_gla, w_in, w_a_up, b_a, gla_norm_g, gmlp_ln_g, gmlp_ln_b, w_s, b_s,
           gmlp_norm_g, w_out, ln1_g, ln1_b, w_router, router_bias, w_gate, w_up, w_down, ws_gate, ws_up,
           ws_down, ln2_g, ln2_b):
    assert x_prompt.shape[1] % PROMPT_TILE == 0 and x_sample.shape[0] % SAMPLE_SEQS == 0
    assert x_sample.shape[1] <= GMLP_CHUNK and w_in.shape[0] == DEPTH
    assert (x_sample.shape[0] * x_sample.shape[1]) % BLOCK == 0 and PROMPT_TILE % BLOCK == 0
    bsz, seq, _ = x_prompt.shape
    dbsz, dseq, _ = x_sample.shape
    hp, hs = x_prompt, x_sample
    gla_p, gla_s, v_s = [], [], []
    for l in range(DEPTH):
        wts = _prep_weights(dseq, w_in[l], w_a_up[l], b_a[l], gla_norm_g[l], gmlp_ln_g[l], gmlp_ln_b[l], w_s[l],
                            b_s[l], gmlp_norm_g[l], w_out[l], ln1_g[l], ln1_b[l], w_router[l], router_bias[l],
                            ws_gate[l], ws_up[l], ws_down[l], ln2_g[l], ln2_b[l])
        h_p, sp = _mixer_prompt(hp, wts)
        h_s, ss, vrows = _mixer_sample(hs, state_gla[l], wts)
        y_p, y_s = _moe(h_p.reshape(bsz * seq, D_MODEL), h_s, w_gate[l], w_up[l], w_down[l], wts)
        hp = y_p.reshape(bsz, seq, D_MODEL)
        hs = y_s.reshape(dbsz, dseq, D_MODEL)
        gla_p.append(sp)
        gla_s.append(ss)
        v_s.append(vrows)
    return (hp, hs, jnp.stack(gla_p), jnp.stack(gla_s), jnp.stack(v_s))
```

```python
import functools
import math

import jax
import jax.numpy as jnp
from jax import lax
from jax.experimental import pallas as pl
from jax.experimental.pallas import tpu as pltpu

F32 = jnp.float32
BF16 = jnp.bfloat16

D_MODEL = 1024
DEPTH = 1
GLA_WIDTH = 512
GLA_HEADS = 4
GLA_DK = 64
GLA_DV = 128
GLA_RANK = 16
GLA_TAU = 16.0
GMLP_WIDTH = 512
GMLP_HEADS = 4
GMLP_DH = 128
GMLP_CHUNK = 128
QK_WIDTH = GLA_HEADS * GLA_DK
N_EXPERTS = 64
TOP_K = 8
N_GROUPS = 8
GROUP_SIZE = N_EXPERTS // N_GROUPS
TOPK_GROUPS = 4
D_EXPERT = 256
D_SHARED = 256
ROUTE_SCALE = 2.5
ALPHA = (2.0 * DEPTH) ** 0.25

LANES = 128

C_Q = 0
C_K = C_Q + QK_WIDTH
C_VA = C_K + QK_WIDTH
C_G = C_VA + GLA_WIDTH
C_U = C_G + GLA_WIDTH
C_VB = C_U + GMLP_WIDTH
C_ALR = C_VB + GMLP_WIDTH
N_PROJ = C_ALR + LANES

CHUNK = 128
PROMPT_TILE = 256
SAMPLE_SEQS = 32
VMEM_LIMIT = 56 * 1024 * 1024

BLOCK = 256
SEG = 16
TILE_CHUNKS = 32
SUB_CHUNKS = 16
TILE_ROWS = TILE_CHUNKS * SEG
GATHER_ROWS = 512
BLOCK_ROWS_USED = BLOCK * TOP_K + N_EXPERTS * (SEG - 1)
BLOCK_ROWS = -(-(BLOCK_ROWS_USED + SEG) // GATHER_ROWS) * GATHER_ROWS
DUMP_BLOCKS = -(-(2 * TILE_ROWS) // BLOCK_ROWS)
ZERO_CHUNK = BLOCK_ROWS_USED // SEG
IN_BUFS = 3


def _dot(a, b):
    return jnp.dot(a, b, preferred_element_type=F32)


def _dot_nt(a, b):
    return lax.dot_general(a, b, (((1,), (1,)), ((), ())), preferred_element_type=F32)


def _dot_tn(a, b):
    return lax.dot_general(a, b, (((0,), (0,)), ((), ())), preferred_element_type=F32)


def _shr(x, d):
    assert d & (d - 1) == 0
    return lax.shift_right_logical(x, d.bit_length() - 1)


def _split_dot(m01, x):
    hi = x.astype(BF16)
    lo = (x - hi.astype(F32)).astype(BF16)
    return _dot(m01, hi) + _dot(m01, lo)


def _split_dot_r(x, m01):
    hi = x.astype(BF16)
    lo = (x - hi.astype(F32)).astype(BF16)
    return _dot(hi, m01) + _dot(lo, m01)


def _sigmoid(x):
    return 1.0 / (1.0 + jnp.exp(-x))


def _silu(x):
    return x * _sigmoid(x)


def _gelu(x):
    c = math.sqrt(2.0 / math.pi)
    return x * (0.5 * (1.0 + jnp.tanh(c * (x + 0.044715 * (x * x * x)))))


def _log_sigmoid(x):
    return -(jnp.maximum(-x, 0.0) + jnp.log1p(jnp.exp(-jnp.abs(x))))


def _layernorm(x, g, b, eps=1e-5):
    mu = jnp.mean(x, axis=-1, keepdims=True)
    xc = x - mu
    var = jnp.mean(xc * xc, axis=-1, keepdims=True)
    return xc * lax.rsqrt(var + eps) * g + b


def _rmsnorm(x, g, eps=1e-6):
    return x * lax.rsqrt(jnp.mean(x * x, axis=-1, keepdims=True) + eps) * g


def _gmlp_heads(u, vg, w_mix, bias, gmg_ref, merged_ref, rows):
    vgb = vg.astype(BF16)
    for h in range(GMLP_HEADS):
        cs = slice(h * GMLP_DH, (h + 1) * GMLP_DH)
        sgu = _dot(w_mix(h), vgb[:, cs]) + bias[:, cs]
        y = _rmsnorm(u[:, cs] * sgu, gmg_ref[h:h + 1, :])
        merged_ref[rows, GLA_WIDTH + h * GMLP_DH:GLA_WIDTH + (h + 1) * GMLP_DH] = y.astype(BF16)


def _out_proj_ln(x, merged_ref, wout_ref, ln1g_ref, ln1b_ref):
    m = _dot(merged_ref[...], wout_ref[...])
    return _layernorm(ALPHA * x + m, ln1g_ref[...], ln1b_ref[...])


def _mixer_prompt_kernel(x_ref, win_ref, waup_ref, ba_ref, glag_ref, lng_ref, lnb_ref, ws_ref, bsf_ref,
                         gmg_ref, wout_ref, ln1g_ref, ln1b_ref,
                         h_ref, state_ref,
                         z_ref, merged_ref, st_ref):
    t = pl.program_id(1)

    @pl.when(t == 0)
    def _():
        st_ref[...] = jnp.zeros_like(st_ref)

    x = x_ref[0]
    z_ref[...] = _dot(x.astype(BF16), win_ref[...])

    row_i = lax.broadcasted_iota(jnp.int32, (CHUNK, CHUNK), 0)
    col_i = lax.broadcasted_iota(jnp.int32, (CHUNK, CHUNK), 1)
    causal = row_i >= col_i
    tri = jnp.where(causal, 1.0, 0.0).astype(BF16)
    lane = lax.broadcasted_iota(jnp.int32, (1, LANES), 1)
    head_lanes = [lane < GLA_DK, lane >= GLA_DK]
    mid = CHUNK // 2 - 1

    for c in range(PROMPT_TILE // CHUNK):
        rows = slice(c * CHUNK, (c + 1) * CHUNK)
        a_pre = _dot(z_ref[rows, C_ALR:C_ALR + LANES].astype(BF16), waup_ref[...]) + ba_ref[...]
        log_a = _log_sigmoid(a_pre) * (1.0 / GLA_TAU)
        b = _split_dot(tri, log_a)
        b_mid = b[mid:mid + 1, :]
        b_last = b[CHUNK - 1:CHUNK, :]
        q = z_ref[rows, C_Q:C_Q + QK_WIDTH] * (GLA_DK ** -0.5)
        k = z_ref[rows, C_K:C_K + QK_WIDTH]
        q_in = (q * jnp.exp(b - b_mid)).astype(BF16)
        k_in = (k * jnp.exp(b_mid - b)).astype(BF16)
        q_st = (q * jnp.exp(b)).astype(BF16)
        k_st = (k * jnp.exp(b_last - b)).astype(BF16)
        d_last = jnp.exp(b_last)
        va = z_ref[rows, C_VA:C_VA + GLA_WIDTH].astype(BF16)
        for h in range(GLA_HEADS):
            ps = slice((h // 2) * LANES, (h // 2 + 1) * LANES)
            vs = slice(h * GLA_DV, (h + 1) * GLA_DV)
            hm = head_lanes[h % 2]
            zero = jnp.zeros((), BF16)
            a = _dot_nt(jnp.where(hm, q_in[:, ps], zero), k_in[:, ps])
            a = jnp.where(causal, a, 0.0).astype(BF16)
            st = st_ref[h]
            o = _dot(a, va[:, vs]) + _dot_nt(jnp.where(hm, q_st[:, ps], zero), st.astype(BF16))
            upd = _dot_tn(va[:, vs], jnp.where(hm, k_st[:, ps], zero))
            st_ref[h] = st * d_last[:, ps] + upd
            gate = z_ref[rows, C_G + h * GLA_DV:C_G + (h + 1) * GLA_DV]
            o = _rmsnorm(o, glag_ref[h:h + 1, :]) * _silu(gate)
            merged_ref[rows, vs] = o.astype(BF16)
        u = _gelu(z_ref[rows, C_U:C_U + GMLP_WIDTH])
        vg = _layernorm(_gelu(z_ref[rows, C_VB:C_VB + GMLP_WIDTH]), lng_ref[...], lnb_ref[...])
        _gmlp_heads(u, vg, lambda h: jnp.where(causal, ws_ref[h], 0.0).astype(BF16), bsf_ref[...],
                    gmg_ref, merged_ref, rows)

    h_ref[0] = _out_proj_ln(x, merged_ref, wout_ref, ln1g_ref, ln1b_ref)

    @pl.when(t == pl.num_programs(1) - 1)
    def _():
        for h in range(GLA_HEADS):
            lo = (h % 2) * GLA_DK
            state_ref[0, h] = st_ref[h].T[lo:lo + GLA_DK, :]


def _mixer_sample_kernel(seq_len, x_ref, s0_ref, win_ref, wkt_ref, walrt_ref, waup_ref, waupt_ref, ba_ref, bac_ref,
                         glag_ref, lng_ref, lnb_ref, wst_ref, bsf_ref, gmg_ref, wout_ref, ln1g_ref, ln1b_ref,
                         h_ref, snew_ref, vg_ref,
                         merged_ref):
    n = SAMPLE_SEQS * seq_len
    x = x_ref[...]
    xb = x.astype(BF16)
    z = _dot(xb, win_ref[...])

    ti = lax.broadcasted_iota(jnp.int32, (n, n), 0)
    tj = lax.broadcasted_iota(jnp.int32, (n, n), 1)
    same = _shr(ti, seq_len) == _shr(tj, seq_len)
    causal = jnp.logical_and(same, ti >= tj)
    tri = jnp.where(causal, 1.0, 0.0).astype(BF16)
    tri_t = jnp.where(jnp.logical_and(same, ti <= tj), 1.0, 0.0).astype(BF16)
    same01 = jnp.where(same, 1.0, 0.0).astype(BF16)

    a_pre = _dot(z[:, C_ALR:C_ALR + LANES].astype(BF16), waup_ref[...]) + ba_ref[...]
    log_a = _log_sigmoid(a_pre) * (1.0 / GLA_TAU)
    b = _split_dot(tri, log_a)
    q = z[:, C_Q:C_Q + QK_WIDTH] * (GLA_DK ** -0.5)
    k = z[:, C_K:C_K + QK_WIDTH]
    q_in = (q * jnp.exp(b)).astype(BF16)
    k_in = (k * jnp.exp(-b)).astype(BF16)
    va = z[:, C_VA:C_VA + GLA_WIDTH].astype(BF16)

    k_t = _dot_nt(wkt_ref[...], xb)
    alr_t = _dot_nt(walrt_ref[...], xb)
    a_pre_t = _dot(waupt_ref[...], alr_t.astype(BF16)) + bac_ref[...]
    log_a_t = _log_sigmoid(a_pre_t) * (1.0 / GLA_TAU)
    b_t = _split_dot_r(log_a_t, tri_t)
    tot_t = _split_dot_r(log_a_t, same01)
    k_st_t = k_t * jnp.exp(tot_t - b_t)
    d_t = jnp.exp(tot_t)

    lane = lax.broadcasted_iota(jnp.int32, (1, LANES), 1)
    head_lanes = [lane < GLA_DK, lane >= GLA_DK]
    nrow = SAMPLE_SEQS * GLA_DK
    r_seq = _shr(lax.broadcasted_iota(jnp.int32, (nrow, n), 0), GLA_DK)
    c_seq = _shr(lax.broadcasted_iota(jnp.int32, (nrow, n), 1), seq_len)
    c_first = (lax.broadcasted_iota(jnp.int32, (nrow, n), 1) & (seq_len - 1)) == 0
    blk = r_seq == c_seq
    blk_first = jnp.logical_and(blk, c_first)
    q_seq = _shr(lax.broadcasted_iota(jnp.int32, (n, nrow), 0), seq_len)
    q_col = _shr(lax.broadcasted_iota(jnp.int32, (n, nrow), 1), GLA_DK)
    blk_q = q_seq == q_col

    for h in range(GLA_HEADS):
        ps = slice((h // 2) * LANES, (h // 2 + 1) * LANES)
        ds_ = slice(h * GLA_DK, (h + 1) * GLA_DK)
        vs = slice(h * GLA_DV, (h + 1) * GLA_DV)
        hm = head_lanes[h % 2]
        zero = jnp.zeros((), BF16)
        a = _dot_nt(jnp.where(hm, q_in[:, ps], zero), k_in[:, ps])
        a = jnp.where(causal, a, 0.0).astype(BF16)
        s0 = s0_ref[:, h].reshape(nrow, GLA_DV)
        q_h = q_in[:, ds_]
        q_bd = jnp.where(blk_q, jnp.concatenate([q_h] * SAMPLE_SEQS, axis=1), zero)
        o = _dot(a, va[:, vs]) + _dot(q_bd, s0.astype(BF16))
        k_bd = jnp.where(blk, jnp.concatenate([k_st_t[ds_, :]] * SAMPLE_SEQS, axis=0), 0.0).astype(BF16)
        upd = _dot(k_bd, va[:, vs])
        d_bd = jnp.where(blk_first, jnp.concatenate([d_t[ds_, :]] * SAMPLE_SEQS, axis=0), 0.0)
        d_col = jnp.sum(d_bd, axis=1, keepdims=True)
        snew_ref[:, h] = (s0 * d_col + upd).reshape(SAMPLE_SEQS, GLA_DK, GLA_DV)
        gate = z[:, C_G + h * GLA_DV:C_G + (h + 1) * GLA_DV]
        o = _rmsnorm(o, glag_ref[h:h + 1, :]) * _silu(gate)
        merged_ref[:, vs] = o.astype(BF16)

    u = _gelu(z[:, C_U:C_U + GMLP_WIDTH])
    vg = _layernorm(_gelu(z[:, C_VB:C_VB + GMLP_WIDTH]), lng_ref[...], lnb_ref[...])
    vg_ref[...] = vg
    _gmlp_heads(u, vg, lambda h: jnp.where(causal, wst_ref[h], 0.0).astype(BF16), bsf_ref[...],
                gmg_ref, merged_ref, slice(None))
    h_ref[...] = _out_proj_ln(x, merged_ref, wout_ref, ln1g_ref, ln1b_ref)


def _route(hb, wrt_ref, rbias_ref):
    tm = hb.shape[0]
    s = _sigmoid(_dot_nt(wrt_ref[...], hb))
    sb = s + rbias_ref[...]
    neg = jnp.float32(-jnp.inf)
    sub = lax.broadcasted_iota(jnp.int32, (GROUP_SIZE, tm), 0)
    gscore = []
    for g in range(N_GROUPS):
        blk = sb[g * GROUP_SIZE:(g + 1) * GROUP_SIZE, :]
        m1 = jnp.max(blk, axis=0, keepdims=True)
        i1 = jnp.min(jnp.where(blk == m1, sub, GROUP_SIZE), axis=0, keepdims=True)
        m2 = jnp.max(jnp.where(sub == i1, neg, blk), axis=0, keepdims=True)
        gscore.append(m1 + m2)
    gsel = []
    for g in range(N_GROUPS):
        rank = jnp.zeros((1, tm), jnp.int32)
        for o in range(N_GROUPS):
            if o == g:
                continue
            ahead = (gscore[o] >= gscore[g]) if o < g else (gscore[o] > gscore[g])
            rank = rank + ahead.astype(jnp.int32)
        gsel.append(rank < TOPK_GROUPS)
    v = jnp.concatenate(
        [jnp.where(gsel[g], sb[g * GROUP_SIZE:(g + 1) * GROUP_SIZE, :], neg) for g in range(N_GROUPS)], axis=0)
    eidx = lax.broadcasted_iota(jnp.int32, (N_EXPERTS, tm), 0)
    sel = jnp.zeros((N_EXPERTS, tm), F32)
    hits = []
    for _ in range(TOP_K):
        m = jnp.max(v, axis=0, keepdims=True)
        first = jnp.min(jnp.where(v == m, eidx, N_EXPERTS), axis=0, keepdims=True)
        hit = eidx == first
        hits.append(hit)
        sel = jnp.where(hit, 1.0, sel)
        v = jnp.where(hit, neg, v)
    wsel = s * sel
    return wsel / jnp.sum(wsel, axis=0, keepdims=True) * ROUTE_SCALE, sel, hits


def _slot_onehot(pos_rows, val_rows, n_rows):
    tm = pos_rows[0].shape[-1]
    r = lax.broadcasted_iota(jnp.int32, (n_rows, tm), 0).astype(F32)
    out = jnp.zeros((n_rows, tm), F32)
    for p, v in zip(pos_rows, val_rows):
        out = out + jnp.where(r == p, v, 0.0)
    return out


def _dispatch_kernel(n_prompt_blocks, n_blocks, hp_ref, hs_ref, wrt_ref, rbias_ref, x_ref, pos_ref, wt_ref,
                     seg_ref):
    i = pl.program_id(0)

    @pl.when(i < n_blocks)
    def _():
        _dispatch_block(i < n_prompt_blocks, hp_ref, hs_ref, wrt_ref, rbias_ref, x_ref, pos_ref, wt_ref, seg_ref)

    @pl.when(i >= n_blocks)
    def _():
        x_ref[...] = jnp.zeros_like(x_ref)
        pos_ref[...] = jnp.zeros_like(pos_ref)
        wt_ref[...] = jnp.zeros_like(wt_ref)
        seg_ref[...] = jnp.zeros_like(seg_ref)


def _dispatch_block(is_prompt, hp_ref, hs_ref, wrt_ref, rbias_ref, x_ref, pos_ref, wt_ref, seg_ref):
    tm = BLOCK
    hb = jnp.where(is_prompt, hp_ref[...], hs_ref[...]).astype(BF16)
    comb, sel, hits = _route(hb, wrt_ref, rbias_ref)
    ti = lax.broadcasted_iota(jnp.int32, (tm, tm), 0)
    tj = lax.broadcasted_iota(jnp.int32, (tm, tm), 1)
    before = jnp.where(ti < tj, 1.0, 0.0).astype(BF16)
    rank = _dot(sel.astype(BF16), before)
    cnt = jnp.sum(sel, axis=1, keepdims=True)
    seg = jnp.floor((cnt + (SEG - 1.0)) * (1.0 / SEG)) * SEG
    ei = lax.broadcasted_iota(jnp.int32, (N_EXPERTS, N_EXPERTS), 0)
    ej = lax.broadcasted_iota(jnp.int32, (N_EXPERTS, N_EXPERTS), 1)
    below = jnp.where(ej < ei, 1.0, 0.0).astype(BF16)
    seg_b = jnp.broadcast_to(seg, (N_EXPERTS, LANES))
    start = _dot(below, seg_b.astype(BF16))[:, 0:1]
    pos_all = start + rank
    pos_rows = [jnp.sum(jnp.where(h, pos_all, 0.0), axis=0, keepdims=True) for h in hits]
    wt_rows = [jnp.sum(jnp.where(h, comb, 0.0), axis=0, keepdims=True) for h in hits]
    ones = jnp.ones((1, tm), F32)
    gather = _slot_onehot(pos_rows, [ones] * TOP_K, BLOCK_ROWS).astype(BF16)
    for c in range(BLOCK_ROWS // GATHER_ROWS):
        rs = slice(c * GATHER_ROWS, (c + 1) * GATHER_ROWS)
        x_ref[rs, :] = _dot(gather[rs, :], hb).astype(BF16)
    pos_ref[0] = jnp.concatenate(pos_rows, axis=0)
    wt_ref[0] = jnp.concatenate(wt_rows, axis=0)
    seg_ref[0] = seg_b


def _ffn_kernel(dump_base, nt_max, tile_expert_ref, tile_in_expert_ref, n_chunks_ref, n_tiles_ref,
                tab0_ref, tab1_ref, tab2_ref, x_hbm, wg_ref, wu_ref, wd_ref,
                o_hbm, ibuf, obuf, wgb, wub, wdb, in_sem, out_sem):
    i = pl.program_id(0)
    n_tiles = n_tiles_ref[0]
    tabs = (tab0_ref, tab1_ref, tab2_ref)

    def start_in(ahead, slot):
        exists = i + ahead < n_tiles
        first = jnp.where(exists, tile_in_expert_ref[jnp.minimum(i + ahead, nt_max - 1)] * TILE_CHUNKS, 0)
        srcs = [jnp.where(exists, tabs[ahead][0, 0, first + j], ZERO_CHUNK) for j in range(TILE_CHUNKS)]
        for j in range(TILE_CHUNKS):
            pltpu.make_async_copy(x_hbm.at[srcs[j]], ibuf.at[slot, j], in_sem.at[slot]).start()

    def start_out(slot):
        e = tile_expert_ref[i]
        first = tile_in_expert_ref[i] * TILE_CHUNKS
        n_real = n_chunks_ref[e] - first
        dsts = [jnp.where(j < n_real, tab0_ref[0, 0, first + j], dump_base + slot * TILE_CHUNKS + j)
                for j in range(TILE_CHUNKS)]
        for j in range(TILE_CHUNKS):
            pltpu.make_async_copy(obuf.at[slot, j], o_hbm.at[dsts[j]], out_sem.at[slot]).start()

    def in_wait(slot):
        for j in range(TILE_CHUNKS):
            pltpu.make_async_copy(x_hbm.at[0], ibuf.at[slot, j], in_sem.at[slot]).wait()

    def out_wait(slot):
        for j in range(TILE_CHUNKS):
            pltpu.make_async_copy(obuf.at[slot, j], o_hbm.at[0], out_sem.at[slot]).wait()

    @pl.when(i < n_tiles)
    def _():
        islot = lax.rem(i, IN_BUFS)
        nslot = jnp.where(islot == 0, IN_BUFS - 1, islot - 1)
        oslot = lax.rem(i, 2)

        @pl.when(i == 0)
        def _():
            start_in(0, 0)
            start_in(1, 1)

        @pl.when(i >= 2)
        def _():
            out_wait(oslot)

        in_wait(islot)

        @pl.when(tile_in_expert_ref[i] == 0)
        def _():
            wgb[...] = wg_ref[0].astype(BF16)
            wub[...] = wu_ref[0].astype(BF16)
            wdb[...] = wd_ref[0].astype(BF16)

        start_in(2, nslot)
        for s in range(TILE_CHUNKS // SUB_CHUNKS):
            cs = slice(s * SUB_CHUNKS, (s + 1) * SUB_CHUNKS)
            x = ibuf[islot, cs].reshape(SUB_CHUNKS * SEG, D_MODEL)
            hid = _silu(_dot(x, wgb[...])) * _dot(x, wub[...])
            obuf[oslot, cs] = _dot(hid.astype(BF16), wdb[...]).astype(BF16).reshape(SUB_CHUNKS, SEG, D_MODEL)
        start_out(oslot)

        @pl.when(i == n_tiles - 1)
        def _():
            in_wait(lax.rem(i + 1, IN_BUFS))
            in_wait(nslot)
            out_wait(oslot)

            @pl.when(i >= 1)
            def _():
                out_wait(1 - oslot)


def _combine_kernel(h_ref, o_ref, pos_ref, wt_ref, wsg_ref, wsu_ref, wsd_ref, ln2g_ref, ln2b_ref, y_ref):
    h = h_ref[...]
    hb = h.astype(BF16)
    pos_rows = [pos_ref[0, k:k + 1, :] for k in range(TOP_K)]
    wt_rows = [wt_ref[0, k:k + 1, :] for k in range(TOP_K)]
    scatter = _slot_onehot(pos_rows, wt_rows, BLOCK_ROWS).astype(BF16)
    routed = _dot_tn(scatter, o_ref[...])
    shared = _dot((_silu(_dot(hb, wsg_ref[...])) * _dot(hb, wsu_ref[...])).astype(BF16), wsd_ref[...])
    y_ref[...] = _layernorm(ALPHA * h + (routed + shared), ln2g_ref[...], ln2b_ref[...])


def _full(shape):
    return pl.BlockSpec(shape, lambda *_: (0,) * len(shape))


def _mixer_prompt(x, wts):
    bsz, seq, _ = x.shape
    n_t = seq // PROMPT_TILE
    weights = [wts[k] for k in ("win", "waup", "ba", "glag", "lng", "lnb", "ws", "bsf_prompt", "gmg", "wout",
                                "ln1g", "ln1b")]
    return pl.pallas_call(
        _mixer_prompt_kernel,
        grid=(bsz, n_t),
        in_specs=[pl.BlockSpec((1, PROMPT_TILE, D_MODEL), lambda b, t: (b, t, 0))]
        + [_full(w.shape) for w in weights],
        out_specs=[pl.BlockSpec((1, PROMPT_TILE, D_MODEL), lambda b, t: (b, t, 0)),
                   pl.BlockSpec((1, GLA_HEADS, GLA_DK, GLA_DV), lambda b, t: (b, 0, 0, 0))],
        out_shape=[jax.ShapeDtypeStruct((bsz, seq, D_MODEL), F32),
                   jax.ShapeDtypeStruct((bsz, GLA_HEADS, GLA_DK, GLA_DV), F32)],
        scratch_shapes=[pltpu.VMEM((PROMPT_TILE, N_PROJ), F32),
                        pltpu.VMEM((PROMPT_TILE, D_MODEL), BF16),
                        pltpu.VMEM((GLA_HEADS, GLA_DV, LANES), F32)],
        compiler_params=pltpu.CompilerParams(dimension_semantics=("arbitrary", "arbitrary"),
                                             vmem_limit_bytes=VMEM_LIMIT),
        name="mixer_prompt",
    )(x, *weights)


def _mixer_sample(x, s0, wts):
    bsz, seq_len, _ = x.shape
    n = SAMPLE_SEQS * seq_len
    x2 = x.reshape(bsz * seq_len, D_MODEL)
    weights = [wts[k] for k in ("win", "wkt", "walrt", "waup", "waupt", "ba", "bac", "glag", "lng", "lnb",
                                "ws_sample", "bsf_sample", "gmg", "wout", "ln1g", "ln1b")]
    state_spec = pl.BlockSpec((SAMPLE_SEQS, GLA_HEADS, GLA_DK, GLA_DV), lambda i: (i, 0, 0, 0))
    h, s_new, vg = pl.pallas_call(
        functools.partial(_mixer_sample_kernel, seq_len),
        grid=(bsz // SAMPLE_SEQS,),
        in_specs=[pl.BlockSpec((n, D_MODEL), lambda i: (i, 0)), state_spec] + [_full(w.shape) for w in weights],
        out_specs=[pl.BlockSpec((n, D_MODEL), lambda i: (i, 0)), state_spec,
                   pl.BlockSpec((n, GMLP_WIDTH), lambda i: (i, 0))],
        out_shape=[jax.ShapeDtypeStruct((bsz * seq_len, D_MODEL), F32),
                   jax.ShapeDtypeStruct(s0.shape, F32),
                   jax.ShapeDtypeStruct((bsz * seq_len, GMLP_WIDTH), F32)],
        scratch_shapes=[pltpu.VMEM((n, D_MODEL), BF16)],
        compiler_params=pltpu.CompilerParams(dimension_semantics=("arbitrary",), vmem_limit_bytes=VMEM_LIMIT),
        name="mixer_sample",
    )(x2, s0, *weights)
    return h, s_new, vg.reshape(bsz, seq_len, GMLP_WIDTH)


def _dispatch(h_p, h_s, wts):
    npb, nsb = h_p.shape[0] // BLOCK, h_s.shape[0] // BLOCK
    nb_real = npb + nsb
    nb = nb_real + DUMP_BLOCKS
    slot_spec = pl.BlockSpec((1, TOP_K, BLOCK), lambda i: (i, 0, 0))
    return pl.pallas_call(
        functools.partial(_dispatch_kernel, npb, nb_real),
        grid=(nb,),
        in_specs=[pl.BlockSpec((BLOCK, D_MODEL), lambda i: (jnp.minimum(i, npb - 1), 0)),
                  pl.BlockSpec((BLOCK, D_MODEL), lambda i: (jnp.clip(i - npb, 0, nsb - 1), 0)),
                  _full(wts["wrt"].shape), _full(wts["rbias"].shape)],
        out_specs=[pl.BlockSpec((BLOCK_ROWS, D_MODEL), lambda i: (i, 0)), slot_spec, slot_spec,
                   pl.BlockSpec((1, N_EXPERTS, LANES), lambda i: (i, 0, 0))],
        out_shape=[jax.ShapeDtypeStruct((nb * BLOCK_ROWS, D_MODEL), BF16),
                   jax.ShapeDtypeStruct((nb, TOP_K, BLOCK), F32),
                   jax.ShapeDtypeStruct((nb, TOP_K, BLOCK), F32),
                   jax.ShapeDtypeStruct((nb, N_EXPERTS, LANES), F32)],
        compiler_params=pltpu.CompilerParams(dimension_semantics=("arbitrary",), vmem_limit_bytes=VMEM_LIMIT),
        name="moe_dispatch",
    )(h_p, h_s, wts["wrt"], wts["rbias"])


def _max_tiles(nb):
    return (nb * (BLOCK_ROWS_USED // SEG) + N_EXPERTS * (TILE_CHUNKS - 1)) // TILE_CHUNKS + 1


def _max_expert_chunks(nb):
    most = nb * (BLOCK // SEG) + TILE_CHUNKS - 1
    return -(-most // LANES) * LANES


def _plan(seg_lanes, nb):
    nt_max = _max_tiles(nb)
    max_chunks = _max_expert_chunks(nb)
    seg = seg_lanes[:nb, :, 0].astype(jnp.int32)
    start = jnp.cumsum(seg, axis=1) - seg
    nch = seg // SEG
    ends = jnp.cumsum(nch, axis=0)
    n_chunks = ends[-1]
    f = jnp.arange(nb, dtype=jnp.int32)[:, None] * (BLOCK_ROWS // SEG) + start // SEG - (ends - nch)
    q = jnp.arange(max_chunks, dtype=jnp.int32)
    passed = (ends[None, :-1, :] <= q[:, None, None]).astype(jnp.int32)
    chunk = q[:, None] + f[0][None, :] + jnp.sum(passed * (f[1:] - f[:-1])[None], axis=1)
    table = jnp.where(q[:, None] < n_chunks[None, :], chunk, ZERO_CHUNK).T
    n_tiles_e = -(-n_chunks // TILE_CHUNKS)
    tile_ends = jnp.cumsum(n_tiles_e)
    t = jnp.arange(nt_max, dtype=jnp.int32)
    done = (tile_ends[None, :] <= t[:, None]).astype(jnp.int32)
    tile_expert = jnp.minimum(jnp.sum(done, axis=1), N_EXPERTS - 1)
    tile_in_expert = t - jnp.sum(done * n_tiles_e[None, :], axis=1)
    return (table.astype(jnp.int32).reshape(N_EXPERTS, 1, max_chunks), tile_expert.astype(jnp.int32),
            tile_in_expert.astype(jnp.int32), n_chunks.astype(jnp.int32), tile_ends[-1:].astype(jnp.int32))


def _ffn(x_rows, nb, table, tile_expert, tile_in_expert, n_chunks, n_tiles, w_gate, w_up, w_down):
    nt_max = tile_expert.shape[0]
    max_chunks = table.shape[-1]
    tab_spec = lambda ahead: pl.BlockSpec(
        (1, 1, max_chunks), lambda i, te, *_: (te[jnp.minimum(i + ahead, nt_max - 1)], 0, 0),
        memory_space=pltpu.SMEM)
    w_spec = lambda shape: pl.BlockSpec((1,) + shape, lambda i, te, *_: (te[i], 0, 0))
    assert x_rows.shape[0] - nb * BLOCK_ROWS >= 2 * TILE_ROWS
    x_chunks = x_rows.reshape(x_rows.shape[0] // SEG, SEG, D_MODEL)
    return pl.pallas_call(
        functools.partial(_ffn_kernel, nb * BLOCK_ROWS // SEG, nt_max),
        grid_spec=pltpu.PrefetchScalarGridSpec(
            num_scalar_prefetch=4,
            grid=(nt_max,),
            in_specs=[tab_spec(0), tab_spec(1), tab_spec(2), pl.BlockSpec(memory_space=pl.ANY),
                      w_spec((D_MODEL, D_EXPERT)), w_spec((D_MODEL, D_EXPERT)), w_spec((D_EXPERT, D_MODEL))],
            out_specs=pl.BlockSpec(memory_space=pl.ANY),
            scratch_shapes=[pltpu.VMEM((IN_BUFS, TILE_CHUNKS, SEG, D_MODEL), BF16),
                            pltpu.VMEM((2, TILE_CHUNKS, SEG, D_MODEL), BF16),
                            pltpu.VMEM((D_MODEL, D_EXPERT), BF16),
                            pltpu.VMEM((D_MODEL, D_EXPERT), BF16),
                            pltpu.VMEM((D_EXPERT, D_MODEL), BF16),
                            pltpu.SemaphoreType.DMA((IN_BUFS,)),
                            pltpu.SemaphoreType.DMA((2,))]),
        out_shape=jax.ShapeDtypeStruct(x_chunks.shape, BF16),
        input_output_aliases={7: 0},
        compiler_params=pltpu.CompilerParams(dimension_semantics=("arbitrary",), vmem_limit_bytes=VMEM_LIMIT),
        name="moe_ffn",
    )(tile_expert, tile_in_expert, n_chunks, n_tiles, table, table, table, x_chunks, w_gate, w_up, w_down
      ).reshape(x_rows.shape)


def _combine(h, o_rows, pos, wt, block_off, wts):
    nblk = h.shape[0] // BLOCK
    weights = [wts[k] for k in ("wsg", "wsu", "wsd", "ln2g", "ln2b")]
    slot_spec = pl.BlockSpec((1, TOP_K, BLOCK), lambda i: (i + block_off, 0, 0))
    return pl.pallas_call(
        _combine_kernel,
        grid=(nblk,),
        in_specs=[pl.BlockSpec((BLOCK, D_MODEL), lambda i: (i, 0)),
                  pl.BlockSpec((BLOCK_ROWS, D_MODEL), lambda i: (i + block_off, 0)),
                  slot_spec, slot_spec] + [_full(w.shape) for w in weights],
        out_specs=pl.BlockSpec((BLOCK, D_MODEL), lambda i: (i, 0)),
        out_shape=jax.ShapeDtypeStruct(h.shape, F32),
        compiler_params=pltpu.CompilerParams(dimension_semantics=("arbitrary",), vmem_limit_bytes=VMEM_LIMIT),
        name="moe_combine",
    )(h, o_rows, pos, wt, *weights)


def _moe(h_p, h_s, w_gate, w_up, w_down, wts):
    nb = (h_p.shape[0] + h_s.shape[0]) // BLOCK
    x_rows, pos, wt, seg_lanes = _dispatch(h_p, h_s, wts)
    o_rows = _ffn(x_rows, nb, *_plan(seg_lanes, nb), w_gate, w_up, w_down)
    y_p = _combine(h_p, o_rows, pos, wt, 0, wts)
    y_s = _combine(h_s, o_rows, pos, wt, h_p.shape[0] // BLOCK, wts)
    return y_p, y_s


def _prep_weights(seq_len, w_in, w_a_up, b_a, gla_norm_g, gmlp_ln_g, gmlp_ln_b, w_s, b_s, gmlp_norm_g, w_out,
                  ln1_g, ln1_b, w_router, router_bias, ws_gate, ws_up, ws_down, ln2_g, ln2_b):
    o1 = QK_WIDTH
    o2 = o1 + QK_WIDTH
    o3 = o2 + GLA_WIDTH
    o4 = o3 + GLA_WIDTH
    o5 = o4 + GLA_RANK
    o6 = o5 + GMLP_WIDTH
    wq, wk, wva, wg_, walr, wu_, wvb = jnp.split(w_in, [o1, o2, o3, o4, o5, o6], axis=-1)
    walr_p = jnp.pad(walr, ((0, 0), (0, LANES - GLA_RANK)))
    waup_p = jnp.pad(w_a_up, ((0, LANES - GLA_RANK), (0, 0)))
    row = lambda a: a.reshape(1, -1)
    reps = (SAMPLE_SEQS * seq_len) // seq_len
    ws_small = w_s[:, :seq_len, :seq_len]
    return {
        "win": jnp.concatenate([wq, wk, wva, wg_, wu_, wvb, walr_p], axis=-1).astype(BF16),
        "wkt": wk.T.astype(BF16),
        "walrt": walr_p.T.astype(BF16),
        "waup": waup_p.astype(BF16),
        "waupt": waup_p.T.astype(BF16),
        "ba": row(b_a), "bac": b_a.reshape(-1, 1),
        "glag": gla_norm_g, "lng": row(gmlp_ln_g), "lnb": row(gmlp_ln_b),
        "ws": w_s,
        "ws_sample": jnp.tile(ws_small, (1, reps, reps)),
        "bsf_prompt": jnp.repeat(b_s[:, :GMLP_CHUNK].T, GMLP_DH, axis=1),
        "bsf_sample": jnp.tile(jnp.repeat(b_s[:, :seq_len].T, GMLP_DH, axis=1), (reps, 1)),
        "gmg": gmlp_norm_g,
        "wout": w_out.astype(BF16),
        "ln1g": row(ln1_g), "ln1b": row(ln1_b),
        "wrt": w_router.T.astype(BF16), "rbias": router_bias.reshape(-1, 1),
        "wsg": ws_gate.astype(BF16), "wsu": ws_up.astype(BF16), "wsd": ws_down.astype(BF16),
        "ln2g": row(ln2_g), "ln2b": row(ln2_b),
    }


def kernel(x_prompt, x_sample, state_gla, w_in, w_a_up, b_a, gla_norm_g, gmlp_ln_g, gmlp_ln_b, w_s, b_s,
           gmlp_norm_g, w_out, ln1_g, ln1_b, w_router, router_bias, w_gate, w_up, w_down, ws_gate, ws_up,
           ws_down, ln2_g, ln2_b):
    assert x_prompt.shape[1] % PROMPT_TILE == 0 and x_sample.shape[0] % SAMPLE_SEQS == 0
    assert x_sample.shape[1] <= GMLP_CHUNK and w_in.shape[0] == DEPTH
    assert (x_sample.shape[0] * x_sample.shape[1]) % BLOCK == 0 and PROMPT_TILE % BLOCK == 0
    bsz, seq, _ = x_prompt.shape
    dbsz, dseq, _ = x_sample.shape
    hp, hs = x_prompt, x_sample
    gla_p, gla_s, v_s = [], [], []
    for l in range(DEPTH):
        wts = _prep_weights(dseq, w_in[l], w_a_up[l], b_a[l], gla_norm_g[l], gmlp_ln_g[l], gmlp_ln_b[l], w_s[l],
                            b_s[l], gmlp_norm_g[l], w_out[l], ln1_g[l], ln1_b[l], w_router[l], router_bias[l],
                            ws_gate[l], ws_up[l], ws_down[l], ln2_g[l], ln2_b[l])
        h_p, sp = _mixer_prompt(hp, wts)
        h_s, ss, vrows = _mixer_sample(hs, state_gla[l], wts)
        y_p, y_s = _moe(h_p.reshape(bsz * seq, D_MODEL), h_s, w_gate[l], w_up[l], w_down[l], wts)
        hp = y_p.reshape(bsz, seq, D_MODEL)
        hs = y_s.reshape(dbsz, dseq, D_MODEL)
        gla_p.append(sp)
        gla_s.append(ss)
        v_s.append(vrows)
    return (hp, hs, jnp.stack(gla_p), jnp.stack(gla_s), jnp.stack(v_s))
```

```python
import functools
import math

import jax
import jax.numpy as jnp
from jax import lax
from jax.experimental import pallas as pl
from jax.experimental.pallas import tpu as pltpu

F32 = jnp.float32
BF16 = jnp.bfloat16

D_MODEL = 1024
DEPTH = 1
GLA_WIDTH = 512
GLA_HEADS = 4
GLA_DK = 64
GLA_DV = 128
GLA_RANK = 16
GLA_TAU = 16.0
GMLP_WIDTH = 512
GMLP_HEADS = 4
GMLP_DH = 128
GMLP_CHUNK = 128
QK_WIDTH = GLA_HEADS * GLA_DK
N_EXPERTS = 64
TOP_K = 8
N_GROUPS = 8
GROUP_SIZE = N_EXPERTS // N_GROUPS
TOPK_GROUPS = 4
D_EXPERT = 256
D_SHARED = 256
ROUTE_SCALE = 2.5
ALPHA = (2.0 * DEPTH) ** 0.25

LANES = 128

C_Q = 0
C_K = C_Q + QK_WIDTH
C_VA = C_K + QK_WIDTH
C_G = C_VA + GLA_WIDTH
C_U = C_G + GLA_WIDTH
C_VB = C_U + GMLP_WIDTH
C_ALR = C_VB + GMLP_WIDTH
N_PROJ = C_ALR + LANES

CHUNK = 128
PROMPT_TILE = 256
SAMPLE_SEQS = 32
VMEM_LIMIT = 56 * 1024 * 1024

BLOCK = 256
SEG = 16
TILE_CHUNKS = 32
SUB_CHUNKS = 16
TILE_ROWS = TILE_CHUNKS * SEG
GATHER_ROWS = 512
BLOCK_ROWS_USED = BLOCK * TOP_K + N_EXPERTS * (SEG - 1)
BLOCK_ROWS = -(-(BLOCK_ROWS_USED + SEG) // GATHER_ROWS) * GATHER_ROWS
DUMP_BLOCKS = -(-(2 * TILE_ROWS) // BLOCK_ROWS)
ZERO_CHUNK = BLOCK_ROWS_USED // SEG
IN_BUFS = 3


def _dot(a, b):
    return jnp.dot(a, b, preferred_element_type=F32)


def _dot_nt(a, b):
    return lax.dot_general(a, b, (((1,), (1,)), ((), ())), preferred_element_type=F32)


def _dot_tn(a, b):
    return lax.dot_general(a, b, (((0,), (0,)), ((), ())), preferred_element_type=F32)


def _shr(x, d):
    assert d & (d - 1) == 0
    return lax.shift_right_logical(x, d.bit_length() - 1)


def _split_dot(m01, x):
    hi = x.astype(BF16)
    lo = (x - hi.astype(F32)).astype(BF16)
    return _dot(m01, hi) + _dot(m01, lo)


def _split_dot_r(x, m01):
    hi = x.astype(BF16)
    lo = (x - hi.astype(F32)).astype(BF16)
    return _dot(hi, m01) + _dot(lo, m01)


def _sigmoid(x):
    return 1.0 / (1.0 + jnp.exp(-x))


def _silu(x):
    return x * _sigmoid(x)


def _gelu(x):
    c = math.sqrt(2.0 / math.pi)
    return x * (0.5 * (1.0 + jnp.tanh(c * (x + 0.044715 * (x * x * x)))))


def _log_sigmoid(x):
    return -(jnp.maximum(-x, 0.0) + jnp.log1p(jnp.exp(-jnp.abs(x))))


def _layernorm(x, g, b, eps=1e-5):
    mu = jnp.mean(x, axis=-1, keepdims=True)
    xc = x - mu
    var = jnp.mean(xc * xc, axis=-1, keepdims=True)
    return xc * lax.rsqrt(var + eps) * g + b


def _rmsnorm(x, g, eps=1e-6):
    return x * lax.rsqrt(jnp.mean(x * x, axis=-1, keepdims=True) + eps) * g


def _gmlp_heads(u, vg, w_mix, bias, gmg_ref, merged_ref, rows):
    vgb = vg.astype(BF16)
    for h in range(GMLP_HEADS):
        cs = slice(h * GMLP_DH, (h + 1) * GMLP_DH)
        sgu = _dot(w_mix(h), vgb[:, cs]) + bias[:, cs]
        y = _rmsnorm(u[:, cs] * sgu, gmg_ref[h:h + 1, :])
        merged_ref[rows, GLA_WIDTH + h * GMLP_DH:GLA_WIDTH + (h + 1) * GMLP_DH] = y.astype(BF16)


def _out_proj_ln(x, merged_ref, wout_ref, ln1g_ref, ln1b_ref):
    m = _dot(merged_ref[...], wout_ref[...])
    return _layernorm(ALPHA * x + m, ln1g_ref[...], ln1b_ref[...])


def _mixer_prompt_kernel(x_ref, win_ref, waup_ref, ba_ref, glag_ref, lng_ref, lnb_ref, ws_ref, bsf_ref,
                         gmg_ref, wout_ref, ln1g_ref, ln1b_ref,
                         h_ref, state_ref,
                         z_ref, merged_ref, st_ref):
    t = pl.program_id(1)

    @pl.when(t == 0)
    def _():
        st_ref[...] = jnp.zeros_like(st_ref)

    x = x_ref[0]
    z_ref[...] = _dot(x.astype(BF16), win_ref[...])

    row_i = lax.broadcasted_iota(jnp.int32, (CHUNK, CHUNK), 0)
    col_i = lax.broadcasted_iota(jnp.int32, (CHUNK, CHUNK), 1)
    causal = row_i >= col_i
    tri = jnp.where(causal, 1.0, 0.0).astype(BF16)
    lane = lax.broadcasted_iota(jnp.int32, (1, LANES), 1)
    head_lanes = [lane < GLA_DK, lane >= GLA_DK]
    mid = CHUNK // 2 - 1

    for c in range(PROMPT_TILE // CHUNK):
        rows = slice(c * CHUNK, (c + 1) * CHUNK)
        a_pre = _dot(z_ref[rows, C_ALR:C_ALR + LANES].astype(BF16), waup_ref[...]) + ba_ref[...]
        log_a = _log_sigmoid(a_pre) * (1.0 / GLA_TAU)
        b = _split_dot(tri, log_a)
        b_mid = b[mid:mid + 1, :]
        b_last = b[CHUNK - 1:CHUNK, :]
        q = z_ref[rows, C_Q:C_Q + QK_WIDTH] * (GLA_DK ** -0.5)
        k = z_ref[rows, C_K:C_K + QK_WIDTH]
        q_in = (q * jnp.exp(b - b_mid)).astype(BF16)
        k_in = (k * jnp.exp(b_mid - b)).astype(BF16)
        q_st = (q * jnp.exp(b)).astype(BF16)
        k_st = (k * jnp.exp(b_last - b)).astype(BF16)
        d_last = jnp.exp(b_last)
        va = z_ref[rows, C_VA:C_VA + GLA_WIDTH].astype(BF16)
        for h in range(GLA_HEADS):
            ps = slice((h // 2) * LANES, (h // 2 + 1) * LANES)
            vs = slice(h * GLA_DV, (h + 1) * GLA_DV)
            hm = head_lanes[h % 2]
            zero = jnp.zeros((), BF16)
            a = _dot_nt(jnp.where(hm, q_in[:, ps], zero), k_in[:, ps])
            a = jnp.where(causal, a, 0.0).astype(BF16)
            st = st_ref[h]
            o = _dot(a, va[:, vs]) + _dot_nt(jnp.where(hm, q_st[:, ps], zero), st.astype(BF16))
            upd = _dot_tn(va[:, vs], jnp.where(hm, k_st[:, ps], zero))
            st_ref[h] = st * d_last[:, ps] + upd
            gate = z_ref[rows, C_G + h * GLA_DV:C_G + (h + 1) * GLA_DV]
            o = _rmsnorm(o, glag_ref[h:h + 1, :]) * _silu(gate)
            merged_ref[rows, vs] = o.astype(BF16)
        u = _gelu(z_ref[rows, C_U:C_U + GMLP_WIDTH])
        vg = _layernorm(_gelu(z_ref[rows, C_VB:C_VB + GMLP_WIDTH]), lng_ref[...], lnb_ref[...])
        _gmlp_heads(u, vg, lambda h: jnp.where(causal, ws_ref[h], 0.0).astype(BF16), bsf_ref[...],
                    gmg_ref, merged_ref, rows)

    h_ref[0] = _out_proj_ln(x, merged_ref, wout_ref, ln1g_ref, ln1b_ref)

    @pl.when(t == pl.num_programs(1) - 1)
    def _():
        for h in range(GLA_HEADS):
            lo = (h % 2) * GLA_DK
            state_ref[0, h] = st_ref[h].T[lo:lo + GLA_DK, :]


def _mixer_sample_kernel(seq_len, x_ref, s0_ref, win_ref, wkt_ref, walrt_ref, waup_ref, waupt_ref, ba_ref, bac_ref,
                         glag_ref, lng_ref, lnb_ref, wst_ref, bsf_ref, gmg_ref, wout_ref, ln1g_ref, ln1b_ref,
                         h_ref, snew_ref, vg_ref,
                         merged_ref):
    n = SAMPLE_SEQS * seq_len
    x = x_ref[...]
    xb = x.astype(BF16)
    z = _dot(xb, win_ref[...])

    ti = lax.broadcasted_iota(jnp.int32, (n, n), 0)
    tj = lax.broadcasted_iota(jnp.int32, (n, n), 1)
    same = _shr(ti, seq_len) == _shr(tj, seq_len)
    causal = jnp.logical_and(same, ti >= tj)
    tri = jnp.where(causal, 1.0, 0.0).astype(BF16)
    tri_t = jnp.where(jnp.logical_and(same, ti <= tj), 1.0, 0.0).astype(BF16)
    same01 = jnp.where(same, 1.0, 0.0).astype(BF16)

    a_pre = _dot(z[:, C_ALR:C_ALR + LANES].astype(BF16), waup_ref[...]) + ba_ref[...]
    log_a = _log_sigmoid(a_pre) * (1.0 / GLA_TAU)
    b = _split_dot(tri, log_a)
    q = z[:, C_Q:C_Q + QK_WIDTH] * (GLA_DK ** -0.5)
    k = z[:, C_K:C_K + QK_WIDTH]
    q_in = (q * jnp.exp(b)).astype(BF16)
    k_in = (k * jnp.exp(-b)).astype(BF16)
    va = z[:, C_VA:C_VA + GLA_WIDTH].astype(BF16)

    k_t = _dot_nt(wkt_ref[...], xb)
    alr_t = _dot_nt(walrt_ref[...], xb)
    a_pre_t = _dot(waupt_ref[...], alr_t.astype(BF16)) + bac_ref[...]
    log_a_t = _log_sigmoid(a_pre_t) * (1.0 / GLA_TAU)
    b_t = _split_dot_r(log_a_t, tri_t)
    tot_t = _split_dot_r(log_a_t, same01)
    k_st_t = k_t * jnp.exp(tot_t - b_t)
    d_t = jnp.exp(tot_t)

    lane = lax.broadcasted_iota(jnp.int32, (1, LANES), 1)
    head_lanes = [lane < GLA_DK, lane >= GLA_DK]
    nrow = SAMPLE_SEQS * GLA_DK
    r_seq = _shr(lax.broadcasted_iota(jnp.int32, (nrow, n), 0), GLA_DK)
    c_seq = _shr(lax.broadcasted_iota(jnp.int32, (nrow, n), 1), seq_len)
    c_first = (lax.broadcasted_iota(jnp.int32, (nrow, n), 1) & (seq_len - 1)) == 0
    blk = r_seq == c_seq
    blk_first = jnp.logical_and(blk, c_first)
    q_seq = _shr(lax.broadcasted_iota(jnp.int32, (n, nrow), 0), seq_len)
    q_col = _shr(lax.broadcasted_iota(jnp.int32, (n, nrow), 1), GLA_DK)
    blk_q = q_seq == q_col

    for h in range(GLA_HEADS):
        ps = slice((h // 2) * LANES, (h // 2 + 1) * LANES)
        ds_ = slice(h * GLA_DK, (h + 1) * GLA_DK)
        vs = slice(h * GLA_DV, (h + 1) * GLA_DV)
        hm = head_lanes[h % 2]
        zero = jnp.zeros((), BF16)
        a = _dot_nt(jnp.where(hm, q_in[:, ps], zero), k_in[:, ps])
        a = jnp.where(causal, a, 0.0).astype(BF16)
        s0 = s0_ref[:, h].reshape(nrow, GLA_DV)
        q_h = q_in[:, ds_]
        q_bd = jnp.where(blk_q, jnp.concatenate([q_h] * SAMPLE_SEQS, axis=1), zero)
        o = _dot(a, va[:, vs]) + _dot(q_bd, s0.astype(BF16))
        k_bd = jnp.where(blk, jnp.concatenate([k_st_t[ds_, :]] * SAMPLE_SEQS, axis=0), 0.0).astype(BF16)
        upd = _dot(k_bd, va[:, vs])
        d_bd = jnp.where(blk_first, jnp.concatenate([d_t[ds_, :]] * SAMPLE_SEQS, axis=0), 0.0)
        d_col = jnp.sum(d_bd, axis=1, keepdims=True)
        snew_ref[:, h] = (s0 * d_col + upd).reshape(SAMPLE_SEQS, GLA_DK, GLA_DV)
        gate = z[:, C_G + h * GLA_DV:C_G + (h + 1) * GLA_DV]
        o = _rmsnorm(o, glag_ref[h:h + 1, :]) * _silu(gate)
        merged_ref[:, vs] = o.astype(BF16)

    u = _gelu(z[:, C_U:C_U + GMLP_WIDTH])
    vg = _layernorm(_gelu(z[:, C_VB:C_VB + GMLP_WIDTH]), lng_ref[...], lnb_ref[...])
    vg_ref[...] = vg
    _gmlp_heads(u, vg, lambda h: jnp.where(causal, wst_ref[h], 0.0).astype(BF16), bsf_ref[...],
                gmg_ref, merged_ref, slice(None))
    h_ref[...] = _out_proj_ln(x, merged_ref, wout_ref, ln1g_ref, ln1b_ref)


def _route(hb, wrt_ref, rbias_ref):
    tm = hb.shape[0]
    s = _sigmoid(_dot_nt(wrt_ref[...], hb))
    sb = s + rbias_ref[...]
    neg = jnp.float32(-jnp.inf)
    sub = lax.broadcasted_iota(jnp.int32, (GROUP_SIZE, tm), 0)
    gscore = []
    for g in range(N_GROUPS):
        blk = sb[g * GROUP_SIZE:(g + 1) * GROUP_SIZE, :]
        m1 = jnp.max(blk, axis=0, keepdims=True)
        i1 = jnp.min(jnp.where(blk == m1, sub, GROUP_SIZE), axis=0, keepdims=True)
        m2 = jnp.max(jnp.where(sub == i1, neg, blk), axis=0, keepdims=True)
        gscore.append(m1 + m2)
    gsel = []
    for g in range(N_GROUPS):
        rank = jnp.zeros((1, tm), jnp.int32)
        for o in range(N_GROUPS):
            if o == g:
                continue
            ahead = (gscore[o] >= gscore[g]) if o < g else (gscore[o] > gscore[g])
            rank = rank + ahead.astype(jnp.int32)
        gsel.append(rank < TOPK_GROUPS)
    v = jnp.concatenate(
        [jnp.where(gsel[g], sb[g * GROUP_SIZE:(g + 1) * GROUP_SIZE, :], neg) for g in range(N_GROUPS)], axis=0)
    eidx = lax.broadcasted_iota(jnp.int32, (N_EXPERTS, tm), 0)
    sel = jnp.zeros((N_EXPERTS, tm), F32)
    for _ in range(TOP_K):
        m = jnp.max(v, axis=0, keepdims=True)
        first = jnp.min(jnp.where(v == m, eidx, N_EXPERTS), axis=0, keepdims=True)
        hit = eidx == first
        sel = jnp.where(hit, 1.0, sel)
        v = jnp.where(hit, neg, v)
    wsel = s * sel
    return wsel / jnp.sum(wsel, axis=0, keepdims=True) * ROUTE_SCALE, sel


def _per_row(chunk_vals):
    n, t = chunk_vals.shape
    return jnp.broadcast_to(chunk_vals[:, None, :], (n, SEG, t)).reshape(n * SEG, t)


def _row_lookup(seg_b, start_b, rank, tables):
    n_chunks = BLOCK_ROWS // SEG
    ei = lax.broadcasted_iota(jnp.int32, (N_EXPERTS, LANES), 0)
    ej = lax.broadcasted_iota(jnp.int32, (N_EXPERTS, LANES), 1)
    seg16 = (seg_b * (1.0 / SEG)).astype(BF16)
    start_row = _dot_tn(seg16, jnp.where(ei < ej, 1.0, 0.0).astype(BF16))[0:1, :]
    seg_row = _dot_tn(seg16, jnp.where(ei == ej, 1.0, 0.0).astype(BF16))[0:1, :]
    c = lax.broadcasted_iota(jnp.int32, (n_chunks, LANES), 0).astype(F32)
    owner = jnp.where(c >= start_row, jnp.where(c < start_row + seg_row, 1.0, 0.0), 0.0).astype(BF16)

    def lookup(tab):
        padded = jnp.concatenate([tab.astype(BF16), jnp.zeros((LANES - N_EXPERTS, tab.shape[1]), BF16)], axis=0)
        return _dot(owner, padded)

    first = SEG * (c - lookup(start_b * (1.0 / SEG)))[:, 0:1]
    rel = lookup(rank) - first
    sub = lax.broadcasted_iota(jnp.int32, (BLOCK_ROWS, rank.shape[1]), 0) & (SEG - 1)
    hits = _per_row(rel) == sub.astype(F32)
    return hits, [_per_row(lookup(tab)) for tab in tables]


def _dispatch_kernel(n_prompt_blocks, n_blocks, hp_ref, hs_ref, wrt_ref, rbias_ref, x_ref, rank_ref, comb_ref,
                     seg_ref, start_ref):
    i = pl.program_id(0)

    @pl.when(i < n_blocks)
    def _():
        _dispatch_block(i < n_prompt_blocks, hp_ref, hs_ref, wrt_ref, rbias_ref, x_ref, rank_ref, comb_ref, seg_ref,
                        start_ref)

    @pl.when(i >= n_blocks)
    def _():
        for ref in (x_ref, rank_ref, comb_ref, seg_ref, start_ref):
            ref[...] = jnp.zeros_like(ref)


def _dispatch_block(is_prompt, hp_ref, hs_ref, wrt_ref, rbias_ref, x_ref, rank_ref, comb_ref, seg_ref, start_ref):
    tm = BLOCK
    hb = jnp.where(is_prompt, hp_ref[...], hs_ref[...]).astype(BF16)
    comb, sel = _route(hb, wrt_ref, rbias_ref)
    ti = lax.broadcasted_iota(jnp.int32, (tm, tm), 0)
    tj = lax.broadcasted_iota(jnp.int32, (tm, tm), 1)
    before = jnp.where(ti < tj, 1.0, 0.0).astype(BF16)
    rank = _dot(sel.astype(BF16), before)
    rank = jnp.where(sel > 0.0, rank, -1.0)
    cnt = jnp.sum(sel, axis=1, keepdims=True)
    seg = jnp.floor((cnt + (SEG - 1.0)) * (1.0 / SEG)) * SEG
    ei = lax.broadcasted_iota(jnp.int32, (N_EXPERTS, N_EXPERTS), 0)
    ej = lax.broadcasted_iota(jnp.int32, (N_EXPERTS, N_EXPERTS), 1)
    below = jnp.where(ej < ei, 1.0, 0.0).astype(BF16)
    seg_b = jnp.broadcast_to(seg, (N_EXPERTS, LANES))
    start_b = _dot(below, seg_b.astype(BF16))
    hits, _ = _row_lookup(seg_b, start_b, rank, [])
    gather = jnp.where(hits, 1.0, 0.0).astype(BF16)
    for c in range(BLOCK_ROWS // GATHER_ROWS):
        rs = slice(c * GATHER_ROWS, (c + 1) * GATHER_ROWS)
        x_ref[rs, :] = _dot(gather[rs, :], hb).astype(BF16)
    rank_ref[0] = rank
    comb_ref[0] = comb
    seg_ref[0] = seg_b
    start_ref[0] = start_b


def _ffn_kernel(dump_base, nt_max, tile_expert_ref, tile_in_expert_ref, n_chunks_ref, n_tiles_ref,
                tab0_ref, tab1_ref, tab2_ref, x_hbm, wg_ref, wu_ref, wd_ref,
                o_hbm, ibuf, obuf, wgb, wub, wdb, in_sem, out_sem):
    i = pl.program_id(0)
    n_tiles = n_tiles_ref[0]
    tabs = (tab0_ref, tab1_ref, tab2_ref)

    def start_in(ahead, slot):
        exists = i + ahead < n_tiles
        first = jnp.where(exists, tile_in_expert_ref[jnp.minimum(i + ahead, nt_max - 1)] * TILE_CHUNKS, 0)
        srcs = [jnp.where(exists, tabs[ahead][0, 0, first + j], ZERO_CHUNK) for j in range(TILE_CHUNKS)]
        for j in range(TILE_CHUNKS):
            pltpu.make_async_copy(x_hbm.at[srcs[j]], ibuf.at[slot, j], in_sem.at[slot]).start()

    def start_out(slot):
        e = tile_expert_ref[i]
        first = tile_in_expert_ref[i] * TILE_CHUNKS
        n_real = n_chunks_ref[e] - first
        dsts = [jnp.where(j < n_real, tab0_ref[0, 0, first + j], dump_base + slot * TILE_CHUNKS + j)
                for j in range(TILE_CHUNKS)]
        for j in range(TILE_CHUNKS):
            pltpu.make_async_copy(obuf.at[slot, j], o_hbm.at[dsts[j]], out_sem.at[slot]).start()

    def in_wait(slot):
        for j in range(TILE_CHUNKS):
            pltpu.make_async_copy(x_hbm.at[0], ibuf.at[slot, j], in_sem.at[slot]).wait()

    def out_wait(slot):
        for j in range(TILE_CHUNKS):
            pltpu.make_async_copy(obuf.at[slot, j], o_hbm.at[0], out_sem.at[slot]).wait()

    @pl.when(i < n_tiles)
    def _():
        islot = lax.rem(i, IN_BUFS)
        nslot = jnp.where(islot == 0, IN_BUFS - 1, islot - 1)
        oslot = lax.rem(i, 2)

        @pl.when(i == 0)
        def _():
            start_in(0, 0)
            start_in(1, 1)

        @pl.when(i >= 2)
        def _():
            out_wait(oslot)

        in_wait(islot)

        @pl.when(tile_in_expert_ref[i] == 0)
        def _():
            wgb[...] = wg_ref[0].astype(BF16)
            wub[...] = wu_ref[0].astype(BF16)
            wdb[...] = wd_ref[0].astype(BF16)

        start_in(2, nslot)
        for s in range(TILE_CHUNKS // SUB_CHUNKS):
            cs = slice(s * SUB_CHUNKS, (s + 1) * SUB_CHUNKS)
            x = ibuf[islot, cs].reshape(SUB_CHUNKS * SEG, D_MODEL)
            hid = _silu(_dot(x, wgb[...])) * _dot(x, wub[...])
            obuf[oslot, cs] = _dot(hid.astype(BF16), wdb[...]).astype(BF16).reshape(SUB_CHUNKS, SEG, D_MODEL)
        start_out(oslot)

        @pl.when(i == n_tiles - 1)
        def _():
            in_wait(lax.rem(i + 1, IN_BUFS))
            in_wait(nslot)
            out_wait(oslot)

            @pl.when(i >= 1)
            def _():
                out_wait(1 - oslot)


def _combine_kernel(h_ref, o_ref, rank_ref, comb_ref, seg_ref, start_ref, wsg_ref, wsu_ref, wsd_ref, ln2g_ref,
                    ln2b_ref, y_ref):
    h = h_ref[...]
    hb = h.astype(BF16)
    hits, (comb_rows,) = _row_lookup(seg_ref[0], start_ref[0], rank_ref[0], [comb_ref[0]])
    scatter = jnp.where(hits, comb_rows, 0.0).astype(BF16)
    routed = _dot_tn(scatter, o_ref[...])
    shared = _dot((_silu(_dot(hb, wsg_ref[...])) * _dot(hb, wsu_ref[...])).astype(BF16), wsd_ref[...])
    y_ref[...] = _layernorm(ALPHA * h + (routed + shared), ln2g_ref[...], ln2b_ref[...])


def _full(shape):
    return pl.BlockSpec(shape, lambda *_: (0,) * len(shape))


def _mixer_prompt(x, wts):
    bsz, seq, _ = x.shape
    n_t = seq // PROMPT_TILE
    weights = [wts[k] for k in ("win", "waup", "ba", "glag", "lng", "lnb", "ws", "bsf_prompt", "gmg", "wout",
                                "ln1g", "ln1b")]
    return pl.pallas_call(
        _mixer_prompt_kernel,
        grid=(bsz, n_t),
        in_specs=[pl.BlockSpec((1, PROMPT_TILE, D_MODEL), lambda b, t: (b, t, 0))]
        + [_full(w.shape) for w in weights],
        out_specs=[pl.BlockSpec((1, PROMPT_TILE, D_MODEL), lambda b, t: (b, t, 0)),
                   pl.BlockSpec((1, GLA_HEADS, GLA_DK, GLA_DV), lambda b, t: (b, 0, 0, 0))],
        out_shape=[jax.ShapeDtypeStruct((bsz, seq, D_MODEL), F32),
                   jax.ShapeDtypeStruct((bsz, GLA_HEADS, GLA_DK, GLA_DV), F32)],
        scratch_shapes=[pltpu.VMEM((PROMPT_TILE, N_PROJ), F32),
                        pltpu.VMEM((PROMPT_TILE, D_MODEL), BF16),
                        pltpu.VMEM((GLA_HEADS, GLA_DV, LANES), F32)],
        compiler_params=pltpu.CompilerParams(dimension_semantics=("arbitrary", "arbitrary"),
                                             vmem_limit_bytes=VMEM_LIMIT),
        name="mixer_prompt",
    )(x, *weights)


def _mixer_sample(x, s0, wts):
    bsz, seq_len, _ = x.shape
    n = SAMPLE_SEQS * seq_len
    x2 = x.reshape(bsz * seq_len, D_MODEL)
    weights = [wts[k] for k in ("win", "wkt", "walrt", "waup", "waupt", "ba", "bac", "glag", "lng", "lnb",
                                "ws_sample", "bsf_sample", "gmg", "wout", "ln1g", "ln1b")]
    state_spec = pl.BlockSpec((SAMPLE_SEQS, GLA_HEADS, GLA_DK, GLA_DV), lambda i: (i, 0, 0, 0))
    h, s_new, vg = pl.pallas_call(
        functools.partial(_mixer_sample_kernel, seq_len),
        grid=(bsz // SAMPLE_SEQS,),
        in_specs=[pl.BlockSpec((n, D_MODEL), lambda i: (i, 0)), state_spec] + [_full(w.shape) for w in weights],
        out_specs=[pl.BlockSpec((n, D_MODEL), lambda i: (i, 0)), state_spec,
                   pl.BlockSpec((n, GMLP_WIDTH), lambda i: (i, 0))],
        out_shape=[jax.ShapeDtypeStruct((bsz * seq_len, D_MODEL), F32),
                   jax.ShapeDtypeStruct(s0.shape, F32),
                   jax.ShapeDtypeStruct((bsz * seq_len, GMLP_WIDTH), F32)],
        scratch_shapes=[pltpu.VMEM((n, D_MODEL), BF16)],
        compiler_params=pltpu.CompilerParams(dimension_semantics=("arbitrary",), vmem_limit_bytes=VMEM_LIMIT),
        name="mixer_sample",
    )(x2, s0, *weights)
    return h, s_new, vg.reshape(bsz, seq_len, GMLP_WIDTH)


def _dispatch(h_p, h_s, wts):
    npb, nsb = h_p.shape[0] // BLOCK, h_s.shape[0] // BLOCK
    nb_real = npb + nsb
    nb = nb_real + DUMP_BLOCKS
    tok_spec = pl.BlockSpec((1, N_EXPERTS, BLOCK), lambda i: (i, 0, 0))
    run_spec = pl.BlockSpec((1, N_EXPERTS, LANES), lambda i: (i, 0, 0))
    return pl.pallas_call(
        functools.partial(_dispatch_kernel, npb, nb_real),
        grid=(nb,),
        in_specs=[pl.BlockSpec((BLOCK, D_MODEL), lambda i: (jnp.minimum(i, npb - 1), 0)),
                  pl.BlockSpec((BLOCK, D_MODEL), lambda i: (jnp.clip(i - npb, 0, nsb - 1), 0)),
                  _full(wts["wrt"].shape), _full(wts["rbias"].shape)],
        out_specs=[pl.BlockSpec((BLOCK_ROWS, D_MODEL), lambda i: (i, 0)), tok_spec, tok_spec, run_spec, run_spec],
        out_shape=[jax.ShapeDtypeStruct((nb * BLOCK_ROWS, D_MODEL), BF16),
                   jax.ShapeDtypeStruct((nb, N_EXPERTS, BLOCK), F32),
                   jax.ShapeDtypeStruct((nb, N_EXPERTS, BLOCK), F32),
                   jax.ShapeDtypeStruct((nb, N_EXPERTS, LANES), F32),
                   jax.ShapeDtypeStruct((nb, N_EXPERTS, LANES), F32)],
        compiler_params=pltpu.CompilerParams(dimension_semantics=("arbitrary",), vmem_limit_bytes=VMEM_LIMIT),
        name="moe_dispatch",
    )(h_p, h_s, wts["wrt"], wts["rbias"])


def _max_tiles(nb):
    return (nb * (BLOCK_ROWS_USED // SEG) + N_EXPERTS * (TILE_CHUNKS - 1)) // TILE_CHUNKS + 1


def _max_expert_chunks(nb):
    most = nb * (BLOCK // SEG) + TILE_CHUNKS - 1
    return -(-most // LANES) * LANES


def _plan(seg_lanes, nb):
    nt_max = _max_tiles(nb)
    max_chunks = _max_expert_chunks(nb)
    seg = seg_lanes[:nb, :, 0].astype(jnp.int32)
    start = jnp.cumsum(seg, axis=1) - seg
    nch = seg // SEG
    ends = jnp.cumsum(nch, axis=0)
    n_chunks = ends[-1]
    f = jnp.arange(nb, dtype=jnp.int32)[:, None] * (BLOCK_ROWS // SEG) + start // SEG - (ends - nch)
    q = jnp.arange(max_chunks, dtype=jnp.int32)
    passed = (ends[None, :-1, :] <= q[:, None, None]).astype(jnp.int32)
    chunk = q[:, None] + f[0][None, :] + jnp.sum(passed * (f[1:] - f[:-1])[None], axis=1)
    table = jnp.where(q[:, None] < n_chunks[None, :], chunk, ZERO_CHUNK).T
    n_tiles_e = -(-n_chunks // TILE_CHUNKS)
    tile_ends = jnp.cumsum(n_tiles_e)
    t = jnp.arange(nt_max, dtype=jnp.int32)
    done = (tile_ends[None, :] <= t[:, None]).astype(jnp.int32)
    tile_expert = jnp.minimum(jnp.sum(done, axis=1), N_EXPERTS - 1)
    tile_in_expert = t - jnp.sum(done * n_tiles_e[None, :], axis=1)
    return (table.astype(jnp.int32).reshape(N_EXPERTS, 1, max_chunks), tile_expert.astype(jnp.int32),
            tile_in_expert.astype(jnp.int32), n_chunks.astype(jnp.int32), tile_ends[-1:].astype(jnp.int32))


def _ffn(x_rows, nb, table, tile_expert, tile_in_expert, n_chunks, n_tiles, w_gate, w_up, w_down):
    nt_max = tile_expert.shape[0]
    max_chunks = table.shape[-1]
    tab_spec = lambda ahead: pl.BlockSpec(
        (1, 1, max_chunks), lambda i, te, *_: (te[jnp.minimum(i + ahead, nt_max - 1)], 0, 0),
        memory_space=pltpu.SMEM)
    w_spec = lambda shape: pl.BlockSpec((1,) + shape, lambda i, te, *_: (te[i], 0, 0))
    assert x_rows.shape[0] - nb * BLOCK_ROWS >= 2 * TILE_ROWS
    x_chunks = x_rows.reshape(x_rows.shape[0] // SEG, SEG, D_MODEL)
    return pl.pallas_call(
        functools.partial(_ffn_kernel, nb * BLOCK_ROWS // SEG, nt_max),
        grid_spec=pltpu.PrefetchScalarGridSpec(
            num_scalar_prefetch=4,
            grid=(nt_max,),
            in_specs=[tab_spec(0), tab_spec(1), tab_spec(2), pl.BlockSpec(memory_space=pl.ANY),
                      w_spec((D_MODEL, D_EXPERT)), w_spec((D_MODEL, D_EXPERT)), w_spec((D_EXPERT, D_MODEL))],
            out_specs=pl.BlockSpec(memory_space=pl.ANY),
            scratch_shapes=[pltpu.VMEM((IN_BUFS, TILE_CHUNKS, SEG, D_MODEL), BF16),
                            pltpu.VMEM((2, TILE_CHUNKS, SEG, D_MODEL), BF16),
                            pltpu.VMEM((D_MODEL, D_EXPERT), BF16),
                            pltpu.VMEM((D_MODEL, D_EXPERT), BF16),
                            pltpu.VMEM((D_EXPERT, D_MODEL), BF16),
                            pltpu.SemaphoreType.DMA((IN_BUFS,)),
                            pltpu.SemaphoreType.DMA((2,))]),
        out_shape=jax.ShapeDtypeStruct(x_chunks.shape, BF16),
        input_output_aliases={7: 0},
        compiler_params=pltpu.CompilerParams(dimension_semantics=("arbitrary",), vmem_limit_bytes=VMEM_LIMIT),
        name="moe_ffn",
    )(tile_expert, tile_in_expert, n_chunks, n_tiles, table, table, table, x_chunks, w_gate, w_up, w_down
      ).reshape(x_rows.shape)


def _combine(h, o_rows, routing, block_off, wts):
    nblk = h.shape[0] // BLOCK
    weights = [wts[k] for k in ("wsg", "wsu", "wsd", "ln2g", "ln2b")]
    tok_spec = pl.BlockSpec((1, N_EXPERTS, BLOCK), lambda i: (i + block_off, 0, 0))
    run_spec = pl.BlockSpec((1, N_EXPERTS, LANES), lambda i: (i + block_off, 0, 0))
    return pl.pallas_call(
        _combine_kernel,
        grid=(nblk,),
        in_specs=[pl.BlockSpec((BLOCK, D_MODEL), lambda i: (i, 0)),
                  pl.BlockSpec((BLOCK_ROWS, D_MODEL), lambda i: (i + block_off, 0)),
                  tok_spec, tok_spec, run_spec, run_spec] + [_full(w.shape) for w in weights],
        out_specs=pl.BlockSpec((BLOCK, D_MODEL), lambda i: (i, 0)),
        out_shape=jax.ShapeDtypeStruct(h.shape, F32),
        compiler_params=pltpu.CompilerParams(dimension_semantics=("arbitrary",), vmem_limit_bytes=VMEM_LIMIT),
        name="moe_combine",
    )(h, o_rows, *routing, *weights)


def _moe(h_p, h_s, w_gate, w_up, w_down, wts):
    nb = (h_p.shape[0] + h_s.shape[0]) // BLOCK
    x_rows, rank, comb, seg_lanes, start_lanes = _dispatch(h_p, h_s, wts)
    o_rows = _ffn(x_rows, nb, *_plan(seg_lanes, nb), w_gate, w_up, w_down)
    routing = (rank, comb, seg_lanes, start_lanes)
    y_p = _combine(h_p, o_rows, routing, 0, wts)
    y_s = _combine(h_s, o_rows, routing, h_p.shape[0] // BLOCK, wts)
    return y_p, y_s


def _prep_weights(seq_len, w_in, w_a_up, b_a, gla_norm_g, gmlp_ln_g, gmlp_ln_b, w_s, b_s, gmlp_norm_g, w_out,
                  ln1_g, ln1_b, w_router, router_bias, ws_gate, ws_up, ws_down, ln2_g, ln2_b):
    o1 = QK_WIDTH
    o2 = o1 + QK_WIDTH
    o3 = o2 + GLA_WIDTH
    o4 = o3 + GLA_WIDTH
    o5 = o4 + GLA_RANK
    o6 = o5 + GMLP_WIDTH
    wq, wk, wva, wg_, walr, wu_, wvb = jnp.split(w_in, [o1, o2, o3, o4, o5, o6], axis=-1)
    walr_p = jnp.pad(walr, ((0, 0), (0, LANES - GLA_RANK)))
    waup_p = jnp.pad(w_a_up, ((0, LANES - GLA_RANK), (0, 0)))
    row = lambda a: a.reshape(1, -1)
    reps = (SAMPLE_SEQS * seq_len) // seq_len
    ws_small = w_s[:, :seq_len, :seq_len]
    return {
        "win": jnp.concatenate([wq, wk, wva, wg_, wu_, wvb, walr_p], axis=-1).astype(BF16),
        "wkt": wk.T.astype(BF16),
        "walrt": walr_p.T.astype(BF16),
        "waup": waup_p.astype(BF16),
        "waupt": waup_p.T.astype(BF16),
        "ba": row(b_a), "bac": b_a.reshape(-1, 1),
        "glag": gla_norm_g, "lng": row(gmlp_ln_g), "lnb": row(gmlp_ln_b),
        "ws": w_s,
        "ws_sample": jnp.tile(ws_small, (1, reps, reps)),
        "bsf_prompt": jnp.repeat(b_s[:, :GMLP_CHUNK].T, GMLP_DH, axis=1),
        "bsf_sample": jnp.tile(jnp.repeat(b_s[:, :seq_len].T, GMLP_DH, axis=1), (reps, 1)),
        "gmg": gmlp_norm_g,
        "wout": w_out.astype(BF16),
        "ln1g": row(ln1_g), "ln1b": row(ln1_b),
        "wrt": w_router.T.astype(BF16), "rbias": router_bias.reshape(-1, 1),
        "wsg": ws_gate.astype(BF16), "wsu": ws_up.astype(BF16), "wsd": ws_down.astype(BF16),
        "ln2g": row(ln2_g), "ln2b": row(ln2_b),
    }


def kernel(x_prompt, x_sample, state_gla, w_in, w_a_up, b_a, gla_norm_g, gmlp_ln_g, gmlp_ln_b, w_s, b_s,
           gmlp_norm_g, w_out, ln1_g, ln1_b, w_router, router_bias, w_gate, w_up, w_down, ws_gate, ws_up,
           ws_down, ln2_g, ln2_b):
    assert x_prompt.shape[1] % PROMPT_TILE == 0 and x_sample.shape[0] % SAMPLE_SEQS == 0
    assert x_sample.shape[1] <= GMLP_CHUNK and w_in.shape[0] == DEPTH
    assert (x_sample.shape[0] * x_sample.shape[1]) % BLOCK == 0 and PROMPT_TILE % BLOCK == 0
    bsz, seq, _ = x_prompt.shape
    dbsz, dseq, _ = x_sample.shape
    hp, hs = x_prompt, x_sample
    gla_p, gla_s, v_s = [], [], []
    for l in range(DEPTH):
        wts = _prep_weights(dseq, w_in[l], w_a_up[l], b_a[l], gla_norm_g[l], gmlp_ln_g[l], gmlp_ln_b[l], w_s[l],
                            b_s[l], gmlp_norm_g[l], w_out[l], ln1_g[l], ln1_b[l], w_router[l], router_bias[l],
                            ws_gate[l], ws_up[l], ws_down[l], ln2_g[l], ln2_b[l])
        h_p, sp = _mixer_prompt(hp, wts)
        h_s, ss, vrows = _mixer_sample(hs, state_gla[l], wts)
        y_p, y_s = _moe(h_p.reshape(bsz * seq, D_MODEL), h_s, w_gate[l], w_up[l], w_down[l], wts)
        hp = y_p.reshape(bsz, seq, D_MODEL)
        hs = y_s.reshape(dbsz, dseq, D_MODEL)
        gla_p.append(sp)
        gla_s.append(ss)
        v_s.append(vrows)
    return (hp, hs, jnp.stack(gla_p), jnp.stack(gla_s), jnp.stack(v_s))
```

```python
import functools
import math

import jax
import jax.numpy as jnp
from jax import lax
from jax.experimental import pallas as pl
from jax.experimental.pallas import tpu as pltpu

F32 = jnp.float32
BF16 = jnp.bfloat16

D_MODEL = 1024
DEPTH = 1
GLA_WIDTH = 512
GLA_HEADS = 4
GLA_DK = 64
GLA_DV = 128
GLA_RANK = 16
GLA_TAU = 16.0
GMLP_WIDTH = 512
GMLP_HEADS = 4
GMLP_DH = 128
GMLP_CHUNK = 128
QK_WIDTH = GLA_HEADS * GLA_DK
N_EXPERTS = 64
TOP_K = 8
N_GROUPS = 8
GROUP_SIZE = N_EXPERTS // N_GROUPS
TOPK_GROUPS = 4
D_EXPERT = 256
D_SHARED = 256
ROUTE_SCALE = 2.5
ALPHA = (2.0 * DEPTH) ** 0.25

LANES = 128

C_Q = 0
C_K = C_Q + QK_WIDTH
C_VA = C_K + QK_WIDTH
C_G = C_VA + GLA_WIDTH
C_U = C_G + GLA_WIDTH
C_VB = C_U + GMLP_WIDTH
C_ALR = C_VB + GMLP_WIDTH
N_PROJ = C_ALR + LANES

CHUNK = 128
PROMPT_TILE = 256
SAMPLE_SEQS = 32
VMEM_LIMIT = 56 * 1024 * 1024

BLOCK = 256
SEG = 16
TILE_CHUNKS = 32
SUB_CHUNKS = 16
TILE_ROWS = TILE_CHUNKS * SEG
GATHER_ROWS = 512
BLOCK_ROWS_USED = BLOCK * TOP_K + N_EXPERTS * (SEG - 1)
BLOCK_ROWS = -(-(BLOCK_ROWS_USED + SEG) // GATHER_ROWS) * GATHER_ROWS
DUMP_BLOCKS = -(-(2 * TILE_ROWS) // BLOCK_ROWS)
ZERO_CHUNK = BLOCK_ROWS_USED // SEG
IN_BUFS = 3


def _dot(a, b):
    return jnp.dot(a, b, preferred_element_type=F32)


def _dot_nt(a, b):
    return lax.dot_general(a, b, (((1,), (1,)), ((), ())), preferred_element_type=F32)


def _dot_tn(a, b):
    return lax.dot_general(a, b, (((0,), (0,)), ((), ())), preferred_element_type=F32)


def _shr(x, d):
    assert d & (d - 1) == 0
    return lax.shift_right_logical(x, d.bit_length() - 1)


def _split_dot(m01, x):
    hi = x.astype(BF16)
    lo = (x - hi.astype(F32)).astype(BF16)
    return _dot(m01, hi) + _dot(m01, lo)


def _split_dot_r(x, m01):
    hi = x.astype(BF16)
    lo = (x - hi.astype(F32)).astype(BF16)
    return _dot(hi, m01) + _dot(lo, m01)


def _sigmoid(x):
    return 1.0 / (1.0 + jnp.exp(-x))


def _silu(x):
    return x * _sigmoid(x)


def _gelu(x):
    c = math.sqrt(2.0 / math.pi)
    return x * (0.5 * (1.0 + jnp.tanh(c * (x + 0.044715 * (x * x * x)))))


def _log_sigmoid(x):
    return -(jnp.maximum(-x, 0.0) + jnp.log1p(jnp.exp(-jnp.abs(x))))


def _layernorm(x, g, b, eps=1e-5):
    mu = jnp.mean(x, axis=-1, keepdims=True)
    xc = x - mu
    var = jnp.mean(xc * xc, axis=-1, keepdims=True)
    return xc * lax.rsqrt(var + eps) * g + b


def _rmsnorm(x, g, eps=1e-6):
    return x * lax.rsqrt(jnp.mean(x * x, axis=-1, keepdims=True) + eps) * g


def _gmlp_heads(u, vg, w_mix, bias, gmg_ref, merged_ref, rows):
    vgb = vg.astype(BF16)
    for h in range(GMLP_HEADS):
        cs = slice(h * GMLP_DH, (h + 1) * GMLP_DH)
        sgu = _dot(w_mix(h), vgb[:, cs]) + bias[:, cs]
        y = _rmsnorm(u[:, cs] * sgu, gmg_ref[h:h + 1, :])
        merged_ref[rows, GLA_WIDTH + h * GMLP_DH:GLA_WIDTH + (h + 1) * GMLP_DH] = y.astype(BF16)


def _out_proj_ln(x, merged_ref, wout_ref, ln1g_ref, ln1b_ref):
    m = _dot(merged_ref[...], wout_ref[...])
    return _layernorm(ALPHA * x + m, ln1g_ref[...], ln1b_ref[...])


def _mixer_prompt_kernel(x_ref, win_ref, waup_ref, ba_ref, glag_ref, lng_ref, lnb_ref, ws_ref, bsf_ref,
                         gmg_ref, wout_ref, ln1g_ref, ln1b_ref,
                         h_ref, state_ref,
                         z_ref, merged_ref, st_ref):
    t = pl.program_id(1)

    @pl.when(t == 0)
    def _():
        st_ref[...] = jnp.zeros_like(st_ref)

    x = x_ref[0]
    z_ref[...] = _dot(x.astype(BF16), win_ref[...])

    row_i = lax.broadcasted_iota(jnp.int32, (CHUNK, CHUNK), 0)
    col_i = lax.broadcasted_iota(jnp.int32, (CHUNK, CHUNK), 1)
    causal = row_i >= col_i
    tri = jnp.where(causal, 1.0, 0.0).astype(BF16)
    lane = lax.broadcasted_iota(jnp.int32, (1, LANES), 1)
    head_lanes = [lane < GLA_DK, lane >= GLA_DK]
    mid = CHUNK // 2 - 1

    for c in range(PROMPT_TILE // CHUNK):
        rows = slice(c * CHUNK, (c + 1) * CHUNK)
        a_pre = _dot(z_ref[rows, C_ALR:C_ALR + LANES].astype(BF16), waup_ref[...]) + ba_ref[...]
        log_a = _log_sigmoid(a_pre) * (1.0 / GLA_TAU)
        b = _split_dot(tri, log_a)
        b_mid = b[mid:mid + 1, :]
        b_last = b[CHUNK - 1:CHUNK, :]
        q = z_ref[rows, C_Q:C_Q + QK_WIDTH] * (GLA_DK ** -0.5)
        k = z_ref[rows, C_K:C_K + QK_WIDTH]
        q_in = (q * jnp.exp(b - b_mid)).astype(BF16)
        k_in = (k * jnp.exp(b_mid - b)).astype(BF16)
        q_st = (q * jnp.exp(b)).astype(BF16)
        k_st = (k * jnp.exp(b_last - b)).astype(BF16)
        d_last = jnp.exp(b_last)
        va = z_ref[rows, C_VA:C_VA + GLA_WIDTH].astype(BF16)
        for h in range(GLA_HEADS):
            ps = slice((h // 2) * LANES, (h // 2 + 1) * LANES)
            vs = slice(h * GLA_DV, (h + 1) * GLA_DV)
            hm = head_lanes[h % 2]
            zero = jnp.zeros((), BF16)
            a = _dot_nt(jnp.where(hm, q_in[:, ps], zero), k_in[:, ps])
            a = jnp.where(causal, a, 0.0).astype(BF16)
            st = st_ref[h]
            o = _dot(a, va[:, vs]) + _dot_nt(jnp.where(hm, q_st[:, ps], zero), st.astype(BF16))
            upd = _dot_tn(va[:, vs], jnp.where(hm, k_st[:, ps], zero))
            st_ref[h] = st * d_last[:, ps] + upd
            gate = z_ref[rows, C_G + h * GLA_DV:C_G + (h + 1) * GLA_DV]
            o = _rmsnorm(o, glag_ref[h:h + 1, :]) * _silu(gate)
            merged_ref[rows, vs] = o.astype(BF16)
        u = _gelu(z_ref[rows, C_U:C_U + GMLP_WIDTH])
        vg = _layernorm(_gelu(z_ref[rows, C_VB:C_VB + GMLP_WIDTH]), lng_ref[...], lnb_ref[...])
        _gmlp_heads(u, vg, lambda h: jnp.where(causal, ws_ref[h], 0.0).astype(BF16), bsf_ref[...],
                    gmg_ref, merged_ref, rows)

    h_ref[0] = _out_proj_ln(x, merged_ref, wout_ref, ln1g_ref, ln1b_ref)

    @pl.when(t == pl.num_programs(1) - 1)
    def _():
        for h in range(GLA_HEADS):
            lo = (h % 2) * GLA_DK
            state_ref[0, h] = st_ref[h].T[lo:lo + GLA_DK, :]


def _mixer_sample_kernel(seq_len, x_ref, s0_ref, win_ref, wkt_ref, walrt_ref, waup_ref, waupt_ref, ba_ref, bac_ref,
                         glag_ref, lng_ref, lnb_ref, wst_ref, bsf_ref, gmg_ref, wout_ref, ln1g_ref, ln1b_ref,
                         h_ref, snew_ref, vg_ref,
                         merged_ref):
    n = SAMPLE_SEQS * seq_len
    x = x_ref[...]
    xb = x.astype(BF16)
    z = _dot(xb, win_ref[...])

    ti = lax.broadcasted_iota(jnp.int32, (n, n), 0)
    tj = lax.broadcasted_iota(jnp.int32, (n, n), 1)
    same = _shr(ti, seq_len) == _shr(tj, seq_len)
    causal = jnp.logical_and(same, ti >= tj)
    tri = jnp.where(causal, 1.0, 0.0).astype(BF16)
    tri_t = jnp.where(jnp.logical_and(same, ti <= tj), 1.0, 0.0).astype(BF16)
    same01 = jnp.where(same, 1.0, 0.0).astype(BF16)

    a_pre = _dot(z[:, C_ALR:C_ALR + LANES].astype(BF16), waup_ref[...]) + ba_ref[...]
    log_a = _log_sigmoid(a_pre) * (1.0 / GLA_TAU)
    b = _split_dot(tri, log_a)
    q = z[:, C_Q:C_Q + QK_WIDTH] * (GLA_DK ** -0.5)
    k = z[:, C_K:C_K + QK_WIDTH]
    q_in = (q * jnp.exp(b)).astype(BF16)
    k_in = (k * jnp.exp(-b)).astype(BF16)
    va = z[:, C_VA:C_VA + GLA_WIDTH].astype(BF16)

    k_t = _dot_nt(wkt_ref[...], xb)
    alr_t = _dot_nt(walrt_ref[...], xb)
    a_pre_t = _dot(waupt_ref[...], alr_t.astype(BF16)) + bac_ref[...]
    log_a_t = _log_sigmoid(a_pre_t) * (1.0 / GLA_TAU)
    b_t = _split_dot_r(log_a_t, tri_t)
    tot_t = _split_dot_r(log_a_t, same01)
    k_st_t = k_t * jnp.exp(tot_t - b_t)
    d_t = jnp.exp(tot_t)

    lane = lax.broadcasted_iota(jnp.int32, (1, LANES), 1)
    head_lanes = [lane < GLA_DK, lane >= GLA_DK]
    nrow = SAMPLE_SEQS * GLA_DK
    r_seq = _shr(lax.broadcasted_iota(jnp.int32, (nrow, n), 0), GLA_DK)
    c_seq = _shr(lax.broadcasted_iota(jnp.int32, (nrow, n), 1), seq_len)
    c_first = (lax.broadcasted_iota(jnp.int32, (nrow, n), 1) & (seq_len - 1)) == 0
    blk = r_seq == c_seq
    blk_first = jnp.logical_and(blk, c_first)
    q_seq = _shr(lax.broadcasted_iota(jnp.int32, (n, nrow), 0), seq_len)
    q_col = _shr(lax.broadcasted_iota(jnp.int32, (n, nrow), 1), GLA_DK)
    blk_q = q_seq == q_col

    for h in range(GLA_HEADS):
        ps = slice((h // 2) * LANES, (h // 2 + 1) * LANES)
        ds_ = slice(h * GLA_DK, (h + 1) * GLA_DK)
        vs = slice(h * GLA_DV, (h + 1) * GLA_DV)
        hm = head_lanes[h % 2]
        zero = jnp.zeros((), BF16)
        a = _dot_nt(jnp.where(hm, q_in[:, ps], zero), k_in[:, ps])
        a = jnp.where(causal, a, 0.0).astype(BF16)
        s0 = s0_ref[:, h].reshape(nrow, GLA_DV)
        q_h = q_in[:, ds_]
        q_bd = jnp.where(blk_q, jnp.concatenate([q_h] * SAMPLE_SEQS, axis=1), zero)
        o = _dot(a, va[:, vs]) + _dot(q_bd, s0.astype(BF16))
        k_bd = jnp.where(blk, jnp.concatenate([k_st_t[ds_, :]] * SAMPLE_SEQS, axis=0), 0.0).astype(BF16)
        upd = _dot(k_bd, va[:, vs])
        d_bd = jnp.where(blk_first, jnp.concatenate([d_t[ds_, :]] * SAMPLE_SEQS, axis=0), 0.0)
        d_col = jnp.sum(d_bd, axis=1, keepdims=True)
        snew_ref[:, h] = (s0 * d_col + upd).reshape(SAMPLE_SEQS, GLA_DK, GLA_DV)
        gate = z[:, C_G + h * GLA_DV:C_G + (h + 1) * GLA_DV]
        o = _rmsnorm(o, glag_ref[h:h + 1, :]) * _silu(gate)
        merged_ref[:, vs] = o.astype(BF16)

    u = _gelu(z[:, C_U:C_U + GMLP_WIDTH])
    vg = _layernorm(_gelu(z[:, C_VB:C_VB + GMLP_WIDTH]), lng_ref[...], lnb_ref[...])
    vg_ref[...] = vg
    _gmlp_heads(u, vg, lambda h: jnp.where(causal, wst_ref[h], 0.0).astype(BF16), bsf_ref[...],
                gmg_ref, merged_ref, slice(None))
    h_ref[...] = _out_proj_ln(x, merged_ref, wout_ref, ln1g_ref, ln1b_ref)


def _route(hb, wrt_ref, rbias_ref):
    tm = hb.shape[0]
    s = _sigmoid(_dot_nt(wrt_ref[...], hb))
    sb = s + rbias_ref[...]
    neg = jnp.float32(-jnp.inf)
    sub = lax.broadcasted_iota(jnp.int32, (GROUP_SIZE, tm), 0)
    gscore = []
    for g in range(N_GROUPS):
        blk = sb[g * GROUP_SIZE:(g + 1) * GROUP_SIZE, :]
        m1 = jnp.max(blk, axis=0, keepdims=True)
        i1 = jnp.min(jnp.where(blk == m1, sub, GROUP_SIZE), axis=0, keepdims=True)
        m2 = jnp.max(jnp.where(sub == i1, neg, blk), axis=0, keepdims=True)
        gscore.append(m1 + m2)
    gsel = []
    for g in range(N_GROUPS):
        rank = jnp.zeros((1, tm), jnp.int32)
        for o in range(N_GROUPS):
            if o == g:
                continue
            ahead = (gscore[o] >= gscore[g]) if o < g else (gscore[o] > gscore[g])
            rank = rank + ahead.astype(jnp.int32)
        gsel.append(rank < TOPK_GROUPS)
    v = jnp.concatenate(
        [jnp.where(gsel[g], sb[g * GROUP_SIZE:(g + 1) * GROUP_SIZE, :], neg) for g in range(N_GROUPS)], axis=0)
    eidx = lax.broadcasted_iota(jnp.int32, (N_EXPERTS, tm), 0)
    sel = jnp.zeros((N_EXPERTS, tm), F32)
    for _ in range(TOP_K):
        m = jnp.max(v, axis=0, keepdims=True)
        first = jnp.min(jnp.where(v == m, eidx, N_EXPERTS), axis=0, keepdims=True)
        hit = eidx == first
        sel = jnp.where(hit, 1.0, sel)
        v = jnp.where(hit, neg, v)
    wsel = s * sel
    return wsel / jnp.sum(wsel, axis=0, keepdims=True) * ROUTE_SCALE, sel


def _per_row(chunk_vals):
    n, t = chunk_vals.shape
    return jnp.broadcast_to(chunk_vals[:, None, :], (n, SEG, t)).reshape(n * SEG, t)


def _row_lookup(seg_b, start_b, rank, tables):
    n_chunks = BLOCK_ROWS // SEG
    ei = lax.broadcasted_iota(jnp.int32, (N_EXPERTS, LANES), 0)
    ej = lax.broadcasted_iota(jnp.int32, (N_EXPERTS, LANES), 1)
    seg16 = (seg_b * (1.0 / SEG)).astype(BF16)
    start_row = _dot_tn(seg16, jnp.where(ei < ej, 1.0, 0.0).astype(BF16))[0:1, :]
    seg_row = _dot_tn(seg16, jnp.where(ei == ej, 1.0, 0.0).astype(BF16))[0:1, :]
    c = lax.broadcasted_iota(jnp.int32, (n_chunks, LANES), 0).astype(F32)
    owner = jnp.where(c >= start_row, jnp.where(c < start_row + seg_row, 1.0, 0.0), 0.0).astype(BF16)

    def lookup(tab):
        padded = jnp.concatenate([tab.astype(BF16), jnp.zeros((LANES - N_EXPERTS, tab.shape[1]), BF16)], axis=0)
        return _dot(owner, padded)

    first = SEG * (c - lookup(start_b * (1.0 / SEG)))[:, 0:1]
    rel = lookup(rank) - first
    sub = lax.broadcasted_iota(jnp.int32, (BLOCK_ROWS, rank.shape[1]), 0) & (SEG - 1)
    hits = _per_row(rel) == sub.astype(F32)
    return hits, [_per_row(lookup(tab)) for tab in tables]


def _dispatch_kernel(n_prompt_blocks, n_blocks, hp_ref, hs_ref, wrt_ref, rbias_ref, x_ref, rank_ref, comb_ref,
                     seg_ref, start_ref):
    i = pl.program_id(0)

    @pl.when(i < n_blocks)
    def _():
        _dispatch_block(i < n_prompt_blocks, hp_ref, hs_ref, wrt_ref, rbias_ref, x_ref, rank_ref, comb_ref, seg_ref,
                        start_ref)

    @pl.when(i >= n_blocks)
    def _():
        for ref in (x_ref, rank_ref, comb_ref, seg_ref, start_ref):
            ref[...] = jnp.zeros_like(ref)


def _dispatch_block(is_prompt, hp_ref, hs_ref, wrt_ref, rbias_ref, x_ref, rank_ref, comb_ref, seg_ref, start_ref):
    tm = BLOCK
    hb = jnp.where(is_prompt, hp_ref[...], hs_ref[...]).astype(BF16)
    comb, sel = _route(hb, wrt_ref, rbias_ref)
    ti = lax.broadcasted_iota(jnp.int32, (tm, tm), 0)
    tj = lax.broadcasted_iota(jnp.int32, (tm, tm), 1)
    before = jnp.where(ti < tj, 1.0, 0.0).astype(BF16)
    rank = _dot(sel.astype(BF16), before)
    rank = jnp.where(sel > 0.0, rank, -1.0)
    cnt = jnp.sum(sel, axis=1, keepdims=True)
    seg = jnp.floor((cnt + (SEG - 1.0)) * (1.0 / SEG)) * SEG
    ei = lax.broadcasted_iota(jnp.int32, (N_EXPERTS, N_EXPERTS), 0)
    ej = lax.broadcasted_iota(jnp.int32, (N_EXPERTS, N_EXPERTS), 1)
    below = jnp.where(ej < ei, 1.0, 0.0).astype(BF16)
    seg_b = jnp.broadcast_to(seg, (N_EXPERTS, LANES))
    start_b = _dot(below, seg_b.astype(BF16))
    hits, _ = _row_lookup(seg_b, start_b, rank, [])
    gather = jnp.where(hits, 1.0, 0.0).astype(BF16)
    for c in range(BLOCK_ROWS // GATHER_ROWS):
        rs = slice(c * GATHER_ROWS, (c + 1) * GATHER_ROWS)
        x_ref[rs, :] = _dot(gather[rs, :], hb).astype(BF16)
    rank_ref[0] = rank
    comb_ref[0] = comb
    seg_ref[0] = seg_b
    start_ref[0] = start_b


def _ffn_kernel(dump_base, nt_max, tile_expert_ref, tile_in_expert_ref, n_chunks_ref, n_tiles_ref,
                tab0_ref, tab1_ref, tab2_ref, x_hbm, wg_ref, wu_ref, wd_ref,
                o_hbm, ibuf, obuf, wgb, wub, wdb, in_sem, out_sem):
    i = pl.program_id(0)
    n_tiles = n_tiles_ref[0]
    tabs = (tab0_ref, tab1_ref, tab2_ref)

    def start_in(ahead, slot):
        exists = i + ahead < n_tiles
        first = jnp.where(exists, tile_in_expert_ref[jnp.minimum(i + ahead, nt_max - 1)] * TILE_CHUNKS, 0)
        srcs = [jnp.where(exists, tabs[ahead][0, 0, first + j], ZERO_CHUNK) for j in range(TILE_CHUNKS)]
        for j in range(TILE_CHUNKS):
            pltpu.make_async_copy(x_hbm.at[srcs[j]], ibuf.at[slot, j], in_sem.at[slot]).start(priority=j % 2)

    def start_out(slot):
        e = tile_expert_ref[i]
        first = tile_in_expert_ref[i] * TILE_CHUNKS
        n_real = n_chunks_ref[e] - first
        dsts = [jnp.where(j < n_real, tab0_ref[0, 0, first + j], dump_base + slot * TILE_CHUNKS + j)
                for j in range(TILE_CHUNKS)]
        for j in range(TILE_CHUNKS):
            pltpu.make_async_copy(obuf.at[slot, j], o_hbm.at[dsts[j]], out_sem.at[slot]).start(priority=j % 2)

    def in_wait(slot):
        for j in range(TILE_CHUNKS):
            pltpu.make_async_copy(x_hbm.at[0], ibuf.at[slot, j], in_sem.at[slot]).wait()

    def out_wait(slot):
        for j in range(TILE_CHUNKS):
            pltpu.make_async_copy(obuf.at[slot, j], o_hbm.at[0], out_sem.at[slot]).wait()

    @pl.when(i < n_tiles)
    def _():
        islot = lax.rem(i, IN_BUFS)
        nslot = jnp.where(islot == 0, IN_BUFS - 1, islot - 1)
        oslot = lax.rem(i, 2)

        @pl.when(i == 0)
        def _():
            start_in(0, 0)
            start_in(1, 1)

        @pl.when(i >= 2)
        def _():
            out_wait(oslot)

        in_wait(islot)

        @pl.when(tile_in_expert_ref[i] == 0)
        def _():
            wgb[...] = wg_ref[0].astype(BF16)
            wub[...] = wu_ref[0].astype(BF16)
            wdb[...] = wd_ref[0].astype(BF16)

        start_in(2, nslot)
        for s in range(TILE_CHUNKS // SUB_CHUNKS):
            cs = slice(s * SUB_CHUNKS, (s + 1) * SUB_CHUNKS)
            x = ibuf[islot, cs].reshape(SUB_CHUNKS * SEG, D_MODEL)
            hid = _silu(_dot(x, wgb[...])) * _dot(x, wub[...])
            obuf[oslot, cs] = _dot(hid.astype(BF16), wdb[...]).astype(BF16).reshape(SUB_CHUNKS, SEG, D_MODEL)
        start_out(oslot)

        @pl.when(i == n_tiles - 1)
        def _():
            in_wait(lax.rem(i + 1, IN_BUFS))
            in_wait(nslot)
            out_wait(oslot)

            @pl.when(i >= 1)
            def _():
                out_wait(1 - oslot)


def _combine_kernel(h_ref, o_ref, rank_ref, comb_ref, seg_ref, start_ref, wsg_ref, wsu_ref, wsd_ref, ln2g_ref,
                    ln2b_ref, y_ref):
    h = h_ref[...]
    hb = h.astype(BF16)
    hits, (comb_rows,) = _row_lookup(seg_ref[0], start_ref[0], rank_ref[0], [comb_ref[0]])
    scatter = jnp.where(hits, comb_rows, 0.0).astype(BF16)
    routed = _dot_tn(scatter, o_ref[...])
    shared = _dot((_silu(_dot(hb, wsg_ref[...])) * _dot(hb, wsu_ref[...])).astype(BF16), wsd_ref[...])
    y_ref[...] = _layernorm(ALPHA * h + (routed + shared), ln2g_ref[...], ln2b_ref[...])


def _full(shape):
    return pl.BlockSpec(shape, lambda *_: (0,) * len(shape))


def _mixer_prompt(x, wts):
    bsz, seq, _ = x.shape
    n_t = seq // PROMPT_TILE
    weights = [wts[k] for k in ("win", "waup", "ba", "glag", "lng", "lnb", "ws", "bsf_prompt", "gmg", "wout",
                                "ln1g", "ln1b")]
    return pl.pallas_call(
        _mixer_prompt_kernel,
        grid=(bsz, n_t),
        in_specs=[pl.BlockSpec((1, PROMPT_TILE, D_MODEL), lambda b, t: (b, t, 0))]
        + [_full(w.shape) for w in weights],
        out_specs=[pl.BlockSpec((1, PROMPT_TILE, D_MODEL), lambda b, t: (b, t, 0)),
                   pl.BlockSpec((1, GLA_HEADS, GLA_DK, GLA_DV), lambda b, t: (b, 0, 0, 0))],
        out_shape=[jax.ShapeDtypeStruct((bsz, seq, D_MODEL), F32),
                   jax.ShapeDtypeStruct((bsz, GLA_HEADS, GLA_DK, GLA_DV), F32)],
        scratch_shapes=[pltpu.VMEM((PROMPT_TILE, N_PROJ), F32),
                        pltpu.VMEM((PROMPT_TILE, D_MODEL), BF16),
                        pltpu.VMEM((GLA_HEADS, GLA_DV, LANES), F32)],
        compiler_params=pltpu.CompilerParams(dimension_semantics=("arbitrary", "arbitrary"),
                                             vmem_limit_bytes=VMEM_LIMIT),
        name="mixer_prompt",
    )(x, *weights)


def _mixer_sample(x, s0, wts):
    bsz, seq_len, _ = x.shape
    n = SAMPLE_SEQS * seq_len
    x2 = x.reshape(bsz * seq_len, D_MODEL)
    weights = [wts[k] for k in ("win", "wkt", "walrt", "waup", "waupt", "ba", "bac", "glag", "lng", "lnb",
                                "ws_sample", "bsf_sample", "gmg", "wout", "ln1g", "ln1b")]
    state_spec = pl.BlockSpec((SAMPLE_SEQS, GLA_HEADS, GLA_DK, GLA_DV), lambda i: (i, 0, 0, 0))
    h, s_new, vg = pl.pallas_call(
        functools.partial(_mixer_sample_kernel, seq_len),
        grid=(bsz // SAMPLE_SEQS,),
        in_specs=[pl.BlockSpec((n, D_MODEL), lambda i: (i, 0)), state_spec] + [_full(w.shape) for w in weights],
        out_specs=[pl.BlockSpec((n, D_MODEL), lambda i: (i, 0)), state_spec,
                   pl.BlockSpec((n, GMLP_WIDTH), lambda i: (i, 0))],
        out_shape=[jax.ShapeDtypeStruct((bsz * seq_len, D_MODEL), F32),
                   jax.ShapeDtypeStruct(s0.shape, F32),
                   jax.ShapeDtypeStruct((bsz * seq_len, GMLP_WIDTH), F32)],
        scratch_shapes=[pltpu.VMEM((n, D_MODEL), BF16)],
        compiler_params=pltpu.CompilerParams(dimension_semantics=("arbitrary",), vmem_limit_bytes=VMEM_LIMIT),
        name="mixer_sample",
    )(x2, s0, *weights)
    return h, s_new, vg.reshape(bsz, seq_len, GMLP_WIDTH)


def _dispatch(h_p, h_s, wts):
    npb, nsb = h_p.shape[0] // BLOCK, h_s.shape[0] // BLOCK
    nb_real = npb + nsb
    nb = nb_real + DUMP_BLOCKS
    tok_spec = pl.BlockSpec((1, N_EXPERTS, BLOCK), lambda i: (i, 0, 0))
    run_spec = pl.BlockSpec((1, N_EXPERTS, LANES), lambda i: (i, 0, 0))
    return pl.pallas_call(
        functools.partial(_dispatch_kernel, npb, nb_real),
        grid=(nb,),
        in_specs=[pl.BlockSpec((BLOCK, D_MODEL), lambda i: (jnp.minimum(i, npb - 1), 0)),
                  pl.BlockSpec((BLOCK, D_MODEL), lambda i: (jnp.clip(i - npb, 0, nsb - 1), 0)),
                  _full(wts["wrt"].shape), _full(wts["rbias"].shape)],
        out_specs=[pl.BlockSpec((BLOCK_ROWS, D_MODEL), lambda i: (i, 0)), tok_spec, tok_spec, run_spec, run_spec],
        out_shape=[jax.ShapeDtypeStruct((nb * BLOCK_ROWS, D_MODEL), BF16),
                   jax.ShapeDtypeStruct((nb, N_EXPERTS, BLOCK), F32),
                   jax.ShapeDtypeStruct((nb, N_EXPERTS, BLOCK), F32),
                   jax.ShapeDtypeStruct((nb, N_EXPERTS, LANES), F32),
                   jax.ShapeDtypeStruct((nb, N_EXPERTS, LANES), F32)],
        compiler_params=pltpu.CompilerParams(dimension_semantics=("arbitrary",), vmem_limit_bytes=VMEM_LIMIT),
        name="moe_dispatch",
    )(h_p, h_s, wts["wrt"], wts["rbias"])


def _max_tiles(nb):
    return (nb * (BLOCK_ROWS_USED // SEG) + N_EXPERTS * (TILE_CHUNKS - 1)) // TILE_CHUNKS + 1


def _max_expert_chunks(nb):
    most = nb * (BLOCK // SEG) + TILE_CHUNKS - 1
    return -(-most // LANES) * LANES


def _plan(seg_lanes, nb):
    nt_max = _max_tiles(nb)
    max_chunks = _max_expert_chunks(nb)
    seg = seg_lanes[:nb, :, 0].astype(jnp.int32)
    start = jnp.cumsum(seg, axis=1) - seg
    nch = seg // SEG
    ends = jnp.cumsum(nch, axis=0)
    n_chunks = ends[-1]
    f = jnp.arange(nb, dtype=jnp.int32)[:, None] * (BLOCK_ROWS // SEG) + start // SEG - (ends - nch)
    q = jnp.arange(max_chunks, dtype=jnp.int32)
    passed = (ends[None, :-1, :] <= q[:, None, None]).astype(jnp.int32)
    chunk = q[:, None] + f[0][None, :] + jnp.sum(passed * (f[1:] - f[:-1])[None], axis=1)
    table = jnp.where(q[:, None] < n_chunks[None, :], chunk, ZERO_CHUNK).T
    n_tiles_e = -(-n_chunks // TILE_CHUNKS)
    tile_ends = jnp.cumsum(n_tiles_e)
    t = jnp.arange(nt_max, dtype=jnp.int32)
    done = (tile_ends[None, :] <= t[:, None]).astype(jnp.int32)
    tile_expert = jnp.minimum(jnp.sum(done, axis=1), N_EXPERTS - 1)
    tile_in_expert = t - jnp.sum(done * n_tiles_e[None, :], axis=1)
    return (table.astype(jnp.int32).reshape(N_EXPERTS, 1, max_chunks), tile_expert.astype(jnp.int32),
            tile_in_expert.astype(jnp.int32), n_chunks.astype(jnp.int32), tile_ends[-1:].astype(jnp.int32))


def _ffn(x_rows, nb, table, tile_expert, tile_in_expert, n_chunks, n_tiles, w_gate, w_up, w_down):
    nt_max = tile_expert.shape[0]
    max_chunks = table.shape[-1]
    tab_spec = lambda ahead: pl.BlockSpec(
        (1, 1, max_chunks), lambda i, te, *_: (te[jnp.minimum(i + ahead, nt_max - 1)], 0, 0),
        memory_space=pltpu.SMEM)
    w_spec = lambda shape: pl.BlockSpec((1,) + shape, lambda i, te, *_: (te[i], 0, 0))
    assert x_rows.shape[0] - nb * BLOCK_ROWS >= 2 * TILE_ROWS
    x_chunks = x_rows.reshape(x_rows.shape[0] // SEG, SEG, D_MODEL)
    return pl.pallas_call(
        functools.partial(_ffn_kernel, nb * BLOCK_ROWS // SEG, nt_max),
        grid_spec=pltpu.PrefetchScalarGridSpec(
            num_scalar_prefetch=4,
            grid=(nt_max,),
            in_specs=[tab_spec(0), tab_spec(1), tab_spec(2), pl.BlockSpec(memory_space=pl.ANY),
                      w_spec((D_MODEL, D_EXPERT)), w_spec((D_MODEL, D_EXPERT)), w_spec((D_EXPERT, D_MODEL))],
            out_specs=pl.BlockSpec(memory_space=pl.ANY),
            scratch_shapes=[pltpu.VMEM((IN_BUFS, TILE_CHUNKS, SEG, D_MODEL), BF16),
                            pltpu.VMEM((2, TILE_CHUNKS, SEG, D_MODEL), BF16),
                            pltpu.VMEM((D_MODEL, D_EXPERT), BF16),
                            pltpu.VMEM((D_MODEL, D_EXPERT), BF16),
                            pltpu.VMEM((D_EXPERT, D_MODEL), BF16),
                            pltpu.SemaphoreType.DMA((IN_BUFS,)),
                            pltpu.SemaphoreType.DMA((2,))]),
        out_shape=jax.ShapeDtypeStruct(x_chunks.shape, BF16),
        input_output_aliases={7: 0},
        compiler_params=pltpu.CompilerParams(dimension_semantics=("arbitrary",), vmem_limit_bytes=VMEM_LIMIT),
        name="moe_ffn",
    )(tile_expert, tile_in_expert, n_chunks, n_tiles, table, table, table, x_chunks, w_gate, w_up, w_down
      ).reshape(x_rows.shape)


def _combine(h, o_rows, routing, block_off, wts):
    nblk = h.shape[0] // BLOCK
    weights = [wts[k] for k in ("wsg", "wsu", "wsd", "ln2g", "ln2b")]
    tok_spec = pl.BlockSpec((1, N_EXPERTS, BLOCK), lambda i: (i + block_off, 0, 0))
    run_spec = pl.BlockSpec((1, N_EXPERTS, LANES), lambda i: (i + block_off, 0, 0))
    return pl.pallas_call(
        _combine_kernel,
        grid=(nblk,),
        in_specs=[pl.BlockSpec((BLOCK, D_MODEL), lambda i: (i, 0)),
                  pl.BlockSpec((BLOCK_ROWS, D_MODEL), lambda i: (i + block_off, 0)),
                  tok_spec, tok_spec, run_spec, run_spec] + [_full(w.shape) for w in weights],
        out_specs=pl.BlockSpec((BLOCK, D_MODEL), lambda i: (i, 0)),
        out_shape=jax.ShapeDtypeStruct(h.shape, F32),
        compiler_params=pltpu.CompilerParams(dimension_semantics=("arbitrary",), vmem_limit_bytes=VMEM_LIMIT),
        name="moe_combine",
    )(h, o_rows, *routing, *weights)


def _moe(h_p, h_s, w_gate, w_up, w_down, wts):
    nb = (h_p.shape[0] + h_s.shape[0]) // BLOCK
    x_rows, rank, comb, seg_lanes, start_lanes = _dispatch(h_p, h_s, wts)
    o_rows = _ffn(x_rows, nb, *_plan(seg_lanes, nb), w_gate, w_up, w_down)
    routing = (rank, comb, seg_lanes, start_lanes)
    y_p = _combine(h_p, o_rows, routing, 0, wts)
    y_s = _combine(h_s, o_rows, routing, h_p.shape[0] // BLOCK, wts)
    return y_p, y_s


def _prep_weights(seq_len, w_in, w_a_up, b_a, gla_norm_g, gmlp_ln_g, gmlp_ln_b, w_s, b_s, gmlp_norm_g, w_out,
                  ln1_g, ln1_b, w_router, router_bias, ws_gate, ws_up, ws_down, ln2_g, ln2_b):
    o1 = QK_WIDTH
    o2 = o1 + QK_WIDTH
    o3 = o2 + GLA_WIDTH
    o4 = o3 + GLA_WIDTH
    o5 = o4 + GLA_RANK
    o6 = o5 + GMLP_WIDTH
    wq, wk, wva, wg_, walr, wu_, wvb = jnp.split(w_in, [o1, o2, o3, o4, o5, o6], axis=-1)
    walr_p = jnp.pad(walr, ((0, 0), (0, LANES - GLA_RANK)))
    waup_p = jnp.pad(w_a_up, ((0, LANES - GLA_RANK), (0, 0)))
    row = lambda a: a.reshape(1, -1)
    reps = (SAMPLE_SEQS * seq_len) // seq_len
    ws_small = w_s[:, :seq_len, :seq_len]
    return {
        "win": jnp.concatenate([wq, wk, wva, wg_, wu_, wvb, walr_p], axis=-1).astype(BF16),
        "wkt": wk.T.astype(BF16),
        "walrt": walr_p.T.astype(BF16),
        "waup": waup_p.astype(BF16),
        "waupt": waup_p.T.astype(BF16),
        "ba": row(b_a), "bac": b_a.reshape(-1, 1),
        "glag": gla_norm_g, "lng": row(gmlp_ln_g), "lnb": row(gmlp_ln_b),
        "ws": w_s,
        "ws_sample": jnp.tile(ws_small, (1, reps, reps)),
        "bsf_prompt": jnp.repeat(b_s[:, :GMLP_CHUNK].T, GMLP_DH, axis=1),
        "bsf_sample": jnp.tile(jnp.repeat(b_s[:, :seq_len].T, GMLP_DH, axis=1), (reps, 1)),
        "gmg": gmlp_norm_g,
        "wout": w_out.astype(BF16),
        "ln1g": row(ln1_g), "ln1b": row(ln1_b),
        "wrt": w_router.T.astype(BF16), "rbias": router_bias.reshape(-1, 1),
        "wsg": ws_gate.astype(BF16), "wsu": ws_up.astype(BF16), "wsd": ws_down.astype(BF16),
        "ln2g": row(ln2_g), "ln2b": row(ln2_b),
    }


def kernel(x_prompt, x_sample, state_gla, w_in, w_a_up, b_a, gla_norm_g, gmlp_ln_g, gmlp_ln_b, w_s, b_s,
           gmlp_norm_g, w_out, ln1_g, ln1_b, w_router, router_bias, w_gate, w_up, w_down, ws_gate, ws_up,
           ws_down, ln2_g, ln2_b):
    assert x_prompt.shape[1] % PROMPT_TILE == 0 and x_sample.shape[0] % SAMPLE_SEQS == 0
    assert x_sample.shape[1] <= GMLP_CHUNK and w_in.shape[0] == DEPTH
    assert (x_sample.shape[0] * x_sample.shape[1]) % BLOCK == 0 and PROMPT_TILE % BLOCK == 0
    bsz, seq, _ = x_prompt.shape
    dbsz, dseq, _ = x_sample.shape
    hp, hs = x_prompt, x_sample
    gla_p, gla_s, v_s = [], [], []
    for l in range(DEPTH):
        wts = _prep_weights(dseq, w_in[l], w_a_up[l], b_a[l], gla_norm_g[l], gmlp_ln_g[l], gmlp_ln_b[l], w_s[l],
                            b_s[l], gmlp_norm_g[l], w_out[l], ln1_g[l], ln1_b[l], w_router[l], router_bias[l],
                            ws_gate[l], ws_up[l], ws_down[l], ln2_g[l], ln2_b[l])
        h_p, sp = _mixer_prompt(hp, wts)
        h_s, ss, vrows = _mixer_sample(hs, state_gla[l], wts)
        y_p, y_s = _moe(h_p.reshape(bsz * seq, D_MODEL), h_s, w_gate[l], w_up[l], w_down[l], wts)
        hp = y_p.reshape(bsz, seq, D_MODEL)
        hs = y_s.reshape(dbsz, dseq, D_MODEL)
        gla_p.append(sp)
        gla_s.append(ss)
        v_s.append(vrows)
    return (hp, hs, jnp.stack(gla_p), jnp.stack(gla_s), jnp.stack(v_s))
```

```python
import functools
import math

import jax
import jax.numpy as jnp
from jax import lax
from jax.experimental import pallas as pl
from jax.experimental.pallas import tpu as pltpu

F32 = jnp.float32
BF16 = jnp.bfloat16

D_MODEL = 1024
DEPTH = 1
GLA_WIDTH = 512
GLA_HEADS = 4
GLA_DK = 64
GLA_DV = 128
GLA_RANK = 16
GLA_TAU = 16.0
GMLP_WIDTH = 512
GMLP_HEADS = 4
GMLP_DH = 128
GMLP_CHUNK = 128
QK_WIDTH = GLA_HEADS * GLA_DK
N_EXPERTS = 64
TOP_K = 8
N_GROUPS = 8
GROUP_SIZE = N_EXPERTS // N_GROUPS
TOPK_GROUPS = 4
D_EXPERT = 256
D_SHARED = 256
ROUTE_SCALE = 2.5
ALPHA = (2.0 * DEPTH) ** 0.25

LANES = 128

C_Q = 0
C_K = C_Q + QK_WIDTH
C_VA = C_K + QK_WIDTH
C_G = C_VA + GLA_WIDTH
C_U = C_G + GLA_WIDTH
C_VB = C_U + GMLP_WIDTH
C_ALR = C_VB + GMLP_WIDTH
N_PROJ = C_ALR + LANES

CHUNK = 128
PROMPT_TILE = 256
PROMPT_SEQS = 2
SAMPLE_SEQS = 32
VMEM_LIMIT = 56 * 1024 * 1024

BLOCK = 256
SEG = 16
TILE_CHUNKS = 32
SUB_CHUNKS = 16
TILE_ROWS = TILE_CHUNKS * SEG
GATHER_ROWS = 512
BLOCK_ROWS_USED = BLOCK * TOP_K + N_EXPERTS * (SEG - 1)
BLOCK_ROWS = -(-(BLOCK_ROWS_USED + SEG) // GATHER_ROWS) * GATHER_ROWS
ZERO_CHUNK = BLOCK_ROWS_USED // SEG
IN_BUFS = 3


def _dot(a, b):
    return jnp.dot(a, b, preferred_element_type=F32)


def _dot_nt(a, b):
    return lax.dot_general(a, b, (((1,), (1,)), ((), ())), preferred_element_type=F32)


def _dot_tn(a, b):
    return lax.dot_general(a, b, (((0,), (0,)), ((), ())), preferred_element_type=F32)


def _shr(x, d):
    assert d & (d - 1) == 0
    return lax.shift_right_logical(x, d.bit_length() - 1)


def _split_dot(m01, x):
    hi = x.astype(BF16)
    lo = (x - hi.astype(F32)).astype(BF16)
    return _dot(m01, hi) + _dot(m01, lo)


def _split_dot_r(x, m01):
    hi = x.astype(BF16)
    lo = (x - hi.astype(F32)).astype(BF16)
    return _dot(hi, m01) + _dot(lo, m01)


def _sigmoid(x):
    return 1.0 / (1.0 + jnp.exp(-x))


def _silu(x):
    return x * _sigmoid(x)


def _gelu(x):
    c = math.sqrt(2.0 / math.pi)
    return x * (0.5 * (1.0 + jnp.tanh(c * (x + 0.044715 * (x * x * x)))))


def _log_sigmoid(x):
    return -(jnp.maximum(-x, 0.0) + jnp.log1p(jnp.exp(-jnp.abs(x))))


def _layernorm(x, g, b, eps=1e-5):
    mu = jnp.mean(x, axis=-1, keepdims=True)
    xc = x - mu
    var = jnp.mean(xc * xc, axis=-1, keepdims=True)
    return xc * lax.rsqrt(var + eps) * g + b


def _rmsnorm(x, g, eps=1e-6):
    return x * lax.rsqrt(jnp.mean(x * x, axis=-1, keepdims=True) + eps) * g


def _gmlp_heads(u, vg, w_mix, bias, gmg_ref, merged_ref, rows):
    vgb = vg.astype(BF16)
    for h in range(GMLP_HEADS):
        cs = slice(h * GMLP_DH, (h + 1) * GMLP_DH)
        sgu = _dot(w_mix(h), vgb[:, cs]) + bias[:, cs]
        y = _rmsnorm(u[:, cs] * sgu, gmg_ref[h:h + 1, :])
        merged_ref[rows, GLA_WIDTH + h * GMLP_DH:GLA_WIDTH + (h + 1) * GMLP_DH] = y.astype(BF16)


def _out_proj_ln(x, merged_ref, wout_ref, ln1g_ref, ln1b_ref):
    m = _dot(merged_ref[...], wout_ref[...])
    return _layernorm(ALPHA * x + m, ln1g_ref[...], ln1b_ref[...])


def _mixer_prompt_kernel(x_ref, win_ref, waup_ref, ba_ref, glag_ref, lng_ref, lnb_ref, ws_ref, bsf_ref,
                         gmg_ref, wout_ref, ln1g_ref, ln1b_ref,
                         h_ref, state_ref,
                         *scratch):
    t = pl.program_id(1)
    per_seq = [scratch[3 * s:3 * s + 3] for s in range(PROMPT_SEQS)]

    @pl.when(t == 0)
    def _():
        for _, _, st_ref in per_seq:
            st_ref[...] = jnp.zeros_like(st_ref)

    for s, (z_ref, _, _) in enumerate(per_seq):
        z_ref[...] = _dot(x_ref[s].astype(BF16), win_ref[...])
    for s, (z_ref, merged_ref, st_ref) in enumerate(per_seq):
        h_ref[s] = _prompt_tile(x_ref[s], win_ref, waup_ref, ba_ref, glag_ref, lng_ref, lnb_ref, ws_ref, bsf_ref,
                                gmg_ref, wout_ref, ln1g_ref, ln1b_ref, z_ref, merged_ref, st_ref)

    @pl.when(t == pl.num_programs(1) - 1)
    def _():
        for s, (_, _, st_ref) in enumerate(per_seq):
            for h in range(GLA_HEADS):
                lo = (h % 2) * GLA_DK
                state_ref[s, h] = st_ref[h].T[lo:lo + GLA_DK, :]


def _prompt_tile(x, win_ref, waup_ref, ba_ref, glag_ref, lng_ref, lnb_ref, ws_ref, bsf_ref, gmg_ref, wout_ref,
                 ln1g_ref, ln1b_ref, z_ref, merged_ref, st_ref):
    row_i = lax.broadcasted_iota(jnp.int32, (CHUNK, CHUNK), 0)
    col_i = lax.broadcasted_iota(jnp.int32, (CHUNK, CHUNK), 1)
    causal = row_i >= col_i
    tri = jnp.where(causal, 1.0, 0.0).astype(BF16)
    lane = lax.broadcasted_iota(jnp.int32, (1, LANES), 1)
    head_lanes = [lane < GLA_DK, lane >= GLA_DK]
    mid = CHUNK // 2 - 1

    for c in range(PROMPT_TILE // CHUNK):
        rows = slice(c * CHUNK, (c + 1) * CHUNK)
        a_pre = _dot(z_ref[rows, C_ALR:C_ALR + LANES].astype(BF16), waup_ref[...]) + ba_ref[...]
        log_a = _log_sigmoid(a_pre) * (1.0 / GLA_TAU)
        b = _split_dot(tri, log_a)
        b_mid = b[mid:mid + 1, :]
        b_last = b[CHUNK - 1:CHUNK, :]
        q = z_ref[rows, C_Q:C_Q + QK_WIDTH] * (GLA_DK ** -0.5)
        k = z_ref[rows, C_K:C_K + QK_WIDTH]
        q_in = (q * jnp.exp(b - b_mid)).astype(BF16)
        k_in = (k * jnp.exp(b_mid - b)).astype(BF16)
        q_st = (q * jnp.exp(b)).astype(BF16)
        k_st = (k * jnp.exp(b_last - b)).astype(BF16)
        d_last = jnp.exp(b_last)
        va = z_ref[rows, C_VA:C_VA + GLA_WIDTH].astype(BF16)
        for h in range(GLA_HEADS):
            ps = slice((h // 2) * LANES, (h // 2 + 1) * LANES)
            vs = slice(h * GLA_DV, (h + 1) * GLA_DV)
            hm = head_lanes[h % 2]
            zero = jnp.zeros((), BF16)
            a = _dot_nt(jnp.where(hm, q_in[:, ps], zero), k_in[:, ps])
            a = jnp.where(causal, a, 0.0).astype(BF16)
            st = st_ref[h]
            o = _dot(a, va[:, vs]) + _dot_nt(jnp.where(hm, q_st[:, ps], zero), st.astype(BF16))
            upd = _dot_tn(va[:, vs], jnp.where(hm, k_st[:, ps], zero))
            st_ref[h] = st * d_last[:, ps] + upd
            gate = z_ref[rows, C_G + h * GLA_DV:C_G + (h + 1) * GLA_DV]
            o = _rmsnorm(o, glag_ref[h:h + 1, :]) * _silu(gate)
            merged_ref[rows, vs] = o.astype(BF16)
        u = _gelu(z_ref[rows, C_U:C_U + GMLP_WIDTH])
        vg = _layernorm(_gelu(z_ref[rows, C_VB:C_VB + GMLP_WIDTH]), lng_ref[...], lnb_ref[...])
        _gmlp_heads(u, vg, lambda h: jnp.where(causal, ws_ref[h], 0.0).astype(BF16), bsf_ref[...],
                    gmg_ref, merged_ref, rows)

    return _out_proj_ln(x, merged_ref, wout_ref, ln1g_ref, ln1b_ref)


def _mixer_sample_kernel(seq_len, x_ref, s0_ref, win_ref, wkt_ref, walrt_ref, waup_ref, waupt_ref, ba_ref, bac_ref,
                         glag_ref, lng_ref, lnb_ref, wst_ref, bsf_ref, gmg_ref, wout_ref, ln1g_ref, ln1b_ref,
                         h_ref, snew_ref, vg_ref,
                         merged_ref):
    n = SAMPLE_SEQS * seq_len
    x = x_ref[...]
    xb = x.astype(BF16)
    z = _dot(xb, win_ref[...])

    ti = lax.broadcasted_iota(jnp.int32, (n, n), 0)
    tj = lax.broadcasted_iota(jnp.int32, (n, n), 1)
    same = _shr(ti, seq_len) == _shr(tj, seq_len)
    causal = jnp.logical_and(same, ti >= tj)
    tri = jnp.where(causal, 1.0, 0.0).astype(BF16)
    tri_t = jnp.where(jnp.logical_and(same, ti <= tj), 1.0, 0.0).astype(BF16)
    same01 = jnp.where(same, 1.0, 0.0).astype(BF16)

    a_pre = _dot(z[:, C_ALR:C_ALR + LANES].astype(BF16), waup_ref[...]) + ba_ref[...]
    log_a = _log_sigmoid(a_pre) * (1.0 / GLA_TAU)
    b = _split_dot(tri, log_a)
    q = z[:, C_Q:C_Q + QK_WIDTH] * (GLA_DK ** -0.5)
    k = z[:, C_K:C_K + QK_WIDTH]
    q_in = (q * jnp.exp(b)).astype(BF16)
    k_in = (k * jnp.exp(-b)).astype(BF16)
    va = z[:, C_VA:C_VA + GLA_WIDTH].astype(BF16)

    k_t = _dot_nt(wkt_ref[...], xb)
    alr_t = _dot_nt(walrt_ref[...], xb)
    a_pre_t = _dot(waupt_ref[...], alr_t.astype(BF16)) + bac_ref[...]
    log_a_t = _log_sigmoid(a_pre_t) * (1.0 / GLA_TAU)
    b_t = _split_dot_r(log_a_t, tri_t)
    tot_t = _split_dot_r(log_a_t, same01)
    k_st_t = k_t * jnp.exp(tot_t - b_t)
    d_t = jnp.exp(tot_t)

    lane = lax.broadcasted_iota(jnp.int32, (1, LANES), 1)
    head_lanes = [lane < GLA_DK, lane >= GLA_DK]
    nrow = SAMPLE_SEQS * GLA_DK
    r_seq = _shr(lax.broadcasted_iota(jnp.int32, (nrow, n), 0), GLA_DK)
    c_seq = _shr(lax.broadcasted_iota(jnp.int32, (nrow, n), 1), seq_len)
    c_first = (lax.broadcasted_iota(jnp.int32, (nrow, n), 1) & (seq_len - 1)) == 0
    blk = r_seq == c_seq
    blk_first = jnp.logical_and(blk, c_first)
    q_seq = _shr(lax.broadcasted_iota(jnp.int32, (n, nrow), 0), seq_len)
    q_col = _shr(lax.broadcasted_iota(jnp.int32, (n, nrow), 1), GLA_DK)
    blk_q = q_seq == q_col

    for h in range(GLA_HEADS):
        ps = slice((h // 2) * LANES, (h // 2 + 1) * LANES)
        ds_ = slice(h * GLA_DK, (h + 1) * GLA_DK)
        vs = slice(h * GLA_DV, (h + 1) * GLA_DV)
        hm = head_lanes[h % 2]
        zero = jnp.zeros((), BF16)
        a = _dot_nt(jnp.where(hm, q_in[:, ps], zero), k_in[:, ps])
        a = jnp.where(causal, a, 0.0).astype(BF16)
        s0 = s0_ref[:, h].reshape(nrow, GLA_DV)
        q_h = q_in[:, ds_]
        q_bd = jnp.where(blk_q, jnp.concatenate([q_h] * SAMPLE_SEQS, axis=1), zero)
        o = _dot(a, va[:, vs]) + _dot(q_bd, s0.astype(BF16))
        k_bd = jnp.where(blk, jnp.concatenate([k_st_t[ds_, :]] * SAMPLE_SEQS, axis=0), 0.0).astype(BF16)
        upd = _dot(k_bd, va[:, vs])
        d_bd = jnp.where(blk_first, jnp.concatenate([d_t[ds_, :]] * SAMPLE_SEQS, axis=0), 0.0)
        d_col = jnp.sum(d_bd, axis=1, keepdims=True)
        snew_ref[:, h] = (s0 * d_col + upd).reshape(SAMPLE_SEQS, GLA_DK, GLA_DV)
        gate = z[:, C_G + h * GLA_DV:C_G + (h + 1) * GLA_DV]
        o = _rmsnorm(o, glag_ref[h:h + 1, :]) * _silu(gate)
        merged_ref[:, vs] = o.astype(BF16)

    u = _gelu(z[:, C_U:C_U + GMLP_WIDTH])
    vg = _layernorm(_gelu(z[:, C_VB:C_VB + GMLP_WIDTH]), lng_ref[...], lnb_ref[...])
    vg_ref[...] = vg
    _gmlp_heads(u, vg, lambda h: jnp.where(causal, wst_ref[h], 0.0).astype(BF16), bsf_ref[...],
                gmg_ref, merged_ref, slice(None))
    h_ref[...] = _out_proj_ln(x, merged_ref, wout_ref, ln1g_ref, ln1b_ref)


def _route(hb, wrt_ref, rbias_ref):
    tm = hb.shape[0]
    s = _sigmoid(_dot_nt(wrt_ref[...], hb))
    sb = s + rbias_ref[...]
    neg = jnp.float32(-jnp.inf)
    sub = lax.broadcasted_iota(jnp.int32, (GROUP_SIZE, tm), 0)
    gscore = []
    for g in range(N_GROUPS):
        blk = sb[g * GROUP_SIZE:(g + 1) * GROUP_SIZE, :]
        m1 = jnp.max(blk, axis=0, keepdims=True)
        i1 = jnp.min(jnp.where(blk == m1, sub, GROUP_SIZE), axis=0, keepdims=True)
        m2 = jnp.max(jnp.where(sub == i1, neg, blk), axis=0, keepdims=True)
        gscore.append(m1 + m2)
    gsel = []
    for g in range(N_GROUPS):
        rank = jnp.zeros((1, tm), jnp.int32)
        for o in range(N_GROUPS):
            if o == g:
                continue
            ahead = (gscore[o] >= gscore[g]) if o < g else (gscore[o] > gscore[g])
            rank = rank + ahead.astype(jnp.int32)
        gsel.append(rank < TOPK_GROUPS)
    v = jnp.concatenate(
        [jnp.where(gsel[g], sb[g * GROUP_SIZE:(g + 1) * GROUP_SIZE, :], neg) for g in range(N_GROUPS)], axis=0)
    eidx = lax.broadcasted_iota(jnp.int32, (N_EXPERTS, tm), 0)
    sel = jnp.zeros((N_EXPERTS, tm), F32)
    for _ in range(TOP_K):
        m = jnp.max(v, axis=0, keepdims=True)
        first = jnp.min(jnp.where(v == m, eidx, N_EXPERTS), axis=0, keepdims=True)
        hit = eidx == first
        sel = jnp.where(hit, 1.0, sel)
        v = jnp.where(hit, neg, v)
    wsel = s * sel
    return wsel / jnp.sum(wsel, axis=0, keepdims=True) * ROUTE_SCALE, sel


def _per_row(chunk_vals):
    n, t = chunk_vals.shape
    return jnp.broadcast_to(chunk_vals[:, None, :], (n, SEG, t)).reshape(n * SEG, t)


def _row_lookup(seg_b, start_b, rank, tables):
    n_chunks = BLOCK_ROWS // SEG
    ei = lax.broadcasted_iota(jnp.int32, (N_EXPERTS, LANES), 0)
    ej = lax.broadcasted_iota(jnp.int32, (N_EXPERTS, LANES), 1)
    seg16 = (seg_b * (1.0 / SEG)).astype(BF16)
    start_row = _dot_tn(seg16, jnp.where(ei < ej, 1.0, 0.0).astype(BF16))[0:1, :]
    seg_row = _dot_tn(seg16, jnp.where(ei == ej, 1.0, 0.0).astype(BF16))[0:1, :]
    c = lax.broadcasted_iota(jnp.int32, (n_chunks, LANES), 0).astype(F32)
    owner = jnp.where(c >= start_row, jnp.where(c < start_row + seg_row, 1.0, 0.0), 0.0).astype(BF16)

    def lookup(tab):
        padded = jnp.concatenate([tab.astype(BF16), jnp.zeros((LANES - N_EXPERTS, tab.shape[1]), BF16)], axis=0)
        return _dot(owner, padded)

    first = SEG * (c - lookup(start_b * (1.0 / SEG)))[:, 0:1]
    rel = lookup(rank) - first
    sub = lax.broadcasted_iota(jnp.int32, (BLOCK_ROWS, rank.shape[1]), 0) & (SEG - 1)
    hits = _per_row(rel) == sub.astype(F32)
    return hits, [_per_row(lookup(tab)) for tab in tables]


def _dispatch_kernel(n_prompt_blocks, hp_ref, hs_ref, wrt_ref, rbias_ref, x_ref, rank_ref, comb_ref, seg_ref,
                     start_ref):
    tm = BLOCK
    hb = jnp.where(pl.program_id(0) < n_prompt_blocks, hp_ref[...], hs_ref[...]).astype(BF16)
    comb, sel = _route(hb, wrt_ref, rbias_ref)
    ti = lax.broadcasted_iota(jnp.int32, (tm, tm), 0)
    tj = lax.broadcasted_iota(jnp.int32, (tm, tm), 1)
    before = jnp.where(ti < tj, 1.0, 0.0).astype(BF16)
    rank = _dot(sel.astype(BF16), before)
    rank = jnp.where(sel > 0.0, rank, -1.0)
    cnt = jnp.sum(sel, axis=1, keepdims=True)
    seg = jnp.floor((cnt + (SEG - 1.0)) * (1.0 / SEG)) * SEG
    ei = lax.broadcasted_iota(jnp.int32, (N_EXPERTS, N_EXPERTS), 0)
    ej = lax.broadcasted_iota(jnp.int32, (N_EXPERTS, N_EXPERTS), 1)
    below = jnp.where(ej < ei, 1.0, 0.0).astype(BF16)
    seg_b = jnp.broadcast_to(seg, (N_EXPERTS, LANES))
    start_b = _dot(below, seg_b.astype(BF16))
    hits, _ = _row_lookup(seg_b, start_b, rank, [])
    gather = jnp.where(hits, 1.0, 0.0).astype(BF16)
    for c in range(BLOCK_ROWS // GATHER_ROWS):
        rs = slice(c * GATHER_ROWS, (c + 1) * GATHER_ROWS)
        x_ref[rs, :] = _dot(gather[rs, :], hb).astype(BF16)
    rank_ref[0] = rank
    comb_ref[0] = comb
    seg_ref[0] = seg_b
    start_ref[0] = start_b


def _ffn_kernel(nt_max, tile_expert_ref, tile_in_expert_ref, n_tiles_ref,
                tab0_ref, tab1_ref, tab2_ref, x_hbm, wg_ref, wu_ref, wd_ref,
                o_ref, ibuf, wgb, wub, wdb, in_sem):
    del tile_expert_ref
    i = pl.program_id(0)
    n_tiles = n_tiles_ref[0]
    tabs = (tab0_ref, tab1_ref, tab2_ref)

    def start_in(ahead, slot):
        exists = i + ahead < n_tiles
        first = jnp.where(exists, tile_in_expert_ref[jnp.minimum(i + ahead, nt_max - 1)] * TILE_CHUNKS, 0)
        srcs = [jnp.where(exists, tabs[ahead][0, 0, first + j], ZERO_CHUNK) for j in range(TILE_CHUNKS)]
        for j in range(TILE_CHUNKS):
            pltpu.make_async_copy(x_hbm.at[srcs[j]], ibuf.at[slot, j], in_sem.at[slot]).start()

    def in_wait(slot):
        for j in range(TILE_CHUNKS):
            pltpu.make_async_copy(x_hbm.at[0], ibuf.at[slot, j], in_sem.at[slot]).wait()

    @pl.when(i >= n_tiles)
    def _():
        o_ref[...] = jnp.zeros_like(o_ref)

    @pl.when(i < n_tiles)
    def _():
        islot = lax.rem(i, IN_BUFS)
        nslot = jnp.where(islot == 0, IN_BUFS - 1, islot - 1)

        @pl.when(i == 0)
        def _():
            start_in(0, 0)
            start_in(1, 1)

        in_wait(islot)

        @pl.when(tile_in_expert_ref[i] == 0)
        def _():
            wgb[...] = wg_ref[0].astype(BF16)
            wub[...] = wu_ref[0].astype(BF16)
            wdb[...] = wd_ref[0].astype(BF16)

        start_in(2, nslot)
        sub_rows = SUB_CHUNKS * SEG
        for s in range(TILE_CHUNKS // SUB_CHUNKS):
            x = ibuf[islot, s * SUB_CHUNKS:(s + 1) * SUB_CHUNKS].reshape(sub_rows, D_MODEL)
            hid = _silu(_dot(x, wgb[...])) * _dot(x, wub[...])
            o_ref[s * sub_rows:(s + 1) * sub_rows, :] = _dot(hid.astype(BF16), wdb[...]).astype(BF16)

        @pl.when(i == n_tiles - 1)
        def _():
            in_wait(lax.rem(i + 1, IN_BUFS))
            in_wait(nslot)


def _combine_kernel(block_off, n_blocks, tab_ref, tab_next_ref, h_ref, o_hbm, rank_ref, comb_ref, seg_ref, start_ref,
                    wsg_ref, wsu_ref, wsd_ref, ln2g_ref, ln2b_ref, y_ref, obuf, sem):
    del block_off
    i = pl.program_id(0)
    slot = lax.rem(i, 2)
    n_chunks = BLOCK_ROWS // SEG

    def start_gather(t_ref, to_slot):
        srcs = [t_ref[0, 0, k] for k in range(n_chunks)]
        for k in range(n_chunks):
            pltpu.make_async_copy(o_hbm.at[srcs[k]], obuf.at[to_slot, k], sem.at[to_slot]).start()

    @pl.when(i == 0)
    def _():
        start_gather(tab_ref, 0)

    @pl.when(i + 1 < n_blocks)
    def _():
        start_gather(tab_next_ref, 1 - slot)

    for k in range(n_chunks):
        pltpu.make_async_copy(o_hbm.at[0], obuf.at[slot, k], sem.at[slot]).wait()

    h = h_ref[...]
    hb = h.astype(BF16)
    hits, (comb_rows,) = _row_lookup(seg_ref[0], start_ref[0], rank_ref[0], [comb_ref[0]])
    scatter = jnp.where(hits, comb_rows, 0.0).astype(BF16)
    routed = _dot_tn(scatter, obuf[slot].reshape(BLOCK_ROWS, D_MODEL))
    shared = _dot((_silu(_dot(hb, wsg_ref[...])) * _dot(hb, wsu_ref[...])).astype(BF16), wsd_ref[...])
    y_ref[...] = _layernorm(ALPHA * h + (routed + shared), ln2g_ref[...], ln2b_ref[...])


def _full(shape):
    return pl.BlockSpec(shape, lambda *_: (0,) * len(shape))


def _mixer_prompt(x, wts):
    bsz, seq, _ = x.shape
    n_t = seq // PROMPT_TILE
    weights = [wts[k] for k in ("win", "waup", "ba", "glag", "lng", "lnb", "ws", "bsf_prompt", "gmg", "wout",
                                "ln1g", "ln1b")]
    return pl.pallas_call(
        _mixer_prompt_kernel,
        grid=(bsz // PROMPT_SEQS, n_t),
        in_specs=[pl.BlockSpec((PROMPT_SEQS, PROMPT_TILE, D_MODEL), lambda b, t: (b, t, 0))]
        + [_full(w.shape) for w in weights],
        out_specs=[pl.BlockSpec((PROMPT_SEQS, PROMPT_TILE, D_MODEL), lambda b, t: (b, t, 0)),
                   pl.BlockSpec((PROMPT_SEQS, GLA_HEADS, GLA_DK, GLA_DV), lambda b, t: (b, 0, 0, 0))],
        out_shape=[jax.ShapeDtypeStruct((bsz, seq, D_MODEL), F32),
                   jax.ShapeDtypeStruct((bsz, GLA_HEADS, GLA_DK, GLA_DV), F32)],
        scratch_shapes=[pltpu.VMEM((PROMPT_TILE, N_PROJ), F32),
                        pltpu.VMEM((PROMPT_TILE, D_MODEL), BF16),
                        pltpu.VMEM((GLA_HEADS, GLA_DV, LANES), F32)] * PROMPT_SEQS,
        compiler_params=pltpu.CompilerParams(dimension_semantics=("arbitrary", "arbitrary"),
                                             vmem_limit_bytes=VMEM_LIMIT),
        name="mixer_prompt",
    )(x, *weights)


def _mixer_sample(x, s0, wts):
    bsz, seq_len, _ = x.shape
    n = SAMPLE_SEQS * seq_len
    x2 = x.reshape(bsz * seq_len, D_MODEL)
    weights = [wts[k] for k in ("win", "wkt", "walrt", "waup", "waupt", "ba", "bac", "glag", "lng", "lnb",
                                "ws_sample", "bsf_sample", "gmg", "wout", "ln1g", "ln1b")]
    state_spec = pl.BlockSpec((SAMPLE_SEQS, GLA_HEADS, GLA_DK, GLA_DV), lambda i: (i, 0, 0, 0))
    h, s_new, vg = pl.pallas_call(
        functools.partial(_mixer_sample_kernel, seq_len),
        grid=(bsz // SAMPLE_SEQS,),
        in_specs=[pl.BlockSpec((n, D_MODEL), lambda i: (i, 0)), state_spec] + [_full(w.shape) for w in weights],
        out_specs=[pl.BlockSpec((n, D_MODEL), lambda i: (i, 0)), state_spec,
                   pl.BlockSpec((n, GMLP_WIDTH), lambda i: (i, 0))],
        out_shape=[jax.ShapeDtypeStruct((bsz * seq_len, D_MODEL), F32),
                   jax.ShapeDtypeStruct(s0.shape, F32),
                   jax.ShapeDtypeStruct((bsz * seq_len, GMLP_WIDTH), F32)],
        scratch_shapes=[pltpu.VMEM((n, D_MODEL), BF16)],
        compiler_params=pltpu.CompilerParams(dimension_semantics=("arbitrary",), vmem_limit_bytes=VMEM_LIMIT),
        name="mixer_sample",
    )(x2, s0, *weights)
    return h, s_new, vg.reshape(bsz, seq_len, GMLP_WIDTH)


def _dispatch(h_p, h_s, wts):
    npb, nsb = h_p.shape[0] // BLOCK, h_s.shape[0] // BLOCK
    nb = npb + nsb
    tok_spec = pl.BlockSpec((1, N_EXPERTS, BLOCK), lambda i: (i, 0, 0))
    run_spec = pl.BlockSpec((1, N_EXPERTS, LANES), lambda i: (i, 0, 0))
    return pl.pallas_call(
        functools.partial(_dispatch_kernel, npb),
        grid=(nb,),
        in_specs=[pl.BlockSpec((BLOCK, D_MODEL), lambda i: (jnp.minimum(i, npb - 1), 0)),
                  pl.BlockSpec((BLOCK, D_MODEL), lambda i: (jnp.clip(i - npb, 0, nsb - 1), 0)),
                  _full(wts["wrt"].shape), _full(wts["rbias"].shape)],
        out_specs=[pl.BlockSpec((BLOCK_ROWS, D_MODEL), lambda i: (i, 0)), tok_spec, tok_spec, run_spec, run_spec],
        out_shape=[jax.ShapeDtypeStruct((nb * BLOCK_ROWS, D_MODEL), BF16),
                   jax.ShapeDtypeStruct((nb, N_EXPERTS, BLOCK), F32),
                   jax.ShapeDtypeStruct((nb, N_EXPERTS, BLOCK), F32),
                   jax.ShapeDtypeStruct((nb, N_EXPERTS, LANES), F32),
                   jax.ShapeDtypeStruct((nb, N_EXPERTS, LANES), F32)],
        compiler_params=pltpu.CompilerParams(dimension_semantics=("arbitrary",), vmem_limit_bytes=VMEM_LIMIT),
        name="moe_dispatch",
    )(h_p, h_s, wts["wrt"], wts["rbias"])


def _max_tiles(nb):
    return (nb * (BLOCK_ROWS_USED // SEG) + N_EXPERTS * (TILE_CHUNKS - 1)) // TILE_CHUNKS + 1


def _max_expert_chunks(nb):
    most = nb * (BLOCK // SEG) + TILE_CHUNKS - 1
    return -(-most // LANES) * LANES


def _plan(seg_lanes, nb):
    nt_max = _max_tiles(nb)
    max_chunks = _max_expert_chunks(nb)
    seg = seg_lanes[:nb, :, 0].astype(jnp.int32)
    start = jnp.cumsum(seg, axis=1) - seg
    nch = seg // SEG
    ends = jnp.cumsum(nch, axis=0)
    n_chunks = ends[-1]
    f = jnp.arange(nb, dtype=jnp.int32)[:, None] * (BLOCK_ROWS // SEG) + start // SEG - (ends - nch)
    q = jnp.arange(max_chunks, dtype=jnp.int32)
    passed = (ends[None, :-1, :] <= q[:, None, None]).astype(jnp.int32)
    chunk = q[:, None] + f[0][None, :] + jnp.sum(passed * (f[1:] - f[:-1])[None], axis=1)
    table = jnp.where(q[:, None] < n_chunks[None, :], chunk, ZERO_CHUNK).T
    n_tiles_e = -(-n_chunks // TILE_CHUNKS)
    tile_ends = jnp.cumsum(n_tiles_e)
    t = jnp.arange(nt_max, dtype=jnp.int32)
    done = (tile_ends[None, :] <= t[:, None]).astype(jnp.int32)
    tile_expert = jnp.minimum(jnp.sum(done, axis=1), N_EXPERTS - 1)
    tile_in_expert = t - jnp.sum(done * n_tiles_e[None, :], axis=1)
    run_end = (start + seg) // SEG
    g = (TILE_CHUNKS * (tile_ends - n_tiles_e))[None, :] + (ends - nch) - start // SEG
    k = jnp.arange(BLOCK_ROWS // SEG, dtype=jnp.int32)
    over = (run_end[:, None, :-1] <= k[None, :, None]).astype(jnp.int32)
    back = k[None, :] + g[:, :1] + jnp.sum(over * (g[:, 1:] - g[:, :-1])[:, None, :], axis=2)
    back = jnp.where(k[None, :] < run_end[:, -1:], back, 0)
    return (table.astype(jnp.int32).reshape(N_EXPERTS, 1, max_chunks), tile_expert.astype(jnp.int32),
            tile_in_expert.astype(jnp.int32), tile_ends[-1:].astype(jnp.int32),
            back.astype(jnp.int32).reshape(nb, 1, BLOCK_ROWS // SEG))


def _ffn(x_rows, table, tile_expert, tile_in_expert, n_tiles, w_gate, w_up, w_down):
    nt_max = tile_expert.shape[0]
    max_chunks = table.shape[-1]
    tab_spec = lambda ahead: pl.BlockSpec(
        (1, 1, max_chunks), lambda i, te, *_: (te[jnp.minimum(i + ahead, nt_max - 1)], 0, 0),
        memory_space=pltpu.SMEM)
    w_spec = lambda shape: pl.BlockSpec((1,) + shape, lambda i, te, *_: (te[i], 0, 0))
    x_chunks = x_rows.reshape(x_rows.shape[0] // SEG, SEG, D_MODEL)
    return pl.pallas_call(
        functools.partial(_ffn_kernel, nt_max),
        grid_spec=pltpu.PrefetchScalarGridSpec(
            num_scalar_prefetch=3,
            grid=(nt_max,),
            in_specs=[tab_spec(0), tab_spec(1), tab_spec(2), pl.BlockSpec(memory_space=pl.ANY),
                      w_spec((D_MODEL, D_EXPERT)), w_spec((D_MODEL, D_EXPERT)), w_spec((D_EXPERT, D_MODEL))],
            out_specs=pl.BlockSpec((TILE_ROWS, D_MODEL), lambda i, *_: (i, 0)),
            scratch_shapes=[pltpu.VMEM((IN_BUFS, TILE_CHUNKS, SEG, D_MODEL), BF16),
                            pltpu.VMEM((D_MODEL, D_EXPERT), BF16),
                            pltpu.VMEM((D_MODEL, D_EXPERT), BF16),
                            pltpu.VMEM((D_EXPERT, D_MODEL), BF16),
                            pltpu.SemaphoreType.DMA((IN_BUFS,))]),
        out_shape=jax.ShapeDtypeStruct((nt_max * TILE_ROWS, D_MODEL), BF16),
        compiler_params=pltpu.CompilerParams(dimension_semantics=("arbitrary",), vmem_limit_bytes=VMEM_LIMIT),
        name="moe_ffn",
    )(tile_expert, tile_in_expert, n_tiles, table, table, table, x_chunks, w_gate, w_up, w_down)


def _combine(h, o_rows, back, routing, block_off, wts):
    nblk = h.shape[0] // BLOCK
    n_chunks = BLOCK_ROWS // SEG
    weights = [wts[k] for k in ("wsg", "wsu", "wsd", "ln2g", "ln2b")]
    tok_spec = pl.BlockSpec((1, N_EXPERTS, BLOCK), lambda i: (i + block_off, 0, 0))
    run_spec = pl.BlockSpec((1, N_EXPERTS, LANES), lambda i: (i + block_off, 0, 0))
    tab_spec = lambda ahead: pl.BlockSpec(
        (1, 1, n_chunks), lambda i: (block_off + jnp.minimum(i + ahead, nblk - 1), 0, 0), memory_space=pltpu.SMEM)
    return pl.pallas_call(
        functools.partial(_combine_kernel, block_off, nblk),
        grid=(nblk,),
        in_specs=[tab_spec(0), tab_spec(1),
                  pl.BlockSpec((BLOCK, D_MODEL), lambda i: (i, 0)),
                  pl.BlockSpec(memory_space=pl.ANY),
                  tok_spec, tok_spec, run_spec, run_spec] + [_full(w.shape) for w in weights],
        out_specs=pl.BlockSpec((BLOCK, D_MODEL), lambda i: (i, 0)),
        out_shape=jax.ShapeDtypeStruct(h.shape, F32),
        scratch_shapes=[pltpu.VMEM((2, n_chunks, SEG, D_MODEL), BF16), pltpu.SemaphoreType.DMA((2,))],
        compiler_params=pltpu.CompilerParams(dimension_semantics=("arbitrary",), vmem_limit_bytes=VMEM_LIMIT),
        name="moe_combine",
    )(back, back, h, o_rows.reshape(o_rows.shape[0] // SEG, SEG, D_MODEL), *routing, *weights)


def _moe(h_p, h_s, w_gate, w_up, w_down, wts):
    nb = (h_p.shape[0] + h_s.shape[0]) // BLOCK
    x_rows, rank, comb, seg_lanes, start_lanes = _dispatch(h_p, h_s, wts)
    *ffn_plan, back = _plan(seg_lanes, nb)
    o_rows = _ffn(x_rows, *ffn_plan, w_gate, w_up, w_down)
    routing = (rank, comb, seg_lanes, start_lanes)
    y_p = _combine(h_p, o_rows, back, routing, 0, wts)
    y_s = _combine(h_s, o_rows, back, routing, h_p.shape[0] // BLOCK, wts)
    return y_p, y_s


def _prep_weights(seq_len, w_in, w_a_up, b_a, gla_norm_g, gmlp_ln_g, gmlp_ln_b, w_s, b_s, gmlp_norm_g, w_out,
                  ln1_g, ln1_b, w_router, router_bias, ws_gate, ws_up, ws_down, ln2_g, ln2_b):
    o1 = QK_WIDTH
    o2 = o1 + QK_WIDTH
    o3 = o2 + GLA_WIDTH
    o4 = o3 + GLA_WIDTH
    o5 = o4 + GLA_RANK
    o6 = o5 + GMLP_WIDTH
    wq, wk, wva, wg_, walr, wu_, wvb = jnp.split(w_in, [o1, o2, o3, o4, o5, o6], axis=-1)
    walr_p = jnp.pad(walr, ((0, 0), (0, LANES - GLA_RANK)))
    waup_p = jnp.pad(w_a_up, ((0, LANES - GLA_RANK), (0, 0)))
    row = lambda a: a.reshape(1, -1)
    reps = (SAMPLE_SEQS * seq_len) // seq_len
    ws_small = w_s[:, :seq_len, :seq_len]
    return {
        "win": jnp.concatenate([wq, wk, wva, wg_, wu_, wvb, walr_p], axis=-1).astype(BF16),
        "wkt": wk.T.astype(BF16),
        "walrt": walr_p.T.astype(BF16),
        "waup": waup_p.astype(BF16),
        "waupt": waup_p.T.astype(BF16),
        "ba": row(b_a), "bac": b_a.reshape(-1, 1),
        "glag": gla_norm_g, "lng": row(gmlp_ln_g), "lnb": row(gmlp_ln_b),
        "ws": w_s,
        "ws_sample": jnp.tile(ws_small, (1, reps, reps)),
        "bsf_prompt": jnp.repeat(b_s[:, :GMLP_CHUNK].T, GMLP_DH, axis=1),
        "bsf_sample": jnp.tile(jnp.repeat(b_s[:, :seq_len].T, GMLP_DH, axis=1), (reps, 1)),
        "gmg": gmlp_norm_g,
        "wout": w_out.astype(BF16),
        "ln1g": row(ln1_g), "ln1b": row(ln1_b),
        "wrt": w_router.T.astype(BF16), "rbias": router_bias.reshape(-1, 1),
        "wsg": ws_gate.astype(BF16), "wsu": ws_up.astype(BF16), "wsd": ws_down.astype(BF16),
        "ln2g": row(ln2_g), "ln2b": row(ln2_b),
    }


def kernel(x_prompt, x_sample, state_gla, w_in, w_a_up, b_a, gla_norm_g, gmlp_ln_g, gmlp_ln_b, w_s, b_s,
           gmlp_norm_g, w_out, ln1_g, ln1_b, w_router, router_bias, w_gate, w_up, w_down, ws_gate, ws_up,
           ws_down, ln2_g, ln2_b):
    assert x_prompt.shape[1] % PROMPT_TILE == 0 and x_sample.shape[0] % SAMPLE_SEQS == 0
    assert x_prompt.shape[0] % PROMPT_SEQS == 0
    assert x_sample.shape[1] <= GMLP_CHUNK and w_in.shape[0] == DEPTH
    assert (x_sample.shape[0] * x_sample.shape[1]) % BLOCK == 0 and PROMPT_TILE % BLOCK == 0
    bsz, seq, _ = x_prompt.shape
    dbsz, dseq, _ = x_sample.shape
    hp, hs = x_prompt, x_sample
    gla_p, gla_s, v_s = [], [], []
    for l in range(DEPTH):
        wts = _prep_weights(dseq, w_in[l], w_a_up[l], b_a[l], gla_norm_g[l], gmlp_ln_g[l], gmlp_ln_b[l], w_s[l],
                            b_s[l], gmlp_norm_g[l], w_out[l], ln1_g[l], ln1_b[l], w_router[l], router_bias[l],
                            ws_gate[l], ws_up[l], ws_down[l], ln2_g[l], ln2_b[l])
        h_p, sp = _mixer_prompt(hp, wts)
        h_s, ss, vrows = _mixer_sample(hs, state_gla[l], wts)
        y_p, y_s = _moe(h_p.reshape(bsz * seq, D_MODEL), h_s, w_gate[l], w_up[l], w_down[l], wts)
        hp = y_p.reshape(bsz, seq, D_MODEL)
        hs = y_s.reshape(dbsz, dseq, D_MODEL)
        gla_p.append(sp)
        gla_s.append(ss)
        v_s.append(vrows)
    return (hp, hs, jnp.stack(gla_p), jnp.stack(gla_s), jnp.stack(v_s))
```

```python
import functools
import math

import jax
import jax.numpy as jnp
from jax import lax
from jax.experimental import pallas as pl
from jax.experimental.pallas import tpu as pltpu

F32 = jnp.float32
BF16 = jnp.bfloat16

D_MODEL = 1024
DEPTH = 1
GLA_WIDTH = 512
GLA_HEADS = 4
GLA_DK = 64
GLA_DV = 128
GLA_RANK = 16
GLA_TAU = 16.0
GMLP_WIDTH = 512
GMLP_HEADS = 4
GMLP_DH = 128
GMLP_CHUNK = 128
QK_WIDTH = GLA_HEADS * GLA_DK
N_EXPERTS = 64
TOP_K = 8
N_GROUPS = 8
GROUP_SIZE = N_EXPERTS // N_GROUPS
TOPK_GROUPS = 4
D_EXPERT = 256
D_SHARED = 256
ROUTE_SCALE = 2.5
ALPHA = (2.0 * DEPTH) ** 0.25

LANES = 128

C_Q = 0
C_K = C_Q + QK_WIDTH
C_VA = C_K + QK_WIDTH
C_G = C_VA + GLA_WIDTH
C_U = C_G + GLA_WIDTH
C_VB = C_U + GMLP_WIDTH
C_ALR = C_VB + GMLP_WIDTH
N_PROJ = C_ALR + LANES

CHUNK = 128
PROMPT_TILE = 512
PROMPT_SEQS = 2
SAMPLE_SEQS = 32
VMEM_LIMIT = 56 * 1024 * 1024

BLOCK = 256
SEG = 16
TILE_CHUNKS = 64
SUB_CHUNKS = 64
TILE_ROWS = TILE_CHUNKS * SEG
GATHER_ROWS = 512
BLOCK_ROWS_USED = BLOCK * TOP_K + N_EXPERTS * (SEG - 1)
BLOCK_ROWS = -(-(BLOCK_ROWS_USED + SEG) // GATHER_ROWS) * GATHER_ROWS
DUMP_BLOCKS = -(-(2 * TILE_ROWS) // BLOCK_ROWS)
ZERO_CHUNK = BLOCK_ROWS_USED // SEG
IN_BUFS = 3


def _dot(a, b):
    return jnp.dot(a, b, preferred_element_type=F32)


def _dot_nt(a, b):
    return lax.dot_general(a, b, (((1,), (1,)), ((), ())), preferred_element_type=F32)


def _dot_tn(a, b):
    return lax.dot_general(a, b, (((0,), (0,)), ((), ())), preferred_element_type=F32)


def _shr(x, d):
    assert d & (d - 1) == 0
    return lax.shift_right_logical(x, d.bit_length() - 1)


def _split_dot(m01, x):
    hi = x.astype(BF16)
    lo = (x - hi.astype(F32)).astype(BF16)
    return _dot(m01, hi) + _dot(m01, lo)


def _split_dot_r(x, m01):
    hi = x.astype(BF16)
    lo = (x - hi.astype(F32)).astype(BF16)
    return _dot(hi, m01) + _dot(lo, m01)


def _sigmoid(x):
    return 1.0 / (1.0 + jnp.exp(-x))


def _silu(x):
    return x * _sigmoid(x)


def _gelu(x):
    c = math.sqrt(2.0 / math.pi)
    return x * (0.5 * (1.0 + jnp.tanh(c * (x + 0.044715 * (x * x * x)))))


def _log_sigmoid(x):
    return -(jnp.maximum(-x, 0.0) + jnp.log1p(jnp.exp(-jnp.abs(x))))


def _layernorm(x, g, b, eps=1e-5):
    mu = jnp.mean(x, axis=-1, keepdims=True)
    xc = x - mu
    var = jnp.mean(xc * xc, axis=-1, keepdims=True)
    return xc * lax.rsqrt(var + eps) * g + b


def _rmsnorm(x, g, eps=1e-6):
    return x * lax.rsqrt(jnp.mean(x * x, axis=-1, keepdims=True) + eps) * g


def _gmlp_heads(u, vg, w_mix, bias, gmg_ref, merged_ref, rows):
    vgb = vg.astype(BF16)
    for h in range(GMLP_HEADS):
        cs = slice(h * GMLP_DH, (h + 1) * GMLP_DH)
        sgu = _dot(w_mix(h), vgb[:, cs]) + bias[:, cs]
        y = _rmsnorm(u[:, cs] * sgu, gmg_ref[h:h + 1, :])
        merged_ref[rows, GLA_WIDTH + h * GMLP_DH:GLA_WIDTH + (h + 1) * GMLP_DH] = y.astype(BF16)


def _out_proj_ln(x, merged_ref, wout_ref, ln1g_ref, ln1b_ref):
    m = _dot(merged_ref[...], wout_ref[...])
    return _layernorm(ALPHA * x + m, ln1g_ref[...], ln1b_ref[...])


def _mixer_prompt_kernel(x_ref, win_ref, waup_ref, ba_ref, glag_ref, lng_ref, lnb_ref, ws_ref, bsf_ref,
                         gmg_ref, wout_ref, ln1g_ref, ln1b_ref,
                         h_ref, state_ref,
                         z_ref, merged_ref, *st_refs):
    t = pl.program_id(1)

    @pl.when(t == 0)
    def _():
        for st_ref in st_refs:
            st_ref[...] = jnp.zeros_like(st_ref)

    x = x_ref[...].reshape(PROMPT_SEQS * PROMPT_TILE, D_MODEL)
    z_ref[...] = _dot(x.astype(BF16), win_ref[...])

    row_i = lax.broadcasted_iota(jnp.int32, (CHUNK, CHUNK), 0)
    col_i = lax.broadcasted_iota(jnp.int32, (CHUNK, CHUNK), 1)
    causal = row_i >= col_i
    tri = jnp.where(causal, 1.0, 0.0).astype(BF16)
    lane = lax.broadcasted_iota(jnp.int32, (1, LANES), 1)
    head_lanes = [lane < GLA_DK, lane >= GLA_DK]
    mid = CHUNK // 2 - 1

    for c in range(PROMPT_SEQS * PROMPT_TILE // CHUNK):
        rows = slice(c * CHUNK, (c + 1) * CHUNK)
        st_ref = st_refs[c // (PROMPT_TILE // CHUNK)]
        a_pre = _dot(z_ref[rows, C_ALR:C_ALR + LANES].astype(BF16), waup_ref[...]) + ba_ref[...]
        log_a = _log_sigmoid(a_pre) * (1.0 / GLA_TAU)
        b = _split_dot(tri, log_a)
        b_mid = b[mid:mid + 1, :]
        b_last = b[CHUNK - 1:CHUNK, :]
        q = z_ref[rows, C_Q:C_Q + QK_WIDTH] * (GLA_DK ** -0.5)
        k = z_ref[rows, C_K:C_K + QK_WIDTH]
        q_in = (q * jnp.exp(b - b_mid)).astype(BF16)
        k_in = (k * jnp.exp(b_mid - b)).astype(BF16)
        q_st = (q * jnp.exp(b)).astype(BF16)
        k_st = (k * jnp.exp(b_last - b)).astype(BF16)
        d_last = jnp.exp(b_last)
        va = z_ref[rows, C_VA:C_VA + GLA_WIDTH].astype(BF16)
        for h in range(GLA_HEADS):
            ps = slice((h // 2) * LANES, (h // 2 + 1) * LANES)
            vs = slice(h * GLA_DV, (h + 1) * GLA_DV)
            hm = head_lanes[h % 2]
            zero = jnp.zeros((), BF16)
            a = _dot_nt(jnp.where(hm, q_in[:, ps], zero), k_in[:, ps])
            a = jnp.where(causal, a, 0.0).astype(BF16)
            st = st_ref[h]
            o = _dot(a, va[:, vs]) + _dot_nt(jnp.where(hm, q_st[:, ps], zero), st.astype(BF16))
            upd = _dot_tn(va[:, vs], jnp.where(hm, k_st[:, ps], zero))
            st_ref[h] = st * d_last[:, ps] + upd
            gate = z_ref[rows, C_G + h * GLA_DV:C_G + (h + 1) * GLA_DV]
            o = _rmsnorm(o, glag_ref[h:h + 1, :]) * _silu(gate)
            merged_ref[rows, vs] = o.astype(BF16)
        u = _gelu(z_ref[rows, C_U:C_U + GMLP_WIDTH])
        vg = _layernorm(_gelu(z_ref[rows, C_VB:C_VB + GMLP_WIDTH]), lng_ref[...], lnb_ref[...])
        _gmlp_heads(u, vg, lambda h: jnp.where(causal, ws_ref[h], 0.0).astype(BF16), bsf_ref[...],
                    gmg_ref, merged_ref, rows)

    h_ref[...] = _out_proj_ln(x, merged_ref, wout_ref, ln1g_ref, ln1b_ref).reshape(PROMPT_SEQS, PROMPT_TILE, D_MODEL)

    @pl.when(t == pl.num_programs(1) - 1)
    def _():
        for s, st_ref in enumerate(st_refs):
            for h in range(GLA_HEADS):
                lo = (h % 2) * GLA_DK
                state_ref[s, h] = st_ref[h].T[lo:lo + GLA_DK, :]


def _mixer_sample_kernel(seq_len, x_ref, s0_ref, win_ref, wkt_ref, walrt_ref, waup_ref, waupt_ref, ba_ref, bac_ref,
                         glag_ref, lng_ref, lnb_ref, wst_ref, bsf_ref, gmg_ref, wout_ref, ln1g_ref, ln1b_ref,
                         h_ref, snew_ref, vg_ref,
                         merged_ref):
    n = SAMPLE_SEQS * seq_len
    x = x_ref[...]
    xb = x.astype(BF16)
    z = _dot(xb, win_ref[...])

    ti = lax.broadcasted_iota(jnp.int32, (n, n), 0)
    tj = lax.broadcasted_iota(jnp.int32, (n, n), 1)
    same = _shr(ti, seq_len) == _shr(tj, seq_len)
    causal = jnp.logical_and(same, ti >= tj)
    tri = jnp.where(causal, 1.0, 0.0).astype(BF16)
    tri_t = jnp.where(jnp.logical_and(same, ti <= tj), 1.0, 0.0).astype(BF16)
    same01 = jnp.where(same, 1.0, 0.0).astype(BF16)

    a_pre = _dot(z[:, C_ALR:C_ALR + LANES].astype(BF16), waup_ref[...]) + ba_ref[...]
    log_a = _log_sigmoid(a_pre) * (1.0 / GLA_TAU)
    b = _split_dot(tri, log_a)
    q = z[:, C_Q:C_Q + QK_WIDTH] * (GLA_DK ** -0.5)
    k = z[:, C_K:C_K + QK_WIDTH]
    q_in = (q * jnp.exp(b)).astype(BF16)
    k_in = (k * jnp.exp(-b)).astype(BF16)
    va = z[:, C_VA:C_VA + GLA_WIDTH].astype(BF16)

    k_t = _dot_nt(wkt_ref[...], xb)
    alr_t = _dot_nt(walrt_ref[...], xb)
    a_pre_t = _dot(waupt_ref[...], alr_t.astype(BF16)) + bac_ref[...]
    log_a_t = _log_sigmoid(a_pre_t) * (1.0 / GLA_TAU)
    b_t = _split_dot_r(log_a_t, tri_t)
    tot_t = _split_dot_r(log_a_t, same01)
    k_st_t = k_t * jnp.exp(tot_t - b_t)
    d_t = jnp.exp(tot_t)

    lane = lax.broadcasted_iota(jnp.int32, (1, LANES), 1)
    head_lanes = [lane < GLA_DK, lane >= GLA_DK]
    nrow = SAMPLE_SEQS * GLA_DK
    r_seq = _shr(lax.broadcasted_iota(jnp.int32, (nrow, n), 0), GLA_DK)
    c_seq = _shr(lax.broadcasted_iota(jnp.int32, (nrow, n), 1), seq_len)
    c_first = (lax.broadcasted_iota(jnp.int32, (nrow, n), 1) & (seq_len - 1)) == 0
    blk = r_seq == c_seq
    blk_first = jnp.logical_and(blk, c_first)
    q_seq = _shr(lax.broadcasted_iota(jnp.int32, (n, nrow), 0), seq_len)
    q_col = _shr(lax.broadcasted_iota(jnp.int32, (n, nrow), 1), GLA_DK)
    blk_q = q_seq == q_col

    for h in range(GLA_HEADS):
        ps = slice((h // 2) * LANES, (h // 2 + 1) * LANES)
        ds_ = slice(h * GLA_DK, (h + 1) * GLA_DK)
        vs = slice(h * GLA_DV, (h + 1) * GLA_DV)
        hm = head_lanes[h % 2]
        zero = jnp.zeros((), BF16)
        a = _dot_nt(jnp.where(hm, q_in[:, ps], zero), k_in[:, ps])
        a = jnp.where(causal, a, 0.0).astype(BF16)
        s0 = s0_ref[:, h].reshape(nrow, GLA_DV)
        q_h = q_in[:, ds_]
        q_bd = jnp.where(blk_q, jnp.concatenate([q_h] * SAMPLE_SEQS, axis=1), zero)
        o = _dot(a, va[:, vs]) + _dot(q_bd, s0.astype(BF16))
        k_bd = jnp.where(blk, jnp.concatenate([k_st_t[ds_, :]] * SAMPLE_SEQS, axis=0), 0.0).astype(BF16)
        upd = _dot(k_bd, va[:, vs])
        d_bd = jnp.where(blk_first, jnp.concatenate([d_t[ds_, :]] * SAMPLE_SEQS, axis=0), 0.0)
        d_col = jnp.sum(d_bd, axis=1, keepdims=True)
        snew_ref[:, h] = (s0 * d_col + upd).reshape(SAMPLE_SEQS, GLA_DK, GLA_DV)
        gate = z[:, C_G + h * GLA_DV:C_G + (h + 1) * GLA_DV]
        o = _rmsnorm(o, glag_ref[h:h + 1, :]) * _silu(gate)
        merged_ref[:, vs] = o.astype(BF16)

    u = _gelu(z[:, C_U:C_U + GMLP_WIDTH])
    vg = _layernorm(_gelu(z[:, C_VB:C_VB + GMLP_WIDTH]), lng_ref[...], lnb_ref[...])
    vg_ref[...] = vg
    _gmlp_heads(u, vg, lambda h: jnp.where(causal, wst_ref[h], 0.0).astype(BF16), bsf_ref[...],
                gmg_ref, merged_ref, slice(None))
    h_ref[...] = _out_proj_ln(x, merged_ref, wout_ref, ln1g_ref, ln1b_ref)


def _route(hb, wrt_ref, rbias_ref):
    tm = hb.shape[0]
    s = _sigmoid(_dot_nt(wrt_ref[...], hb))
    sb = s + rbias_ref[...]
    neg = jnp.float32(-jnp.inf)
    sub = lax.broadcasted_iota(jnp.int32, (GROUP_SIZE, tm), 0)
    gscore = []
    for g in range(N_GROUPS):
        blk = sb[g * GROUP_SIZE:(g + 1) * GROUP_SIZE, :]
        m1 = jnp.max(blk, axis=0, keepdims=True)
        i1 = jnp.min(jnp.where(blk == m1, sub, GROUP_SIZE), axis=0, keepdims=True)
        m2 = jnp.max(jnp.where(sub == i1, neg, blk), axis=0, keepdims=True)
        gscore.append(m1 + m2)
    gsel = []
    for g in range(N_GROUPS):
        rank = jnp.zeros((1, tm), jnp.int32)
        for o in range(N_GROUPS):
            if o == g:
                continue
            ahead = (gscore[o] >= gscore[g]) if o < g else (gscore[o] > gscore[g])
            rank = rank + ahead.astype(jnp.int32)
        gsel.append(rank < TOPK_GROUPS)
    v = jnp.concatenate(
        [jnp.where(gsel[g], sb[g * GROUP_SIZE:(g + 1) * GROUP_SIZE, :], neg) for g in range(N_GROUPS)], axis=0)
    eidx = lax.broadcasted_iota(jnp.int32, (N_EXPERTS, tm), 0)
    sel = jnp.zeros((N_EXPERTS, tm), F32)
    for _ in range(TOP_K):
        m = jnp.max(v, axis=0, keepdims=True)
        first = jnp.min(jnp.where(v == m, eidx, N_EXPERTS), axis=0, keepdims=True)
        hit = eidx == first
        sel = jnp.where(hit, 1.0, sel)
        v = jnp.where(hit, neg, v)
    wsel = s * sel
    return wsel / jnp.sum(wsel, axis=0, keepdims=True) * ROUTE_SCALE, sel


def _per_row(chunk_vals):
    n, t = chunk_vals.shape
    return jnp.broadcast_to(chunk_vals[:, None, :], (n, SEG, t)).reshape(n * SEG, t)


def _row_lookup(seg_b, start_b, rank, tables):
    n_chunks = BLOCK_ROWS // SEG
    ei = lax.broadcasted_iota(jnp.int32, (N_EXPERTS, LANES), 0)
    ej = lax.broadcasted_iota(jnp.int32, (N_EXPERTS, LANES), 1)
    seg16 = (seg_b * (1.0 / SEG)).astype(BF16)
    start_row = _dot_tn(seg16, jnp.where(ei < ej, 1.0, 0.0).astype(BF16))[0:1, :]
    seg_row = _dot_tn(seg16, jnp.where(ei == ej, 1.0, 0.0).astype(BF16))[0:1, :]
    c = lax.broadcasted_iota(jnp.int32, (n_chunks, LANES), 0).astype(F32)
    owner = jnp.where(c >= start_row, jnp.where(c < start_row + seg_row, 1.0, 0.0), 0.0).astype(BF16)

    def lookup(tab):
        padded = jnp.concatenate([tab.astype(BF16), jnp.zeros((LANES - N_EXPERTS, tab.shape[1]), BF16)], axis=0)
        return _dot(owner, padded)

    first = SEG * (c - lookup(start_b * (1.0 / SEG)))[:, 0:1]
    rel = lookup(rank) - first
    sub = lax.broadcasted_iota(jnp.int32, (BLOCK_ROWS, rank.shape[1]), 0) & (SEG - 1)
    hits = _per_row(rel) == sub.astype(F32)
    return hits, [_per_row(lookup(tab)) for tab in tables]


def _dispatch_kernel(n_prompt_blocks, n_blocks, hp_ref, hs_ref, wrt_ref, rbias_ref, x_ref, rank_ref, comb_ref,
                     seg_ref, start_ref):
    i = pl.program_id(0)

    @pl.when(i < n_blocks)
    def _():
        _dispatch_block(i < n_prompt_blocks, hp_ref, hs_ref, wrt_ref, rbias_ref, x_ref, rank_ref, comb_ref, seg_ref,
                        start_ref)

    @pl.when(i >= n_blocks)
    def _():
        for ref in (x_ref, rank_ref, comb_ref, seg_ref, start_ref):
            ref[...] = jnp.zeros_like(ref)


def _dispatch_block(is_prompt, hp_ref, hs_ref, wrt_ref, rbias_ref, x_ref, rank_ref, comb_ref, seg_ref, start_ref):
    tm = BLOCK
    hb = jnp.where(is_prompt, hp_ref[...], hs_ref[...]).astype(BF16)
    comb, sel = _route(hb, wrt_ref, rbias_ref)
    ti = lax.broadcasted_iota(jnp.int32, (tm, tm), 0)
    tj = lax.broadcasted_iota(jnp.int32, (tm, tm), 1)
    before = jnp.where(ti < tj, 1.0, 0.0).astype(BF16)
    rank = _dot(sel.astype(BF16), before)
    rank = jnp.where(sel > 0.0, rank, -1.0)
    cnt = jnp.sum(sel, axis=1, keepdims=True)
    seg = jnp.floor((cnt + (SEG - 1.0)) * (1.0 / SEG)) * SEG
    ei = lax.broadcasted_iota(jnp.int32, (N_EXPERTS, N_EXPERTS), 0)
    ej = lax.broadcasted_iota(jnp.int32, (N_EXPERTS, N_EXPERTS), 1)
    below = jnp.where(ej < ei, 1.0, 0.0).astype(BF16)
    seg_b = jnp.broadcast_to(seg, (N_EXPERTS, LANES))
    start_b = _dot(below, seg_b.astype(BF16))
    hits, _ = _row_lookup(seg_b, start_b, rank, [])
    gather = jnp.where(hits, 1.0, 0.0).astype(BF16)
    for c in range(BLOCK_ROWS // GATHER_ROWS):
        rs = slice(c * GATHER_ROWS, (c + 1) * GATHER_ROWS)
        x_ref[rs, :] = _dot(gather[rs, :], hb).astype(BF16)
    rank_ref[0] = rank
    comb_ref[0] = comb
    seg_ref[0] = seg_b
    start_ref[0] = start_b


def _ffn_kernel(dump_base, nt_max, tile_expert_ref, tile_in_expert_ref, n_chunks_ref, n_tiles_ref,
                tab0_ref, tab1_ref, tab2_ref, x_hbm, wg_ref, wu_ref, wd_ref,
                o_hbm, ibuf, obuf, wgb, wub, wdb, in_sem, out_sem):
    i = pl.program_id(0)
    n_tiles = n_tiles_ref[0]
    tabs = (tab0_ref, tab1_ref, tab2_ref)

    def start_in(ahead, slot):
        exists = i + ahead < n_tiles
        first = jnp.where(exists, tile_in_expert_ref[jnp.minimum(i + ahead, nt_max - 1)] * TILE_CHUNKS, 0)
        srcs = [jnp.where(exists, tabs[ahead][0, 0, first + j], ZERO_CHUNK) for j in range(TILE_CHUNKS)]
        for j in range(TILE_CHUNKS):
            pltpu.make_async_copy(x_hbm.at[srcs[j]], ibuf.at[slot, j], in_sem.at[slot]).start()

    def start_out(slot):
        e = tile_expert_ref[i]
        first = tile_in_expert_ref[i] * TILE_CHUNKS
        n_real = n_chunks_ref[e] - first
        dsts = [jnp.where(j < n_real, tab0_ref[0, 0, first + j], dump_base + slot * TILE_CHUNKS + j)
                for j in range(TILE_CHUNKS)]
        for j in range(TILE_CHUNKS):
            pltpu.make_async_copy(obuf.at[slot, j], o_hbm.at[dsts[j]], out_sem.at[slot]).start()

    def in_wait(slot):
        for j in range(TILE_CHUNKS):
            pltpu.make_async_copy(x_hbm.at[0], ibuf.at[slot, j], in_sem.at[slot]).wait()

    def out_wait(slot):
        for j in range(TILE_CHUNKS):
            pltpu.make_async_copy(obuf.at[slot, j], o_hbm.at[0], out_sem.at[slot]).wait()

    @pl.when(i < n_tiles)
    def _():
        islot = lax.rem(i, IN_BUFS)
        nslot = jnp.where(islot == 0, IN_BUFS - 1, islot - 1)
        oslot = lax.rem(i, 2)

        @pl.when(i == 0)
        def _():
            start_in(0, 0)
            start_in(1, 1)

        @pl.when(i >= 2)
        def _():
            out_wait(oslot)

        in_wait(islot)

        @pl.when(tile_in_expert_ref[i] == 0)
        def _():
            wgb[...] = wg_ref[0].astype(BF16)
            wub[...] = wu_ref[0].astype(BF16)
            wdb[...] = wd_ref[0].astype(BF16)

        start_in(2, nslot)
        for s in range(TILE_CHUNKS // SUB_CHUNKS):
            cs = slice(s * SUB_CHUNKS, (s + 1) * SUB_CHUNKS)
            x = ibuf[islot, cs].reshape(SUB_CHUNKS * SEG, D_MODEL)
            hid = _silu(_dot(x, wgb[...])) * _dot(x, wub[...])
            obuf[oslot, cs] = _dot(hid.astype(BF16), wdb[...]).astype(BF16).reshape(SUB_CHUNKS, SEG, D_MODEL)
        start_out(oslot)

        @pl.when(i == n_tiles - 1)
        def _():
            in_wait(lax.rem(i + 1, IN_BUFS))
            in_wait(nslot)
            out_wait(oslot)

            @pl.when(i >= 1)
            def _():
                out_wait(1 - oslot)


def _combine_kernel(h_ref, o_ref, rank_ref, comb_ref, seg_ref, start_ref, wsg_ref, wsu_ref, wsd_ref, ln2g_ref,
                    ln2b_ref, y_ref):
    h = h_ref[...]
    hb = h.astype(BF16)
    hits, (comb_rows,) = _row_lookup(seg_ref[0], start_ref[0], rank_ref[0], [comb_ref[0]])
    scatter = jnp.where(hits, comb_rows, 0.0).astype(BF16)
    routed = _dot_tn(scatter, o_ref[...])
    shared = _dot((_silu(_dot(hb, wsg_ref[...])) * _dot(hb, wsu_ref[...])).astype(BF16), wsd_ref[...])
    y_ref[...] = _layernorm(ALPHA * h + (routed + shared), ln2g_ref[...], ln2b_ref[...])


def _full(shape):
    return pl.BlockSpec(shape, lambda *_: (0,) * len(shape))


def _const(shape):
    return pl.BlockSpec(shape, lambda *_: (0,) * len(shape), pipeline_mode=pl.Buffered(1))


def _mixer_prompt(x, wts):
    bsz, seq, _ = x.shape
    n_t = seq // PROMPT_TILE
    weights = [wts[k] for k in ("win", "waup", "ba", "glag", "lng", "lnb", "ws", "bsf_prompt", "gmg", "wout",
                                "ln1g", "ln1b")]
    return pl.pallas_call(
        _mixer_prompt_kernel,
        grid=(bsz // PROMPT_SEQS, n_t),
        in_specs=[pl.BlockSpec((PROMPT_SEQS, PROMPT_TILE, D_MODEL), lambda b, t: (b, t, 0))]
        + [_const(w.shape) for w in weights],
        out_specs=[pl.BlockSpec((PROMPT_SEQS, PROMPT_TILE, D_MODEL), lambda b, t: (b, t, 0)),
                   pl.BlockSpec((PROMPT_SEQS, GLA_HEADS, GLA_DK, GLA_DV), lambda b, t: (b, 0, 0, 0))],
        out_shape=[jax.ShapeDtypeStruct((bsz, seq, D_MODEL), F32),
                   jax.ShapeDtypeStruct((bsz, GLA_HEADS, GLA_DK, GLA_DV), F32)],
        scratch_shapes=[pltpu.VMEM((PROMPT_SEQS * PROMPT_TILE, N_PROJ), F32),
                        pltpu.VMEM((PROMPT_SEQS * PROMPT_TILE, D_MODEL), BF16)]
        + [pltpu.VMEM((GLA_HEADS, GLA_DV, LANES), F32)] * PROMPT_SEQS,
        compiler_params=pltpu.CompilerParams(dimension_semantics=("arbitrary", "arbitrary"),
                                             vmem_limit_bytes=VMEM_LIMIT),
        name="mixer_prompt",
    )(x, *weights)


def _mixer_sample(x, s0, wts):
    bsz, seq_len, _ = x.shape
    n = SAMPLE_SEQS * seq_len
    x2 = x.reshape(bsz * seq_len, D_MODEL)
    weights = [wts[k] for k in ("win", "wkt", "walrt", "waup", "waupt", "ba", "bac", "glag", "lng", "lnb",
                                "ws_sample", "bsf_sample", "gmg", "wout", "ln1g", "ln1b")]
    state_spec = pl.BlockSpec((SAMPLE_SEQS, GLA_HEADS, GLA_DK, GLA_DV), lambda i: (i, 0, 0, 0))
    h, s_new, vg = pl.pallas_call(
        functools.partial(_mixer_sample_kernel, seq_len),
        grid=(bsz // SAMPLE_SEQS,),
        in_specs=[pl.BlockSpec((n, D_MODEL), lambda i: (i, 0)), state_spec] + [_full(w.shape) for w in weights],
        out_specs=[pl.BlockSpec((n, D_MODEL), lambda i: (i, 0)), state_spec,
                   pl.BlockSpec((n, GMLP_WIDTH), lambda i: (i, 0))],
        out_shape=[jax.ShapeDtypeStruct((bsz * seq_len, D_MODEL), F32),
                   jax.ShapeDtypeStruct(s0.shape, F32),
                   jax.ShapeDtypeStruct((bsz * seq_len, GMLP_WIDTH), F32)],
        scratch_shapes=[pltpu.VMEM((n, D_MODEL), BF16)],
        compiler_params=pltpu.CompilerParams(dimension_semantics=("arbitrary",), vmem_limit_bytes=VMEM_LIMIT),
        name="mixer_sample",
    )(x2, s0, *weights)
    return h, s_new, vg.reshape(bsz, seq_len, GMLP_WIDTH)


def _dispatch(h_p, h_s, wts):
    npb, nsb = h_p.shape[0] // BLOCK, h_s.shape[0] // BLOCK
    nb_real = npb + nsb
    nb = nb_real + DUMP_BLOCKS
    tok_spec = pl.BlockSpec((1, N_EXPERTS, BLOCK), lambda i: (i, 0, 0))
    run_spec = pl.BlockSpec((1, N_EXPERTS, LANES), lambda i: (i, 0, 0))
    return pl.pallas_call(
        functools.partial(_dispatch_kernel, npb, nb_real),
        grid=(nb,),
        in_specs=[pl.BlockSpec((BLOCK, D_MODEL), lambda i: (jnp.minimum(i, npb - 1), 0)),
                  pl.BlockSpec((BLOCK, D_MODEL), lambda i: (jnp.clip(i - npb, 0, nsb - 1), 0)),
                  _full(wts["wrt"].shape), _full(wts["rbias"].shape)],
        out_specs=[pl.BlockSpec((BLOCK_ROWS, D_MODEL), lambda i: (i, 0)), tok_spec, tok_spec, run_spec, run_spec],
        out_shape=[jax.ShapeDtypeStruct((nb * BLOCK_ROWS, D_MODEL), BF16),
                   jax.ShapeDtypeStruct((nb, N_EXPERTS, BLOCK), F32),
                   jax.ShapeDtypeStruct((nb, N_EXPERTS, BLOCK), F32),
                   jax.ShapeDtypeStruct((nb, N_EXPERTS, LANES), F32),
                   jax.ShapeDtypeStruct((nb, N_EXPERTS, LANES), F32)],
        compiler_params=pltpu.CompilerParams(dimension_semantics=("arbitrary",), vmem_limit_bytes=VMEM_LIMIT),
        name="moe_dispatch",
    )(h_p, h_s, wts["wrt"], wts["rbias"])


def _max_tiles(nb):
    return (nb * (BLOCK_ROWS_USED // SEG) + N_EXPERTS * (TILE_CHUNKS - 1)) // TILE_CHUNKS + 1


def _max_expert_chunks(nb):
    most = nb * (BLOCK // SEG) + TILE_CHUNKS - 1
    return -(-most // LANES) * LANES


def _plan(seg_lanes, nb):
    nt_max = _max_tiles(nb)
    max_chunks = _max_expert_chunks(nb)
    seg = seg_lanes[:nb, :, 0].astype(jnp.int32)
    start = jnp.cumsum(seg, axis=1) - seg
    nch = seg // SEG
    ends = jnp.cumsum(nch, axis=0)
    n_chunks = ends[-1]
    f = jnp.arange(nb, dtype=jnp.int32)[:, None] * (BLOCK_ROWS // SEG) + start // SEG - (ends - nch)
    q = jnp.arange(max_chunks, dtype=jnp.int32)
    passed = (ends[None, :-1, :] <= q[:, None, None]).astype(jnp.int32)
    chunk = q[:, None] + f[0][None, :] + jnp.sum(passed * (f[1:] - f[:-1])[None], axis=1)
    table = jnp.where(q[:, None] < n_chunks[None, :], chunk, ZERO_CHUNK).T
    n_tiles_e = -(-n_chunks // TILE_CHUNKS)
    tile_ends = jnp.cumsum(n_tiles_e)
    t = jnp.arange(nt_max, dtype=jnp.int32)
    done = (tile_ends[None, :] <= t[:, None]).astype(jnp.int32)
    tile_expert = jnp.minimum(jnp.sum(done, axis=1), N_EXPERTS - 1)
    tile_in_expert = t - jnp.sum(done * n_tiles_e[None, :], axis=1)
    return (table.astype(jnp.int32).reshape(N_EXPERTS, 1, max_chunks), tile_expert.astype(jnp.int32),
            tile_in_expert.astype(jnp.int32), n_chunks.astype(jnp.int32), tile_ends[-1:].astype(jnp.int32))


def _ffn(x_rows, nb, table, tile_expert, tile_in_expert, n_chunks, n_tiles, w_gate, w_up, w_down):
    nt_max = tile_expert.shape[0]
    max_chunks = table.shape[-1]
    tab_spec = lambda ahead: pl.BlockSpec(
        (1, 1, max_chunks), lambda i, te, *_: (te[jnp.minimum(i + ahead, nt_max - 1)], 0, 0),
        memory_space=pltpu.SMEM)
    w_spec = lambda shape: pl.BlockSpec((1,) + shape, lambda i, te, *_: (te[i], 0, 0))
    assert x_rows.shape[0] - nb * BLOCK_ROWS >= 2 * TILE_ROWS
    x_chunks = x_rows.reshape(x_rows.shape[0] // SEG, SEG, D_MODEL)
    return pl.pallas_call(
        functools.partial(_ffn_kernel, nb * BLOCK_ROWS // SEG, nt_max),
        grid_spec=pltpu.PrefetchScalarGridSpec(
            num_scalar_prefetch=4,
            grid=(nt_max,),
            in_specs=[tab_spec(0), tab_spec(1), tab_spec(2), pl.BlockSpec(memory_space=pl.ANY),
                      w_spec((D_MODEL, D_EXPERT)), w_spec((D_MODEL, D_EXPERT)), w_spec((D_EXPERT, D_MODEL))],
            out_specs=pl.BlockSpec(memory_space=pl.ANY),
            scratch_shapes=[pltpu.VMEM((IN_BUFS, TILE_CHUNKS, SEG, D_MODEL), BF16),
                            pltpu.VMEM((2, TILE_CHUNKS, SEG, D_MODEL), BF16),
                            pltpu.VMEM((D_MODEL, D_EXPERT), BF16),
                            pltpu.VMEM((D_MODEL, D_EXPERT), BF16),
                            pltpu.VMEM((D_EXPERT, D_MODEL), BF16),
                            pltpu.SemaphoreType.DMA((IN_BUFS,)),
                            pltpu.SemaphoreType.DMA((2,))]),
        out_shape=jax.ShapeDtypeStruct(x_chunks.shape, BF16),
        input_output_aliases={7: 0},
        compiler_params=pltpu.CompilerParams(dimension_semantics=("arbitrary",), vmem_limit_bytes=VMEM_LIMIT),
        name="moe_ffn",
    )(tile_expert, tile_in_expert, n_chunks, n_tiles, table, table, table, x_chunks, w_gate, w_up, w_down
      ).reshape(x_rows.shape)


def _combine(h, o_rows, routing, block_off, wts):
    nblk = h.shape[0] // BLOCK
    weights = [wts[k] for k in ("wsg", "wsu", "wsd", "ln2g", "ln2b")]
    tok_spec = pl.BlockSpec((1, N_EXPERTS, BLOCK), lambda i: (i + block_off, 0, 0))
    run_spec = pl.BlockSpec((1, N_EXPERTS, LANES), lambda i: (i + block_off, 0, 0))
    return pl.pallas_call(
        _combine_kernel,
        grid=(nblk,),
        in_specs=[pl.BlockSpec((BLOCK, D_MODEL), lambda i: (i, 0)),
                  pl.BlockSpec((BLOCK_ROWS, D_MODEL), lambda i: (i + block_off, 0)),
                  tok_spec, tok_spec, run_spec, run_spec] + [_full(w.shape) for w in weights],
        out_specs=pl.BlockSpec((BLOCK, D_MODEL), lambda i: (i, 0)),
        out_shape=jax.ShapeDtypeStruct(h.shape, F32),
        compiler_params=pltpu.CompilerParams(dimension_semantics=("arbitrary",), vmem_limit_bytes=VMEM_LIMIT),
        name="moe_combine",
    )(h, o_rows, *routing, *weights)


def _moe(h_p, h_s, w_gate, w_up, w_down, wts):
    nb = (h_p.shape[0] + h_s.shape[0]) // BLOCK
    x_rows, rank, comb, seg_lanes, start_lanes = _dispatch(h_p, h_s, wts)
    o_rows = _ffn(x_rows, nb, *_plan(seg_lanes, nb), w_gate, w_up, w_down)
    routing = (rank, comb, seg_lanes, start_lanes)
    y_p = _combine(h_p, o_rows, routing, 0, wts)
    y_s = _combine(h_s, o_rows, routing, h_p.shape[0] // BLOCK, wts)
    return y_p, y_s


def _prep_weights(seq_len, w_in, w_a_up, b_a, gla_norm_g, gmlp_ln_g, gmlp_ln_b, w_s, b_s, gmlp_norm_g, w_out,
                  ln1_g, ln1_b, w_router, router_bias, ws_gate, ws_up, ws_down, ln2_g, ln2_b):
    o1 = QK_WIDTH
    o2 = o1 + QK_WIDTH
    o3 = o2 + GLA_WIDTH
    o4 = o3 + GLA_WIDTH
    o5 = o4 + GLA_RANK
    o6 = o5 + GMLP_WIDTH
    wq, wk, wva, wg_, walr, wu_, wvb = jnp.split(w_in, [o1, o2, o3, o4, o5, o6], axis=-1)
    walr_p = jnp.pad(walr, ((0, 0), (0, LANES - GLA_RANK)))
    waup_p = jnp.pad(w_a_up, ((0, LANES - GLA_RANK), (0, 0)))
    row = lambda a: a.reshape(1, -1)
    reps = (SAMPLE_SEQS * seq_len) // seq_len
    ws_small = w_s[:, :seq_len, :seq_len]
    return {
        "win": jnp.concatenate([wq, wk, wva, wg_, wu_, wvb, walr_p], axis=-1).astype(BF16),
        "wkt": wk.T.astype(BF16),
        "walrt": walr_p.T.astype(BF16),
        "waup": waup_p.astype(BF16),
        "waupt": waup_p.T.astype(BF16),
        "ba": row(b_a), "bac": b_a.reshape(-1, 1),
        "glag": gla_norm_g, "lng": row(gmlp_ln_g), "lnb": row(gmlp_ln_b),
        "ws": w_s,
        "ws_sample": jnp.tile(ws_small, (1, reps, reps)),
        "bsf_prompt": jnp.repeat(b_s[:, :GMLP_CHUNK].T, GMLP_DH, axis=1),
        "bsf_sample": jnp.tile(jnp.repeat(b_s[:, :seq_len].T, GMLP_DH, axis=1), (reps, 1)),
        "gmg": gmlp_norm_g,
        "wout": w_out.astype(BF16),
        "ln1g": row(ln1_g), "ln1b": row(ln1_b),
        "wrt": w_router.T.astype(BF16), "rbias": router_bias.reshape(-1, 1),
        "wsg": ws_gate.astype(BF16), "wsu": ws_up.astype(BF16), "wsd": ws_down.astype(BF16),
        "ln2g": row(ln2_g), "ln2b": row(ln2_b),
    }


def kernel(x_prompt, x_sample, state_gla, w_in, w_a_up, b_a, gla_norm_g, gmlp_ln_g, gmlp_ln_b, w_s, b_s,
           gmlp_norm_g, w_out, ln1_g, ln1_b, w_router, router_bias, w_gate, w_up, w_down, ws_gate, ws_up,
           ws_down, ln2_g, ln2_b):
    assert x_prompt.shape[1] % PROMPT_TILE == 0 and x_sample.shape[0] % SAMPLE_SEQS == 0
    assert x_prompt.shape[0] % PROMPT_SEQS == 0
    assert x_sample.shape[1] <= GMLP_CHUNK and w_in.shape[0] == DEPTH
    assert (x_sample.shape[0] * x_sample.shape[1]) % BLOCK == 0 and PROMPT_TILE % BLOCK == 0
    bsz, seq, _ = x_prompt.shape
    dbsz, dseq, _ = x_sample.shape
    hp, hs = x_prompt, x_sample
    gla_p, gla_s, v_s = [], [], []
    for l in range(DEPTH):
        wts = _prep_weights(dseq, w_in[l], w_a_up[l], b_a[l], gla_norm_g[l], gmlp_ln_g[l], gmlp_ln_b[l], w_s[l],
                            b_s[l], gmlp_norm_g[l], w_out[l], ln1_g[l], ln1_b[l], w_router[l], router_bias[l],
                            ws_gate[l], ws_up[l], ws_down[l], ln2_g[l], ln2_b[l])
        h_p, sp = _mixer_prompt(hp, wts)
        h_s, ss, vrows = _mixer_sample(hs, state_gla[l], wts)
        y_p, y_s = _moe(h_p.reshape(bsz * seq, D_MODEL), h_s, w_gate[l], w_up[l], w_down[l], wts)
        hp = y_p.reshape(bsz, seq, D_MODEL)
        hs = y_s.reshape(dbsz, dseq, D_MODEL)
        gla_p.append(sp)
        gla_s.append(ss)
        v_s.append(vrows)
    return (hp, hs, jnp.stack(gla_p), jnp.stack(gla_s), jnp.stack(v_s))
```

```python
import functools
import math

import jax
import jax.numpy as jnp
from jax import lax
from jax.experimental import pallas as pl
from jax.experimental.pallas import tpu as pltpu

F32 = jnp.float32
BF16 = jnp.bfloat16

D_MODEL = 1024
DEPTH = 1
GLA_WIDTH = 512
GLA_HEADS = 4
GLA_DK = 64
GLA_DV = 128
GLA_RANK = 16
GLA_TAU = 16.0
GMLP_WIDTH = 512
GMLP_HEADS = 4
GMLP_DH = 128
GMLP_CHUNK = 128
QK_WIDTH = GLA_HEADS * GLA_DK
N_EXPERTS = 64
TOP_K = 8
N_GROUPS = 8
GROUP_SIZE = N_EXPERTS // N_GROUPS
TOPK_GROUPS = 4
D_EXPERT = 256
D_SHARED = 256
ROUTE_SCALE = 2.5
ALPHA = (2.0 * DEPTH) ** 0.25

LANES = 128

C_Q = 0
C_K = C_Q + QK_WIDTH
C_VA = C_K + QK_WIDTH
C_G = C_VA + GLA_WIDTH
C_U = C_G + GLA_WIDTH
C_VB = C_U + GMLP_WIDTH
C_ALR = C_VB + GMLP_WIDTH
N_PROJ = C_ALR + LANES

CHUNK = 128
PROMPT_TILE = 512
PROMPT_SEQS = 2
SAMPLE_SEQS = 32
VMEM_LIMIT = 56 * 1024 * 1024

BLOCK = 256
SEG = 16
TILE_CHUNKS = 64
SUB_CHUNKS = 64
TILE_ROWS = TILE_CHUNKS * SEG
GATHER_ROWS = 512
BLOCK_ROWS_USED = BLOCK * TOP_K + N_EXPERTS * (SEG - 1)
BLOCK_ROWS = -(-(BLOCK_ROWS_USED + SEG) // GATHER_ROWS) * GATHER_ROWS
DUMP_BLOCKS = -(-(2 * TILE_ROWS) // BLOCK_ROWS)
ZERO_CHUNK = BLOCK_ROWS_USED // SEG
IN_BUFS = 3


def _dot(a, b):
    return jnp.dot(a, b, preferred_element_type=F32)


def _dot_nt(a, b):
    return lax.dot_general(a, b, (((1,), (1,)), ((), ())), preferred_element_type=F32)


def _dot_tn(a, b):
    return lax.dot_general(a, b, (((0,), (0,)), ((), ())), preferred_element_type=F32)


def _shr(x, d):
    assert d & (d - 1) == 0
    return lax.shift_right_logical(x, d.bit_length() - 1)


def _split_dot(m01, x):
    hi = x.astype(BF16)
    lo = (x - hi.astype(F32)).astype(BF16)
    return _dot(m01, hi) + _dot(m01, lo)


def _split_dot_r(x, m01):
    hi = x.astype(BF16)
    lo = (x - hi.astype(F32)).astype(BF16)
    return _dot(hi, m01) + _dot(lo, m01)


def _sigmoid(x):
    return 1.0 / (1.0 + jnp.exp(-x))


def _silu(x):
    return x * _sigmoid(x)


def _gelu(x):
    c = math.sqrt(2.0 / math.pi)
    return x * (0.5 * (1.0 + jnp.tanh(c * (x + 0.044715 * (x * x * x)))))


def _log_sigmoid(x):
    return -(jnp.maximum(-x, 0.0) + jnp.log(1.0 + jnp.exp(-jnp.abs(x))))


def _layernorm(x, g, b, eps=1e-5):
    mu = jnp.mean(x, axis=-1, keepdims=True)
    xc = x - mu
    var = jnp.mean(xc * xc, axis=-1, keepdims=True)
    return xc * lax.rsqrt(var + eps) * g + b


def _rmsnorm(x, g, eps=1e-6):
    return x * lax.rsqrt(jnp.mean(x * x, axis=-1, keepdims=True) + eps) * g


def _gmlp_heads(u, vg, w_mix, bias, gmg_ref, merged_ref, rows):
    vgb = vg.astype(BF16)
    for h in range(GMLP_HEADS):
        cs = slice(h * GMLP_DH, (h + 1) * GMLP_DH)
        sgu = _dot(w_mix(h), vgb[:, cs]) + bias[:, cs]
        y = _rmsnorm(u[:, cs] * sgu, gmg_ref[h:h + 1, :])
        merged_ref[rows, GLA_WIDTH + h * GMLP_DH:GLA_WIDTH + (h + 1) * GMLP_DH] = y.astype(BF16)


def _out_proj_ln(x, merged_ref, wout_ref, ln1g_ref, ln1b_ref):
    m = _dot(merged_ref[...], wout_ref[...])
    return _layernorm(ALPHA * x + m, ln1g_ref[...], ln1b_ref[...])


def _mixer_prompt_kernel(x_ref, win_ref, waup_ref, ba_ref, glag_ref, lng_ref, lnb_ref, ws_ref, bsf_ref,
                         gmg_ref, wout_ref, ln1g_ref, ln1b_ref,
                         h_ref, state_ref,
                         z_ref, merged_ref, *st_refs):
    t = pl.program_id(1)

    @pl.when(t == 0)
    def _():
        for st_ref in st_refs:
            st_ref[...] = jnp.zeros_like(st_ref)

    x = x_ref[...].reshape(PROMPT_SEQS * PROMPT_TILE, D_MODEL)
    z_ref[...] = _dot(x.astype(BF16), win_ref[...])

    row_i = lax.broadcasted_iota(jnp.int32, (CHUNK, CHUNK), 0)
    col_i = lax.broadcasted_iota(jnp.int32, (CHUNK, CHUNK), 1)
    causal = row_i >= col_i
    tri = jnp.where(causal, 1.0, 0.0).astype(BF16)
    lane = lax.broadcasted_iota(jnp.int32, (1, LANES), 1)
    head_lanes = [lane < GLA_DK, lane >= GLA_DK]
    mid = CHUNK // 2 - 1

    for c in range(PROMPT_SEQS * PROMPT_TILE // CHUNK):
        rows = slice(c * CHUNK, (c + 1) * CHUNK)
        st_ref = st_refs[c // (PROMPT_TILE // CHUNK)]
        a_pre = _dot(z_ref[rows, C_ALR:C_ALR + LANES].astype(BF16), waup_ref[...]) + ba_ref[...]
        log_a = _log_sigmoid(a_pre) * (1.0 / GLA_TAU)
        b = _split_dot(tri, log_a)
        b_mid = b[mid:mid + 1, :]
        b_last = b[CHUNK - 1:CHUNK, :]
        q = z_ref[rows, C_Q:C_Q + QK_WIDTH] * (GLA_DK ** -0.5)
        k = z_ref[rows, C_K:C_K + QK_WIDTH]
        q_in = (q * jnp.exp(b - b_mid)).astype(BF16)
        k_in = (k * jnp.exp(b_mid - b)).astype(BF16)
        q_st = (q * jnp.exp(b)).astype(BF16)
        k_st = (k * jnp.exp(b_last - b)).astype(BF16)
        d_last = jnp.exp(b_last)
        va = z_ref[rows, C_VA:C_VA + GLA_WIDTH].astype(BF16)
        for h in range(GLA_HEADS):
            ps = slice((h // 2) * LANES, (h // 2 + 1) * LANES)
            vs = slice(h * GLA_DV, (h + 1) * GLA_DV)
            hm = head_lanes[h % 2]
            zero = jnp.zeros((), BF16)
            a = _dot_nt(jnp.where(hm, q_in[:, ps], zero), k_in[:, ps])
            a = jnp.where(causal, a, 0.0).astype(BF16)
            st = st_ref[h]
            o = _dot(a, va[:, vs]) + _dot_nt(jnp.where(hm, q_st[:, ps], zero), st.astype(BF16))
            upd = _dot_tn(va[:, vs], jnp.where(hm, k_st[:, ps], zero))
            st_ref[h] = st * d_last[:, ps] + upd
            gate = z_ref[rows, C_G + h * GLA_DV:C_G + (h + 1) * GLA_DV]
            o = _rmsnorm(o, glag_ref[h:h + 1, :]) * _silu(gate)
            merged_ref[rows, vs] = o.astype(BF16)
        u = _gelu(z_ref[rows, C_U:C_U + GMLP_WIDTH])
        vg = _layernorm(_gelu(z_ref[rows, C_VB:C_VB + GMLP_WIDTH]), lng_ref[...], lnb_ref[...])
        _gmlp_heads(u, vg, lambda h: jnp.where(causal, ws_ref[h], 0.0).astype(BF16), bsf_ref[...],
                    gmg_ref, merged_ref, rows)

    h_ref[...] = _out_proj_ln(x, merged_ref, wout_ref, ln1g_ref, ln1b_ref).reshape(PROMPT_SEQS, PROMPT_TILE, D_MODEL)

    @pl.when(t == pl.num_programs(1) - 1)
    def _():
        for s, st_ref in enumerate(st_refs):
            for h in range(GLA_HEADS):
                lo = (h % 2) * GLA_DK
                state_ref[s, h] = st_ref[h].T[lo:lo + GLA_DK, :]


def _mixer_sample_kernel(seq_len, x_ref, s0_ref, win_ref, wkt_ref, walrt_ref, waup_ref, waupt_ref, ba_ref, bac_ref,
                         glag_ref, lng_ref, lnb_ref, wst_ref, bsf_ref, gmg_ref, wout_ref, ln1g_ref, ln1b_ref,
                         h_ref, snew_ref, vg_ref,
                         merged_ref):
    n = SAMPLE_SEQS * seq_len
    x = x_ref[...]
    xb = x.astype(BF16)
    z = _dot(xb, win_ref[...])

    ti = lax.broadcasted_iota(jnp.int32, (n, n), 0)
    tj = lax.broadcasted_iota(jnp.int32, (n, n), 1)
    same = _shr(ti, seq_len) == _shr(tj, seq_len)
    causal = jnp.logical_and(same, ti >= tj)
    tri = jnp.where(causal, 1.0, 0.0).astype(BF16)
    tri_t = jnp.where(jnp.logical_and(same, ti <= tj), 1.0, 0.0).astype(BF16)
    same01 = jnp.where(same, 1.0, 0.0).astype(BF16)

    a_pre = _dot(z[:, C_ALR:C_ALR + LANES].astype(BF16), waup_ref[...]) + ba_ref[...]
    log_a = _log_sigmoid(a_pre) * (1.0 / GLA_TAU)
    b = _split_dot(tri, log_a)
    q = z[:, C_Q:C_Q + QK_WIDTH] * (GLA_DK ** -0.5)
    k = z[:, C_K:C_K + QK_WIDTH]
    q_in = (q * jnp.exp(b)).astype(BF16)
    k_in = (k * jnp.exp(-b)).astype(BF16)
    va = z[:, C_VA:C_VA + GLA_WIDTH].astype(BF16)

    k_t = _dot_nt(wkt_ref[...], xb)
    alr_t = _dot_nt(walrt_ref[...], xb)
    a_pre_t = _dot(waupt_ref[...], alr_t.astype(BF16)) + bac_ref[...]
    log_a_t = _log_sigmoid(a_pre_t) * (1.0 / GLA_TAU)
    b_t = _split_dot_r(log_a_t, tri_t)
    tot_t = _split_dot_r(log_a_t, same01)
    k_st_t = k_t * jnp.exp(tot_t - b_t)
    d_t = jnp.exp(tot_t)

    lane = lax.broadcasted_iota(jnp.int32, (1, LANES), 1)
    head_lanes = [lane < GLA_DK, lane >= GLA_DK]
    nrow = SAMPLE_SEQS * GLA_DK
    r_seq = _shr(lax.broadcasted_iota(jnp.int32, (nrow, n), 0), GLA_DK)
    c_seq = _shr(lax.broadcasted_iota(jnp.int32, (nrow, n), 1), seq_len)
    c_first = (lax.broadcasted_iota(jnp.int32, (nrow, n), 1) & (seq_len - 1)) == 0
    blk = r_seq == c_seq
    blk_first = jnp.logical_and(blk, c_first)
    q_seq = _shr(lax.broadcasted_iota(jnp.int32, (n, nrow), 0), seq_len)
    q_col = _shr(lax.broadcasted_iota(jnp.int32, (n, nrow), 1), GLA_DK)
    blk_q = q_seq == q_col

    for h in range(GLA_HEADS):
        ps = slice((h // 2) * LANES, (h // 2 + 1) * LANES)
        ds_ = slice(h * GLA_DK, (h + 1) * GLA_DK)
        vs = slice(h * GLA_DV, (h + 1) * GLA_DV)
        hm = head_lanes[h % 2]
        zero = jnp.zeros((), BF16)
        a = _dot_nt(jnp.where(hm, q_in[:, ps], zero), k_in[:, ps])
        a = jnp.where(causal, a, 0.0).astype(BF16)
        s0 = s0_ref[:, h].reshape(nrow, GLA_DV)
        q_h = q_in[:, ds_]
        q_bd = jnp.where(blk_q, jnp.concatenate([q_h] * SAMPLE_SEQS, axis=1), zero)
        o = _dot(a, va[:, vs]) + _dot(q_bd, s0.astype(BF16))
        k_bd = jnp.where(blk, jnp.concatenate([k_st_t[ds_, :]] * SAMPLE_SEQS, axis=0), 0.0).astype(BF16)
        upd = _dot(k_bd, va[:, vs])
        d_bd = jnp.where(blk_first, jnp.concatenate([d_t[ds_, :]] * SAMPLE_SEQS, axis=0), 0.0)
        d_col = jnp.sum(d_bd, axis=1, keepdims=True)
        snew_ref[:, h] = (s0 * d_col + upd).reshape(SAMPLE_SEQS, GLA_DK, GLA_DV)
        gate = z[:, C_G + h * GLA_DV:C_G + (h + 1) * GLA_DV]
        o = _rmsnorm(o, glag_ref[h:h + 1, :]) * _silu(gate)
        merged_ref[:, vs] = o.astype(BF16)

    u = _gelu(z[:, C_U:C_U + GMLP_WIDTH])
    vg = _layernorm(_gelu(z[:, C_VB:C_VB + GMLP_WIDTH]), lng_ref[...], lnb_ref[...])
    vg_ref[...] = vg
    pi = lax.broadcasted_iota(jnp.int32, (n, LANES), 0)
    pj = lax.broadcasted_iota(jnp.int32, (n, LANES), 1)
    pos = jnp.where((pi & (seq_len - 1)) == pj, 1.0, 0.0).astype(BF16)

    def w_mix(h):
        rows = _dot(pos, wst_ref[h].astype(BF16)).astype(BF16)
        return jnp.where(causal, _dot_nt(rows, pos), 0.0).astype(BF16)

    _gmlp_heads(u, vg, w_mix, bsf_ref[...], gmg_ref, merged_ref, slice(None))
    h_ref[...] = _out_proj_ln(x, merged_ref, wout_ref, ln1g_ref, ln1b_ref)


def _route(hb, wrt_ref, rbias_ref):
    tm = hb.shape[0]
    s = _sigmoid(_dot_nt(wrt_ref[...], hb))
    sb = s + rbias_ref[...]
    neg = jnp.float32(-jnp.inf)
    sub = lax.broadcasted_iota(jnp.int32, (GROUP_SIZE, tm), 0)
    gscore = []
    for g in range(N_GROUPS):
        blk = sb[g * GROUP_SIZE:(g + 1) * GROUP_SIZE, :]
        m1 = jnp.max(blk, axis=0, keepdims=True)
        i1 = jnp.min(jnp.where(blk == m1, sub, GROUP_SIZE), axis=0, keepdims=True)
        m2 = jnp.max(jnp.where(sub == i1, neg, blk), axis=0, keepdims=True)
        gscore.append(m1 + m2)
    gsel = []
    for g in range(N_GROUPS):
        rank = jnp.zeros((1, tm), jnp.int32)
        for o in range(N_GROUPS):
            if o == g:
                continue
            ahead = (gscore[o] >= gscore[g]) if o < g else (gscore[o] > gscore[g])
            rank = rank + ahead.astype(jnp.int32)
        gsel.append(rank < TOPK_GROUPS)
    v = jnp.concatenate(
        [jnp.where(gsel[g], sb[g * GROUP_SIZE:(g + 1) * GROUP_SIZE, :], neg) for g in range(N_GROUPS)], axis=0)
    eidx = lax.broadcasted_iota(jnp.int32, (N_EXPERTS, tm), 0)
    sel = jnp.zeros((N_EXPERTS, tm), F32)
    for _ in range(TOP_K):
        m = jnp.max(v, axis=0, keepdims=True)
        first = jnp.min(jnp.where(v == m, eidx, N_EXPERTS), axis=0, keepdims=True)
        hit = eidx == first
        sel = jnp.where(hit, 1.0, sel)
        v = jnp.where(hit, neg, v)
    wsel = s * sel
    return wsel / jnp.sum(wsel, axis=0, keepdims=True) * ROUTE_SCALE, sel


def _per_row(chunk_vals):
    n, t = chunk_vals.shape
    return jnp.broadcast_to(chunk_vals[:, None, :], (n, SEG, t)).reshape(n * SEG, t)


def _row_lookup(seg_b, start_b, rank, tables):
    n_chunks = BLOCK_ROWS // SEG
    ei = lax.broadcasted_iota(jnp.int32, (N_EXPERTS, LANES), 0)
    ej = lax.broadcasted_iota(jnp.int32, (N_EXPERTS, LANES), 1)
    seg16 = (seg_b * (1.0 / SEG)).astype(BF16)
    start_row = _dot_tn(seg16, jnp.where(ei < ej, 1.0, 0.0).astype(BF16))[0:1, :]
    seg_row = _dot_tn(seg16, jnp.where(ei == ej, 1.0, 0.0).astype(BF16))[0:1, :]
    c = lax.broadcasted_iota(jnp.int32, (n_chunks, LANES), 0).astype(F32)
    owner = jnp.where(c >= start_row, jnp.where(c < start_row + seg_row, 1.0, 0.0), 0.0).astype(BF16)

    def lookup(tab):
        padded = jnp.concatenate([tab.astype(BF16), jnp.zeros((LANES - N_EXPERTS, tab.shape[1]), BF16)], axis=0)
        return _dot(owner, padded)

    first = SEG * (c - lookup(start_b * (1.0 / SEG)))[:, 0:1]
    rel = lookup(rank) - first
    sub = lax.broadcasted_iota(jnp.int32, (BLOCK_ROWS, rank.shape[1]), 0) & (SEG - 1)
    hits = _per_row(rel) == sub.astype(F32)
    return hits, [_per_row(lookup(tab)) for tab in tables]


def _dispatch_kernel(n_prompt_blocks, n_blocks, hp_ref, hs_ref, wrt_ref, rbias_ref, x_ref, rank_ref, comb_ref,
                     seg_ref, start_ref):
    i = pl.program_id(0)

    @pl.when(i < n_blocks)
    def _():
        _dispatch_block(i < n_prompt_blocks, hp_ref, hs_ref, wrt_ref, rbias_ref, x_ref, rank_ref, comb_ref, seg_ref,
                        start_ref)

    @pl.when(i >= n_blocks)
    def _():
        for ref in (x_ref, rank_ref, comb_ref, seg_ref, start_ref):
            ref[...] = jnp.zeros_like(ref)


def _dispatch_block(is_prompt, hp_ref, hs_ref, wrt_ref, rbias_ref, x_ref, rank_ref, comb_ref, seg_ref, start_ref):
    tm = BLOCK
    hb = jnp.where(is_prompt, hp_ref[...], hs_ref[...]).astype(BF16)
    comb, sel = _route(hb, wrt_ref, rbias_ref)
    ti = lax.broadcasted_iota(jnp.int32, (tm, tm), 0)
    tj = lax.broadcasted_iota(jnp.int32, (tm, tm), 1)
    before = jnp.where(ti < tj, 1.0, 0.0).astype(BF16)
    rank = _dot(sel.astype(BF16), before)
    rank = jnp.where(sel > 0.0, rank, -1.0)
    cnt = jnp.sum(sel, axis=1, keepdims=True)
    seg = jnp.floor((cnt + (SEG - 1.0)) * (1.0 / SEG)) * SEG
    ei = lax.broadcasted_iota(jnp.int32, (N_EXPERTS, N_EXPERTS), 0)
    ej = lax.broadcasted_iota(jnp.int32, (N_EXPERTS, N_EXPERTS), 1)
    below = jnp.where(ej < ei, 1.0, 0.0).astype(BF16)
    seg_b = jnp.broadcast_to(seg, (N_EXPERTS, LANES))
    start_b = _dot(below, seg_b.astype(BF16))
    hits, _ = _row_lookup(seg_b, start_b, rank, [])
    gather = jnp.where(hits, 1.0, 0.0).astype(BF16)
    for c in range(BLOCK_ROWS // GATHER_ROWS):
        rs = slice(c * GATHER_ROWS, (c + 1) * GATHER_ROWS)
        x_ref[rs, :] = _dot(gather[rs, :], hb).astype(BF16)
    rank_ref[0] = rank
    comb_ref[0] = comb
    seg_ref[0] = seg_b
    start_ref[0] = start_b


def _ffn_kernel(dump_base, nt_max, tile_expert_ref, tile_in_expert_ref, n_chunks_ref, n_tiles_ref,
                tabp_ref, tab0_ref, tab1_ref, tab2_ref, x_hbm, wg_ref, wu_ref, wd_ref,
                o_hbm, ibuf, obuf, wgb, wub, wdb, in_sem, out_sem):
    i = pl.program_id(0)
    n_tiles = n_tiles_ref[0]
    tabs = (tab0_ref, tab1_ref, tab2_ref)

    def start_in(ahead, slot):
        exists = i + ahead < n_tiles
        first = jnp.where(exists, tile_in_expert_ref[jnp.minimum(i + ahead, nt_max - 1)] * TILE_CHUNKS, 0)
        srcs = [jnp.where(exists, tabs[ahead][0, 0, first + j], ZERO_CHUNK) for j in range(TILE_CHUNKS)]
        for j in range(TILE_CHUNKS):
            pltpu.make_async_copy(x_hbm.at[srcs[j]], ibuf.at[slot, j], in_sem.at[slot]).start()

    def start_out(tile, tab_ref, slot):
        t = jnp.maximum(tile, 0)
        first = tile_in_expert_ref[t] * TILE_CHUNKS
        n_real = jnp.where(tile >= 0, n_chunks_ref[tile_expert_ref[t]] - first, 0)
        dsts = [jnp.where(j < n_real, tab_ref[0, 0, first + j], dump_base + slot * TILE_CHUNKS + j)
                for j in range(TILE_CHUNKS)]
        for j in range(TILE_CHUNKS):
            pltpu.make_async_copy(obuf.at[slot, j], o_hbm.at[dsts[j]], out_sem.at[slot]).start()

    def in_wait(slot):
        for j in range(TILE_CHUNKS):
            pltpu.make_async_copy(x_hbm.at[0], ibuf.at[slot, j], in_sem.at[slot]).wait()

    def out_wait(slot):
        for j in range(TILE_CHUNKS):
            pltpu.make_async_copy(obuf.at[slot, j], o_hbm.at[0], out_sem.at[slot]).wait()

    @pl.when(i < n_tiles)
    def _():
        islot = lax.rem(i, IN_BUFS)
        nslot = jnp.where(islot == 0, IN_BUFS - 1, islot - 1)
        oslot = lax.rem(i, 2)

        @pl.when(i == 0)
        def _():
            start_in(0, 0)
            start_in(1, 1)
            obuf[1] = jnp.zeros(obuf.shape[1:], BF16)

        @pl.when(i >= 1)
        def _():
            out_wait(oslot)

        in_wait(islot)

        @pl.when(tile_in_expert_ref[i] == 0)
        def _():
            wgb[...] = wg_ref[0].astype(BF16)
            wub[...] = wu_ref[0].astype(BF16)
            wdb[...] = wd_ref[0].astype(BF16)

        for s in range(TILE_CHUNKS // SUB_CHUNKS):
            cs = slice(s * SUB_CHUNKS, (s + 1) * SUB_CHUNKS)
            x = ibuf[islot, cs].reshape(SUB_CHUNKS * SEG, D_MODEL)
            hid = (_silu(_dot(x, wgb[...])) * _dot(x, wub[...])).astype(BF16)
            if s == 0:
                start_in(2, nslot)
                start_out(i - 1, tabp_ref, 1 - oslot)
            obuf[oslot, cs] = _dot(hid, wdb[...]).astype(BF16).reshape(SUB_CHUNKS, SEG, D_MODEL)

        @pl.when(i == n_tiles - 1)
        def _():
            start_out(i, tab0_ref, oslot)
            in_wait(lax.rem(i + 1, IN_BUFS))
            in_wait(nslot)
            out_wait(1 - oslot)
            out_wait(oslot)


def _combine_kernel(h_ref, o_ref, rank_ref, comb_ref, seg_ref, start_ref, wsg_ref, wsu_ref, wsd_ref, ln2g_ref,
                    ln2b_ref, y_ref):
    h = h_ref[...]
    hb = h.astype(BF16)
    hits, (comb_rows,) = _row_lookup(seg_ref[0], start_ref[0], rank_ref[0], [comb_ref[0]])
    scatter = jnp.where(hits, comb_rows, 0.0).astype(BF16)
    routed = _dot_tn(scatter, o_ref[...])
    shared = _dot((_silu(_dot(hb, wsg_ref[...])) * _dot(hb, wsu_ref[...])).astype(BF16), wsd_ref[...])
    y_ref[...] = _layernorm(ALPHA * h + (routed + shared), ln2g_ref[...], ln2b_ref[...])


def _full(shape):
    return pl.BlockSpec(shape, lambda *_: (0,) * len(shape))


def _const(shape):
    return pl.BlockSpec(shape, lambda *_: (0,) * len(shape), pipeline_mode=pl.Buffered(1))


def _mixer_prompt(x, wts):
    bsz, seq, _ = x.shape
    n_t = seq // PROMPT_TILE
    weights = [wts[k] for k in ("win", "waup", "ba", "glag", "lng", "lnb", "ws", "bsf_prompt", "gmg", "wout",
                                "ln1g", "ln1b")]
    return pl.pallas_call(
        _mixer_prompt_kernel,
        grid=(bsz // PROMPT_SEQS, n_t),
        in_specs=[pl.BlockSpec((PROMPT_SEQS, PROMPT_TILE, D_MODEL), lambda b, t: (b, t, 0))]
        + [_const(w.shape) for w in weights],
        out_specs=[pl.BlockSpec((PROMPT_SEQS, PROMPT_TILE, D_MODEL), lambda b, t: (b, t, 0)),
                   pl.BlockSpec((PROMPT_SEQS, GLA_HEADS, GLA_DK, GLA_DV), lambda b, t: (b, 0, 0, 0))],
        out_shape=[jax.ShapeDtypeStruct((bsz, seq, D_MODEL), F32),
                   jax.ShapeDtypeStruct((bsz, GLA_HEADS, GLA_DK, GLA_DV), F32)],
        scratch_shapes=[pltpu.VMEM((PROMPT_SEQS * PROMPT_TILE, N_PROJ), F32),
                        pltpu.VMEM((PROMPT_SEQS * PROMPT_TILE, D_MODEL), BF16)]
        + [pltpu.VMEM((GLA_HEADS, GLA_DV, LANES), F32)] * PROMPT_SEQS,
        compiler_params=pltpu.CompilerParams(dimension_semantics=("arbitrary", "arbitrary"),
                                             vmem_limit_bytes=VMEM_LIMIT),
        name="mixer_prompt",
    )(x, *weights)


def _mixer_sample(x, s0, wts):
    bsz, seq_len, _ = x.shape
    n = SAMPLE_SEQS * seq_len
    x2 = x.reshape(bsz * seq_len, D_MODEL)
    weights = [wts[k] for k in ("win", "wkt", "walrt", "waup", "waupt", "ba", "bac", "glag", "lng", "lnb",
                                "ws_sample", "bsf_sample", "gmg", "wout", "ln1g", "ln1b")]
    state_spec = pl.BlockSpec((SAMPLE_SEQS, GLA_HEADS, GLA_DK, GLA_DV), lambda i: (i, 0, 0, 0))
    h, s_new, vg = pl.pallas_call(
        functools.partial(_mixer_sample_kernel, seq_len),
        grid=(bsz // SAMPLE_SEQS,),
        in_specs=[pl.BlockSpec((n, D_MODEL), lambda i: (i, 0)), state_spec] + [_full(w.shape) for w in weights],
        out_specs=[pl.BlockSpec((n, D_MODEL), lambda i: (i, 0)), state_spec,
                   pl.BlockSpec((n, GMLP_WIDTH), lambda i: (i, 0))],
        out_shape=[jax.ShapeDtypeStruct((bsz * seq_len, D_MODEL), F32),
                   jax.ShapeDtypeStruct(s0.shape, F32),
                   jax.ShapeDtypeStruct((bsz * seq_len, GMLP_WIDTH), F32)],
        scratch_shapes=[pltpu.VMEM((n, D_MODEL), BF16)],
        compiler_params=pltpu.CompilerParams(dimension_semantics=("arbitrary",), vmem_limit_bytes=VMEM_LIMIT),
        name="mixer_sample",
    )(x2, s0, *weights)
    return h, s_new, vg.reshape(bsz, seq_len, GMLP_WIDTH)


def _dispatch(h_p, h_s, wts):
    npb, nsb = h_p.shape[0] // BLOCK, h_s.shape[0] // BLOCK
    nb_real = npb + nsb
    nb = nb_real + DUMP_BLOCKS
    tok_spec = pl.BlockSpec((1, N_EXPERTS, BLOCK), lambda i: (i, 0, 0))
    run_spec = pl.BlockSpec((1, N_EXPERTS, LANES), lambda i: (i, 0, 0))
    return pl.pallas_call(
        functools.partial(_dispatch_kernel, npb, nb_real),
        grid=(nb,),
        in_specs=[pl.BlockSpec((BLOCK, D_MODEL), lambda i: (jnp.minimum(i, npb - 1), 0)),
                  pl.BlockSpec((BLOCK, D_MODEL), lambda i: (jnp.clip(i - npb, 0, nsb - 1), 0)),
                  _full(wts["wrt"].shape), _full(wts["rbias"].shape)],
        out_specs=[pl.BlockSpec((BLOCK_ROWS, D_MODEL), lambda i: (i, 0)), tok_spec, tok_spec, run_spec, run_spec],
        out_shape=[jax.ShapeDtypeStruct((nb * BLOCK_ROWS, D_MODEL), BF16),
                   jax.ShapeDtypeStruct((nb, N_EXPERTS, BLOCK), F32),
                   jax.ShapeDtypeStruct((nb, N_EXPERTS, BLOCK), F32),
                   jax.ShapeDtypeStruct((nb, N_EXPERTS, LANES), F32),
                   jax.ShapeDtypeStruct((nb, N_EXPERTS, LANES), F32)],
        compiler_params=pltpu.CompilerParams(dimension_semantics=("arbitrary",), vmem_limit_bytes=VMEM_LIMIT),
        name="moe_dispatch",
    )(h_p, h_s, wts["wrt"], wts["rbias"])


def _max_tiles(nb):
    return (nb * (BLOCK_ROWS_USED // SEG) + N_EXPERTS * (TILE_CHUNKS - 1)) // TILE_CHUNKS + 1


def _max_expert_chunks(nb):
    most = nb * (BLOCK // SEG) + TILE_CHUNKS - 1
    return -(-most // LANES) * LANES


def _plan(seg_lanes, nb):
    nt_max = _max_tiles(nb)
    max_chunks = _max_expert_chunks(nb)
    seg = seg_lanes[:nb, :, 0].astype(jnp.int32)
    start = jnp.cumsum(seg, axis=1) - seg
    nch = seg // SEG
    ends = jnp.cumsum(nch, axis=0)
    n_chunks = ends[-1]
    f = jnp.arange(nb, dtype=jnp.int32)[:, None] * (BLOCK_ROWS // SEG) + start // SEG - (ends - nch)
    q = jnp.arange(max_chunks, dtype=jnp.int32)
    passed = (ends[None, :-1, :] <= q[:, None, None]).astype(jnp.int32)
    chunk = q[:, None] + f[0][None, :] + jnp.sum(passed * (f[1:] - f[:-1])[None], axis=1)
    table = jnp.where(q[:, None] < n_chunks[None, :], chunk, ZERO_CHUNK).T
    n_tiles_e = -(-n_chunks // TILE_CHUNKS)
    tile_ends = jnp.cumsum(n_tiles_e)
    t = jnp.arange(nt_max, dtype=jnp.int32)
    done = (tile_ends[None, :] <= t[:, None]).astype(jnp.int32)
    tile_expert = jnp.minimum(jnp.sum(done, axis=1), N_EXPERTS - 1)
    tile_in_expert = t - jnp.sum(done * n_tiles_e[None, :], axis=1)
    return (table.astype(jnp.int32).reshape(N_EXPERTS, 1, max_chunks), tile_expert.astype(jnp.int32),
            tile_in_expert.astype(jnp.int32), n_chunks.astype(jnp.int32), tile_ends[-1:].astype(jnp.int32))


def _ffn(x_rows, nb, table, tile_expert, tile_in_expert, n_chunks, n_tiles, w_gate, w_up, w_down):
    nt_max = tile_expert.shape[0]
    max_chunks = table.shape[-1]
    tab_spec = lambda ahead: pl.BlockSpec(
        (1, 1, max_chunks), lambda i, te, *_: (te[jnp.clip(i + ahead, 0, nt_max - 1)], 0, 0),
        memory_space=pltpu.SMEM)
    w_spec = lambda shape: pl.BlockSpec((1,) + shape, lambda i, te, *_: (te[i], 0, 0))
    assert x_rows.shape[0] - nb * BLOCK_ROWS >= 2 * TILE_ROWS
    x_chunks = x_rows.reshape(x_rows.shape[0] // SEG, SEG, D_MODEL)
    return pl.pallas_call(
        functools.partial(_ffn_kernel, nb * BLOCK_ROWS // SEG, nt_max),
        grid_spec=pltpu.PrefetchScalarGridSpec(
            num_scalar_prefetch=4,
            grid=(nt_max,),
            in_specs=[tab_spec(-1), tab_spec(0), tab_spec(1), tab_spec(2), pl.BlockSpec(memory_space=pl.ANY),
                      w_spec((D_MODEL, D_EXPERT)), w_spec((D_MODEL, D_EXPERT)), w_spec((D_EXPERT, D_MODEL))],
            out_specs=pl.BlockSpec(memory_space=pl.ANY),
            scratch_shapes=[pltpu.VMEM((IN_BUFS, TILE_CHUNKS, SEG, D_MODEL), BF16),
                            pltpu.VMEM((2, TILE_CHUNKS, SEG, D_MODEL), BF16),
                            pltpu.VMEM((D_MODEL, D_EXPERT), BF16),
                            pltpu.VMEM((D_MODEL, D_EXPERT), BF16),
                            pltpu.VMEM((D_EXPERT, D_MODEL), BF16),
                            pltpu.SemaphoreType.DMA((IN_BUFS,)),
                            pltpu.SemaphoreType.DMA((2,))]),
        out_shape=jax.ShapeDtypeStruct(x_chunks.shape, BF16),
        input_output_aliases={8: 0},
        compiler_params=pltpu.CompilerParams(dimension_semantics=("arbitrary",), vmem_limit_bytes=VMEM_LIMIT),
        name="moe_ffn",
    )(tile_expert, tile_in_expert, n_chunks, n_tiles, table, table, table, table, x_chunks, w_gate, w_up, w_down
      ).reshape(x_rows.shape)


def _combine(h, o_rows, routing, block_off, wts):
    nblk = h.shape[0] // BLOCK
    weights = [wts[k] for k in ("wsg", "wsu", "wsd", "ln2g", "ln2b")]
    tok_spec = pl.BlockSpec((1, N_EXPERTS, BLOCK), lambda i: (i + block_off, 0, 0))
    run_spec = pl.BlockSpec((1, N_EXPERTS, LANES), lambda i: (i + block_off, 0, 0))
    return pl.pallas_call(
        _combine_kernel,
        grid=(nblk,),
        in_specs=[pl.BlockSpec((BLOCK, D_MODEL), lambda i: (i, 0)),
                  pl.BlockSpec((BLOCK_ROWS, D_MODEL), lambda i: (i + block_off, 0)),
                  tok_spec, tok_spec, run_spec, run_spec] + [_full(w.shape) for w in weights],
        out_specs=pl.BlockSpec((BLOCK, D_MODEL), lambda i: (i, 0)),
        out_shape=jax.ShapeDtypeStruct(h.shape, F32),
        compiler_params=pltpu.CompilerParams(dimension_semantics=("arbitrary",), vmem_limit_bytes=VMEM_LIMIT),
        name="moe_combine",
    )(h, o_rows, *routing, *weights)


def _moe(h_p, h_s, w_gate, w_up, w_down, wts):
    nb = (h_p.shape[0] + h_s.shape[0]) // BLOCK
    x_rows, rank, comb, seg_lanes, start_lanes = _dispatch(h_p, h_s, wts)
    o_rows = _ffn(x_rows, nb, *_plan(seg_lanes, nb), w_gate, w_up, w_down)
    routing = (rank, comb, seg_lanes, start_lanes)
    y_p = _combine(h_p, o_rows, routing, 0, wts)
    y_s = _combine(h_s, o_rows, routing, h_p.shape[0] // BLOCK, wts)
    return y_p, y_s


def _prep_weights(seq_len, w_in, w_a_up, b_a, gla_norm_g, gmlp_ln_g, gmlp_ln_b, w_s, b_s, gmlp_norm_g, w_out,
                  ln1_g, ln1_b, w_router, router_bias, ws_gate, ws_up, ws_down, ln2_g, ln2_b):
    o1 = QK_WIDTH
    o2 = o1 + QK_WIDTH
    o3 = o2 + GLA_WIDTH
    o4 = o3 + GLA_WIDTH
    o5 = o4 + GLA_RANK
    o6 = o5 + GMLP_WIDTH
    wq, wk, wva, wg_, walr, wu_, wvb = jnp.split(w_in, [o1, o2, o3, o4, o5, o6], axis=-1)
    walr_p = jnp.pad(walr, ((0, 0), (0, LANES - GLA_RANK)))
    waup_p = jnp.pad(w_a_up, ((0, LANES - GLA_RANK), (0, 0)))
    row = lambda a: a.reshape(1, -1)
    reps = (SAMPLE_SEQS * seq_len) // seq_len
    ws_small = w_s[:, :seq_len, :seq_len]
    return {
        "win": jnp.concatenate([wq, wk, wva, wg_, wu_, wvb, walr_p], axis=-1).astype(BF16),
        "wkt": wk.T.astype(BF16),
        "walrt": walr_p.T.astype(BF16),
        "waup": waup_p.astype(BF16),
        "waupt": waup_p.T.astype(BF16),
        "ba": row(b_a), "bac": b_a.reshape(-1, 1),
        "glag": gla_norm_g, "lng": row(gmlp_ln_g), "lnb": row(gmlp_ln_b),
        "ws": w_s,
        "ws_sample": jnp.pad(ws_small, ((0, 0), (0, LANES - seq_len), (0, LANES - seq_len))),
        "bsf_prompt": jnp.repeat(b_s[:, :GMLP_CHUNK].T, GMLP_DH, axis=1),
        "bsf_sample": jnp.tile(jnp.repeat(b_s[:, :seq_len].T, GMLP_DH, axis=1), (reps, 1)),
        "gmg": gmlp_norm_g,
        "wout": w_out.astype(BF16),
        "ln1g": row(ln1_g), "ln1b": row(ln1_b),
        "wrt": w_router.T.astype(BF16), "rbias": router_bias.reshape(-1, 1),
        "wsg": ws_gate.astype(BF16), "wsu": ws_up.astype(BF16), "wsd": ws_down.astype(BF16),
        "ln2g": row(ln2_g), "ln2b": row(ln2_b),
    }


def kernel(x_prompt, x_sample, state_gla, w_in, w_a_up, b_a, gla_norm_g, gmlp_ln_g, gmlp_ln_b, w_s, b_s,
           gmlp_norm_g, w_out, ln1_g, ln1_b, w_router, router_bias, w_gate, w_up, w_down, ws_gate, ws_up,
           ws_down, ln2_g, ln2_b):
    assert x_prompt.shape[1] % PROMPT_TILE == 0 and x_sample.shape[0] % SAMPLE_SEQS == 0
    assert x_prompt.shape[0] % PROMPT_SEQS == 0
    assert x_sample.shape[1] <= GMLP_CHUNK and w_in.shape[0] == DEPTH
    assert (x_sample.shape[0] * x_sample.shape[1]) % BLOCK == 0 and PROMPT_TILE % BLOCK == 0
    bsz, seq, _ = x_prompt.shape
    dbsz, dseq, _ = x_sample.shape
    hp, hs = x_prompt, x_sample
    gla_p, gla_s, v_s = [], [], []
    for l in range(DEPTH):
        wts = _prep_weights(dseq, w_in[l], w_a_up[l], b_a[l], gla_norm_g[l], gmlp_ln_g[l], gmlp_ln_b[l], w_s[l],
                            b_s[l], gmlp_norm_g[l], w_out[l], ln1_g[l], ln1_b[l], w_router[l], router_bias[l],
                            ws_gate[l], ws_up[l], ws_down[l], ln2_g[l], ln2_b[l])
        h_p, sp = _mixer_prompt(hp, wts)
        h_s, ss, vrows = _mixer_sample(hs, state_gla[l], wts)
        y_p, y_s = _moe(h_p.reshape(bsz * seq, D_MODEL), h_s, w_gate[l], w_up[l], w_down[l], wts)
        hp = y_p.reshape(bsz, seq, D_MODEL)
        hs = y_s.reshape(dbsz, dseq, D_MODEL)
        gla_p.append(sp)
        gla_s.append(ss)
        v_s.append(vrows)
    return (hp, hs, jnp.stack(gla_p), jnp.stack(gla_s), jnp.stack(v_s))
```

```python
import functools
import math

import jax
import jax.numpy as jnp
from jax import lax
from jax.experimental import pallas as pl
from jax.experimental.pallas import tpu as pltpu

F32 = jnp.float32
BF16 = jnp.bfloat16

D_MODEL = 1024
DEPTH = 1
GLA_WIDTH = 512
GLA_HEADS = 4
GLA_DK = 64
GLA_DV = 128
GLA_RANK = 16
GLA_TAU = 16.0
GMLP_WIDTH = 512
GMLP_HEADS = 4
GMLP_DH = 128
GMLP_CHUNK = 128
QK_WIDTH = GLA_HEADS * GLA_DK
N_EXPERTS = 64
TOP_K = 8
N_GROUPS = 8
GROUP_SIZE = N_EXPERTS // N_GROUPS
TOPK_GROUPS = 4
D_EXPERT = 256
D_SHARED = 256
ROUTE_SCALE = 2.5
ALPHA = (2.0 * DEPTH) ** 0.25

LANES = 128

C_Q = 0
C_K = C_Q + QK_WIDTH
C_VA = C_K + QK_WIDTH
C_G = C_VA + GLA_WIDTH
C_U = C_G + GLA_WIDTH
C_VB = C_U + GMLP_WIDTH
C_ALR = C_VB + GMLP_WIDTH
N_PROJ = C_ALR + LANES

CHUNK = 128
PROMPT_TILE = 512
PROMPT_SEQS = 2
SAMPLE_SEQS = 32
VMEM_LIMIT = 56 * 1024 * 1024

BLOCK = 256
SEG = 16
TILE_CHUNKS = 64
SUB_CHUNKS = 64
TILE_ROWS = TILE_CHUNKS * SEG
GATHER_ROWS = 512
BLOCK_ROWS_USED = BLOCK * TOP_K + N_EXPERTS * (SEG - 1)
BLOCK_ROWS = -(-(BLOCK_ROWS_USED + SEG) // GATHER_ROWS) * GATHER_ROWS
DUMP_BLOCKS = -(-(2 * TILE_ROWS) // BLOCK_ROWS)
ZERO_CHUNK = BLOCK_ROWS_USED // SEG
IN_BUFS = 3


def _dot(a, b):
    return jnp.dot(a, b, preferred_element_type=F32)


def _dot_nt(a, b):
    return lax.dot_general(a, b, (((1,), (1,)), ((), ())), preferred_element_type=F32)


def _dot_tn(a, b):
    return lax.dot_general(a, b, (((0,), (0,)), ((), ())), preferred_element_type=F32)


def _shr(x, d):
    assert d & (d - 1) == 0
    return lax.shift_right_logical(x, d.bit_length() - 1)


def _split_dot(m01, x):
    hi = x.astype(BF16)
    lo = (x - hi.astype(F32)).astype(BF16)
    return _dot(m01, hi) + _dot(m01, lo)


def _split_dot_r(x, m01):
    hi = x.astype(BF16)
    lo = (x - hi.astype(F32)).astype(BF16)
    return _dot(hi, m01) + _dot(lo, m01)


def _sigmoid(x):
    return 1.0 / (1.0 + jnp.exp(-x))


def _silu(x):
    return x * _sigmoid(x)


def _gelu(x):
    c = math.sqrt(2.0 / math.pi)
    return x * (0.5 * (1.0 + jnp.tanh(c * (x + 0.044715 * (x * x * x)))))


def _log_sigmoid(x):
    return -(jnp.maximum(-x, 0.0) + jnp.log(1.0 + jnp.exp(-jnp.abs(x))))


def _layernorm(x, g, b, eps=1e-5):
    mu = jnp.mean(x, axis=-1, keepdims=True)
    xc = x - mu
    var = jnp.mean(xc * xc, axis=-1, keepdims=True)
    return xc * lax.rsqrt(var + eps) * g + b


def _rmsnorm(x, g, eps=1e-6):
    return x * lax.rsqrt(jnp.mean(x * x, axis=-1, keepdims=True) + eps) * g


def _gmlp_heads(u, vg, w_mix, bias, gmg_ref, merged_ref, rows):
    vgb = vg.astype(BF16)
    for h in range(GMLP_HEADS):
        cs = slice(h * GMLP_DH, (h + 1) * GMLP_DH)
        sgu = _dot(w_mix(h), vgb[:, cs]) + bias[:, cs]
        y = _rmsnorm(u[:, cs] * sgu, gmg_ref[h:h + 1, :])
        merged_ref[rows, GLA_WIDTH + h * GMLP_DH:GLA_WIDTH + (h + 1) * GMLP_DH] = y.astype(BF16)


def _out_proj_ln(x, merged_ref, wout_ref, ln1g_ref, ln1b_ref):
    m = _dot(merged_ref[...], wout_ref[...])
    return _layernorm(ALPHA * x + m, ln1g_ref[...], ln1b_ref[...])


def _mixer_prompt_kernel(x_ref, win_ref, waup_ref, ba_ref, glag_ref, lng_ref, lnb_ref, ws_ref, bsf_ref,
                         gmg_ref, wout_ref, ln1g_ref, ln1b_ref,
                         h_ref, state_ref,
                         z_ref, merged_ref, *st_refs):
    t = pl.program_id(1)

    @pl.when(t == 0)
    def _():
        for st_ref in st_refs:
            st_ref[...] = jnp.zeros_like(st_ref)

    x = x_ref[...].reshape(PROMPT_SEQS * PROMPT_TILE, D_MODEL)
    z_ref[...] = _dot(x.astype(BF16), win_ref[...])

    row_i = lax.broadcasted_iota(jnp.int32, (CHUNK, CHUNK), 0)
    col_i = lax.broadcasted_iota(jnp.int32, (CHUNK, CHUNK), 1)
    causal = row_i >= col_i
    tri = jnp.where(causal, 1.0, 0.0).astype(BF16)
    lane = lax.broadcasted_iota(jnp.int32, (1, LANES), 1)
    head_lanes = [lane < GLA_DK, lane >= GLA_DK]
    mid = CHUNK // 2 - 1

    for c in range(PROMPT_SEQS * PROMPT_TILE // CHUNK):
        rows = slice(c * CHUNK, (c + 1) * CHUNK)
        st_ref = st_refs[c // (PROMPT_TILE // CHUNK)]
        a_pre = _dot(z_ref[rows, C_ALR:C_ALR + LANES].astype(BF16), waup_ref[...]) + ba_ref[...]
        log_a = _log_sigmoid(a_pre) * (1.0 / GLA_TAU)
        b = _split_dot(tri, log_a)
        b_mid = b[mid:mid + 1, :]
        b_last = b[CHUNK - 1:CHUNK, :]
        q = z_ref[rows, C_Q:C_Q + QK_WIDTH] * (GLA_DK ** -0.5)
        k = z_ref[rows, C_K:C_K + QK_WIDTH]
        q_in = (q * jnp.exp(b - b_mid)).astype(BF16)
        k_in = (k * jnp.exp(b_mid - b)).astype(BF16)
        q_st = (q * jnp.exp(b)).astype(BF16)
        k_st = (k * jnp.exp(b_last - b)).astype(BF16)
        d_last = jnp.exp(b_last)
        va = z_ref[rows, C_VA:C_VA + GLA_WIDTH].astype(BF16)
        for h in range(GLA_HEADS):
            ps = slice((h // 2) * LANES, (h // 2 + 1) * LANES)
            vs = slice(h * GLA_DV, (h + 1) * GLA_DV)
            hm = head_lanes[h % 2]
            zero = jnp.zeros((), BF16)
            a = _dot_nt(jnp.where(hm, q_in[:, ps], zero), k_in[:, ps])
            a = jnp.where(causal, a, 0.0).astype(BF16)
            st = st_ref[h]
            o = _dot(a, va[:, vs]) + _dot_nt(jnp.where(hm, q_st[:, ps], zero), st.astype(BF16))
            upd = _dot_tn(va[:, vs], jnp.where(hm, k_st[:, ps], zero))
            st_ref[h] = st * d_last[:, ps] + upd
            gate = z_ref[rows, C_G + h * GLA_DV:C_G + (h + 1) * GLA_DV]
            o = _rmsnorm(o, glag_ref[h:h + 1, :]) * _silu(gate)
            merged_ref[rows, vs] = o.astype(BF16)
        u = _gelu(z_ref[rows, C_U:C_U + GMLP_WIDTH])
        vg = _layernorm(_gelu(z_ref[rows, C_VB:C_VB + GMLP_WIDTH]), lng_ref[...], lnb_ref[...])
        _gmlp_heads(u, vg, lambda h: jnp.where(causal, ws_ref[h], 0.0).astype(BF16), bsf_ref[...],
                    gmg_ref, merged_ref, rows)

    h_ref[...] = _out_proj_ln(x, merged_ref, wout_ref, ln1g_ref, ln1b_ref).reshape(PROMPT_SEQS, PROMPT_TILE, D_MODEL)

    @pl.when(t == pl.num_programs(1) - 1)
    def _():
        for s, st_ref in enumerate(st_refs):
            for h in range(GLA_HEADS):
                lo = (h % 2) * GLA_DK
                state_ref[s, h] = st_ref[h].T[lo:lo + GLA_DK, :]


def _mixer_sample_kernel(seq_len, x_ref, s0_ref, win_ref, wkt_ref, walrt_ref, waup_ref, waupt_ref, ba_ref, bac_ref,
                         glag_ref, lng_ref, lnb_ref, wst_ref, bsf_ref, gmg_ref, wout_ref, ln1g_ref, ln1b_ref,
                         h_ref, snew_ref, vg_ref,
                         merged_ref):
    n = SAMPLE_SEQS * seq_len
    x = x_ref[...]
    xb = x.astype(BF16)
    z = _dot(xb, win_ref[...])

    ti = lax.broadcasted_iota(jnp.int32, (n, n), 0)
    tj = lax.broadcasted_iota(jnp.int32, (n, n), 1)
    same = _shr(ti, seq_len) == _shr(tj, seq_len)
    causal = jnp.logical_and(same, ti >= tj)
    tri = jnp.where(causal, 1.0, 0.0).astype(BF16)
    tri_t = jnp.where(jnp.logical_and(same, ti <= tj), 1.0, 0.0).astype(BF16)
    same01 = jnp.where(same, 1.0, 0.0).astype(BF16)

    a_pre = _dot(z[:, C_ALR:C_ALR + LANES].astype(BF16), waup_ref[...]) + ba_ref[...]
    log_a = _log_sigmoid(a_pre) * (1.0 / GLA_TAU)
    b = _split_dot(tri, log_a)
    q = z[:, C_Q:C_Q + QK_WIDTH] * (GLA_DK ** -0.5)
    k = z[:, C_K:C_K + QK_WIDTH]
    q_in = (q * jnp.exp(b)).astype(BF16)
    k_in = (k * jnp.exp(-b)).astype(BF16)
    va = z[:, C_VA:C_VA + GLA_WIDTH].astype(BF16)

    k_t = _dot_nt(wkt_ref[...], xb)
    alr_t = _dot_nt(walrt_ref[...], xb)
    a_pre_t = _dot(waupt_ref[...], alr_t.astype(BF16)) + bac_ref[...]
    log_a_t = _log_sigmoid(a_pre_t) * (1.0 / GLA_TAU)
    b_t = _split_dot_r(log_a_t, tri_t)
    tot_t = _split_dot_r(log_a_t, same01)
    k_st_t = k_t * jnp.exp(tot_t - b_t)
    d_t = jnp.exp(tot_t)

    lane = lax.broadcasted_iota(jnp.int32, (1, LANES), 1)
    head_lanes = [lane < GLA_DK, lane >= GLA_DK]
    nrow = SAMPLE_SEQS * GLA_DK
    r_seq = _shr(lax.broadcasted_iota(jnp.int32, (nrow, n), 0), GLA_DK)
    c_seq = _shr(lax.broadcasted_iota(jnp.int32, (nrow, n), 1), seq_len)
    c_first = (lax.broadcasted_iota(jnp.int32, (nrow, n), 1) & (seq_len - 1)) == 0
    blk = r_seq == c_seq
    blk_first = jnp.logical_and(blk, c_first)
    q_seq = _shr(lax.broadcasted_iota(jnp.int32, (n, nrow), 0), seq_len)
    q_col = _shr(lax.broadcasted_iota(jnp.int32, (n, nrow), 1), GLA_DK)
    blk_q = q_seq == q_col

    for h in range(GLA_HEADS):
        ps = slice((h // 2) * LANES, (h // 2 + 1) * LANES)
        ds_ = slice(h * GLA_DK, (h + 1) * GLA_DK)
        vs = slice(h * GLA_DV, (h + 1) * GLA_DV)
        hm = head_lanes[h % 2]
        zero = jnp.zeros((), BF16)
        a = _dot_nt(jnp.where(hm, q_in[:, ps], zero), k_in[:, ps])
        a = jnp.where(causal, a, 0.0).astype(BF16)
        s0 = s0_ref[:, h].reshape(nrow, GLA_DV)
        q_h = q_in[:, ds_]
        q_bd = jnp.where(blk_q, jnp.concatenate([q_h] * SAMPLE_SEQS, axis=1), zero)
        o = _dot(a, va[:, vs]) + _dot(q_bd, s0.astype(BF16))
        k_bd = jnp.where(blk, jnp.concatenate([k_st_t[ds_, :]] * SAMPLE_SEQS, axis=0), 0.0).astype(BF16)
        upd = _dot(k_bd, va[:, vs])
        d_bd = jnp.where(blk_first, jnp.concatenate([d_t[ds_, :]] * SAMPLE_SEQS, axis=0), 0.0)
        d_col = jnp.sum(d_bd, axis=1, keepdims=True)
        snew_ref[:, h] = (s0 * d_col + upd).reshape(SAMPLE_SEQS, GLA_DK, GLA_DV)
        gate = z[:, C_G + h * GLA_DV:C_G + (h + 1) * GLA_DV]
        o = _rmsnorm(o, glag_ref[h:h + 1, :]) * _silu(gate)
        merged_ref[:, vs] = o.astype(BF16)

    u = _gelu(z[:, C_U:C_U + GMLP_WIDTH])
    vg = _layernorm(_gelu(z[:, C_VB:C_VB + GMLP_WIDTH]), lng_ref[...], lnb_ref[...])
    vg_ref[...] = vg
    pi = lax.broadcasted_iota(jnp.int32, (n, LANES), 0)
    pj = lax.broadcasted_iota(jnp.int32, (n, LANES), 1)
    pos = jnp.where((pi & (seq_len - 1)) == pj, 1.0, 0.0).astype(BF16)

    def w_mix(h):
        rows = _dot(pos, wst_ref[h].astype(BF16)).astype(BF16)
        return jnp.where(causal, _dot_nt(rows, pos), 0.0).astype(BF16)

    _gmlp_heads(u, vg, w_mix, bsf_ref[...], gmg_ref, merged_ref, slice(None))
    h_ref[...] = _out_proj_ln(x, merged_ref, wout_ref, ln1g_ref, ln1b_ref)


def _route(hb, wrt_ref, rbias_ref):
    tm = hb.shape[0]
    s = _sigmoid(_dot_nt(wrt_ref[...], hb))
    sb = s + rbias_ref[...]
    neg = jnp.float32(-jnp.inf)
    sub = lax.broadcasted_iota(jnp.int32, (GROUP_SIZE, tm), 0)
    gscore = []
    for g in range(N_GROUPS):
        blk = sb[g * GROUP_SIZE:(g + 1) * GROUP_SIZE, :]
        m1 = jnp.max(blk, axis=0, keepdims=True)
        i1 = jnp.min(jnp.where(blk == m1, sub, GROUP_SIZE), axis=0, keepdims=True)
        m2 = jnp.max(jnp.where(sub == i1, neg, blk), axis=0, keepdims=True)
        gscore.append(m1 + m2)
    gsel = []
    for g in range(N_GROUPS):
        rank = jnp.zeros((1, tm), jnp.int32)
        for o in range(N_GROUPS):
            if o == g:
                continue
            ahead = (gscore[o] >= gscore[g]) if o < g else (gscore[o] > gscore[g])
            rank = rank + ahead.astype(jnp.int32)
        gsel.append(rank < TOPK_GROUPS)
    v = jnp.concatenate(
        [jnp.where(gsel[g], sb[g * GROUP_SIZE:(g + 1) * GROUP_SIZE, :], neg) for g in range(N_GROUPS)], axis=0)
    eidx = lax.broadcasted_iota(jnp.int32, (N_EXPERTS, tm), 0)
    sel = jnp.zeros((N_EXPERTS, tm), F32)
    for _ in range(TOP_K):
        m = jnp.max(v, axis=0, keepdims=True)
        first = jnp.min(jnp.where(v == m, eidx, N_EXPERTS), axis=0, keepdims=True)
        hit = eidx == first
        sel = jnp.where(hit, 1.0, sel)
        v = jnp.where(hit, neg, v)
    wsel = s * sel
    return wsel / jnp.sum(wsel, axis=0, keepdims=True) * ROUTE_SCALE, sel


def _per_row(chunk_vals):
    n, t = chunk_vals.shape
    return jnp.broadcast_to(chunk_vals[:, None, :], (n, SEG, t)).reshape(n * SEG, t)


def _row_lookup(seg_b, start_b, rank, tables):
    n_chunks = BLOCK_ROWS // SEG
    ei = lax.broadcasted_iota(jnp.int32, (N_EXPERTS, LANES), 0)
    ej = lax.broadcasted_iota(jnp.int32, (N_EXPERTS, LANES), 1)
    seg16 = (seg_b * (1.0 / SEG)).astype(BF16)
    start_row = _dot_tn(seg16, jnp.where(ei < ej, 1.0, 0.0).astype(BF16))[0:1, :]
    seg_row = _dot_tn(seg16, jnp.where(ei == ej, 1.0, 0.0).astype(BF16))[0:1, :]
    c = lax.broadcasted_iota(jnp.int32, (n_chunks, LANES), 0).astype(F32)
    owner = jnp.where(c >= start_row, jnp.where(c < start_row + seg_row, 1.0, 0.0), 0.0).astype(BF16)

    def lookup(tab):
        padded = jnp.concatenate([tab.astype(BF16), jnp.zeros((LANES - N_EXPERTS, tab.shape[1]), BF16)], axis=0)
        return _dot(owner, padded)

    first = SEG * (c - lookup(start_b * (1.0 / SEG)))[:, 0:1]
    rel = lookup(rank) - first
    sub = lax.broadcasted_iota(jnp.int32, (BLOCK_ROWS, rank.shape[1]), 0) & (SEG - 1)
    hits = _per_row(rel) == sub.astype(F32)
    return hits, [_per_row(lookup(tab)) for tab in tables]


def _dispatch_kernel(n_prompt_blocks, n_blocks, hp_ref, hs_ref, wrt_ref, rbias_ref, x_ref, rank_ref, comb_ref,
                     seg_ref, start_ref):
    i = pl.program_id(0)

    @pl.when(i < n_blocks)
    def _():
        _dispatch_block(i < n_prompt_blocks, hp_ref, hs_ref, wrt_ref, rbias_ref, x_ref, rank_ref, comb_ref, seg_ref,
                        start_ref)

    @pl.when(i >= n_blocks)
    def _():
        for ref in (x_ref, rank_ref, comb_ref, seg_ref, start_ref):
            ref[...] = jnp.zeros_like(ref)


def _dispatch_block(is_prompt, hp_ref, hs_ref, wrt_ref, rbias_ref, x_ref, rank_ref, comb_ref, seg_ref, start_ref):
    tm = BLOCK
    hb = jnp.where(is_prompt, hp_ref[...], hs_ref[...]).astype(BF16)
    comb, sel = _route(hb, wrt_ref, rbias_ref)
    ti = lax.broadcasted_iota(jnp.int32, (tm, tm), 0)
    tj = lax.broadcasted_iota(jnp.int32, (tm, tm), 1)
    before = jnp.where(ti < tj, 1.0, 0.0).astype(BF16)
    rank = _dot(sel.astype(BF16), before)
    rank = jnp.where(sel > 0.0, rank, -1.0)
    cnt = jnp.sum(sel, axis=1, keepdims=True)
    seg = jnp.floor((cnt + (SEG - 1.0)) * (1.0 / SEG)) * SEG
    ei = lax.broadcasted_iota(jnp.int32, (N_EXPERTS, N_EXPERTS), 0)
    ej = lax.broadcasted_iota(jnp.int32, (N_EXPERTS, N_EXPERTS), 1)
    below = jnp.where(ej < ei, 1.0, 0.0).astype(BF16)
    seg_b = jnp.broadcast_to(seg, (N_EXPERTS, LANES))
    start_b = _dot(below, seg_b.astype(BF16))
    hits, _ = _row_lookup(seg_b, start_b, rank, [])
    gather = jnp.where(hits, 1.0, 0.0).astype(BF16)
    for c in range(BLOCK_ROWS // GATHER_ROWS):
        rs = slice(c * GATHER_ROWS, (c + 1) * GATHER_ROWS)
        x_ref[rs, :] = _dot(gather[rs, :], hb).astype(BF16)
    rank_ref[0] = rank
    comb_ref[0] = comb
    seg_ref[0] = seg_b
    start_ref[0] = start_b


def _ffn_kernel(dump_base, nt_max, tile_expert_ref, tile_in_expert_ref, n_chunks_ref, n_tiles_ref,
                tab0_ref, tab1_ref, tab2_ref, x_hbm, wg_ref, wu_ref, wd_ref,
                o_hbm, ibuf, obuf, wgb, wub, wdb, in_sem, out_sem):
    i = pl.program_id(0)
    n_tiles = n_tiles_ref[0]
    tabs = (tab0_ref, tab1_ref, tab2_ref)

    def start_in(ahead, slot):
        exists = i + ahead < n_tiles
        first = jnp.where(exists, tile_in_expert_ref[jnp.minimum(i + ahead, nt_max - 1)] * TILE_CHUNKS, 0)
        srcs = [jnp.where(exists, tabs[ahead][0, 0, first + j], ZERO_CHUNK) for j in range(TILE_CHUNKS)]
        for j in range(TILE_CHUNKS):
            pltpu.make_async_copy(x_hbm.at[srcs[j]], ibuf.at[slot, j], in_sem.at[slot]).start()

    def start_out(slot):
        e = tile_expert_ref[i]
        first = tile_in_expert_ref[i] * TILE_CHUNKS
        n_real = n_chunks_ref[e] - first
        dsts = [jnp.where(j < n_real, tab0_ref[0, 0, first + j], dump_base + slot * TILE_CHUNKS + j)
                for j in range(TILE_CHUNKS)]
        for j in range(TILE_CHUNKS):
            pltpu.make_async_copy(obuf.at[slot, j], o_hbm.at[dsts[j]], out_sem.at[slot]).start()

    def in_wait(slot):
        for j in range(TILE_CHUNKS):
            pltpu.make_async_copy(x_hbm.at[0], ibuf.at[slot, j], in_sem.at[slot]).wait()

    def out_wait(slot):
        for j in range(TILE_CHUNKS):
            pltpu.make_async_copy(obuf.at[slot, j], o_hbm.at[0], out_sem.at[slot]).wait()

    @pl.when(i < n_tiles)
    def _():
        islot = lax.rem(i, IN_BUFS)
        nslot = jnp.where(islot == 0, IN_BUFS - 1, islot - 1)
        oslot = lax.rem(i, 2)

        @pl.when(i == 0)
        def _():
            start_in(0, 0)
            start_in(1, 1)

        @pl.when(i >= 2)
        def _():
            out_wait(oslot)

        in_wait(islot)

        @pl.when(tile_in_expert_ref[i] == 0)
        def _():
            wgb[...] = wg_ref[0].astype(BF16)
            wub[...] = wu_ref[0].astype(BF16)
            wdb[...] = wd_ref[0].astype(BF16)

        for s in range(TILE_CHUNKS // SUB_CHUNKS):
            cs = slice(s * SUB_CHUNKS, (s + 1) * SUB_CHUNKS)
            x = ibuf[islot, cs].reshape(SUB_CHUNKS * SEG, D_MODEL)
            hid = (_silu(_dot(x, wgb[...])) * _dot(x, wub[...])).astype(BF16)
            if s == 0:
                start_in(2, nslot)
            obuf[oslot, cs] = _dot(hid, wdb[...]).astype(BF16).reshape(SUB_CHUNKS, SEG, D_MODEL)
        start_out(oslot)

        @pl.when(i == n_tiles - 1)
        def _():
            in_wait(lax.rem(i + 1, IN_BUFS))
            in_wait(nslot)
            out_wait(oslot)

            @pl.when(i >= 1)
            def _():
                out_wait(1 - oslot)


def _combine_kernel(h_ref, o_ref, rank_ref, comb_ref, seg_ref, start_ref, wsg_ref, wsu_ref, wsd_ref, ln2g_ref,
                    ln2b_ref, y_ref):
    h = h_ref[...]
    hb = h.astype(BF16)
    hits, (comb_rows,) = _row_lookup(seg_ref[0], start_ref[0], rank_ref[0], [comb_ref[0]])
    scatter = jnp.where(hits, comb_rows, 0.0).astype(BF16)
    routed = _dot_tn(o_ref[...], scatter).T
    shared = _dot((_silu(_dot(hb, wsg_ref[...])) * _dot(hb, wsu_ref[...])).astype(BF16), wsd_ref[...])
    y_ref[...] = _layernorm(ALPHA * h + (routed + shared), ln2g_ref[...], ln2b_ref[...])


def _full(shape):
    return pl.BlockSpec(shape, lambda *_: (0,) * len(shape))


def _const(shape):
    return pl.BlockSpec(shape, lambda *_: (0,) * len(shape), pipeline_mode=pl.Buffered(1))


def _mixer_prompt(x, wts):
    bsz, seq, _ = x.shape
    n_t = seq // PROMPT_TILE
    weights = [wts[k] for k in ("win", "waup", "ba", "glag", "lng", "lnb", "ws", "bsf_prompt", "gmg", "wout",
                                "ln1g", "ln1b")]
    return pl.pallas_call(
        _mixer_prompt_kernel,
        grid=(bsz // PROMPT_SEQS, n_t),
        in_specs=[pl.BlockSpec((PROMPT_SEQS, PROMPT_TILE, D_MODEL), lambda b, t: (b, t, 0))]
        + [_const(w.shape) for w in weights],
        out_specs=[pl.BlockSpec((PROMPT_SEQS, PROMPT_TILE, D_MODEL), lambda b, t: (b, t, 0)),
                   pl.BlockSpec((PROMPT_SEQS, GLA_HEADS, GLA_DK, GLA_DV), lambda b, t: (b, 0, 0, 0))],
        out_shape=[jax.ShapeDtypeStruct((bsz, seq, D_MODEL), F32),
                   jax.ShapeDtypeStruct((bsz, GLA_HEADS, GLA_DK, GLA_DV), F32)],
        scratch_shapes=[pltpu.VMEM((PROMPT_SEQS * PROMPT_TILE, N_PROJ), F32),
                        pltpu.VMEM((PROMPT_SEQS * PROMPT_TILE, D_MODEL), BF16)]
        + [pltpu.VMEM((GLA_HEADS, GLA_DV, LANES), F32)] * PROMPT_SEQS,
        compiler_params=pltpu.CompilerParams(dimension_semantics=("arbitrary", "arbitrary"),
                                             vmem_limit_bytes=VMEM_LIMIT),
        name="mixer_prompt",
    )(x, *weights)


def _mixer_sample(x, s0, wts):
    bsz, seq_len, _ = x.shape
    n = SAMPLE_SEQS * seq_len
    x2 = x.reshape(bsz * seq_len, D_MODEL)
    weights = [wts[k] for k in ("win", "wkt", "walrt", "waup", "waupt", "ba", "bac", "glag", "lng", "lnb",
                                "ws_sample", "bsf_sample", "gmg", "wout", "ln1g", "ln1b")]
    state_spec = pl.BlockSpec((SAMPLE_SEQS, GLA_HEADS, GLA_DK, GLA_DV), lambda i: (i, 0, 0, 0))
    h, s_new, vg = pl.pallas_call(
        functools.partial(_mixer_sample_kernel, seq_len),
        grid=(bsz // SAMPLE_SEQS,),
        in_specs=[pl.BlockSpec((n, D_MODEL), lambda i: (i, 0)), state_spec] + [_full(w.shape) for w in weights],
        out_specs=[pl.BlockSpec((n, D_MODEL), lambda i: (i, 0)), state_spec,
                   pl.BlockSpec((n, GMLP_WIDTH), lambda i: (i, 0))],
        out_shape=[jax.ShapeDtypeStruct((bsz * seq_len, D_MODEL), F32),
                   jax.ShapeDtypeStruct(s0.shape, F32),
                   jax.ShapeDtypeStruct((bsz * seq_len, GMLP_WIDTH), F32)],
        scratch_shapes=[pltpu.VMEM((n, D_MODEL), BF16)],
        compiler_params=pltpu.CompilerParams(dimension_semantics=("arbitrary",), vmem_limit_bytes=VMEM_LIMIT),
        name="mixer_sample",
    )(x2, s0, *weights)
    return h, s_new, vg.reshape(bsz, seq_len, GMLP_WIDTH)


def _dispatch(h_p, h_s, wts):
    npb, nsb = h_p.shape[0] // BLOCK, h_s.shape[0] // BLOCK
    nb_real = npb + nsb
    nb = nb_real + DUMP_BLOCKS
    tok_spec = pl.BlockSpec((1, N_EXPERTS, BLOCK), lambda i: (i, 0, 0))
    run_spec = pl.BlockSpec((1, N_EXPERTS, LANES), lambda i: (i, 0, 0))
    return pl.pallas_call(
        functools.partial(_dispatch_kernel, npb, nb_real),
        grid=(nb,),
        in_specs=[pl.BlockSpec((BLOCK, D_MODEL), lambda i: (jnp.minimum(i, npb - 1), 0)),
                  pl.BlockSpec((BLOCK, D_MODEL), lambda i: (jnp.clip(i - npb, 0, nsb - 1), 0)),
                  _full(wts["wrt"].shape), _full(wts["rbias"].shape)],
        out_specs=[pl.BlockSpec((BLOCK_ROWS, D_MODEL), lambda i: (i, 0)), tok_spec, tok_spec, run_spec, run_spec],
        out_shape=[jax.ShapeDtypeStruct((nb * BLOCK_ROWS, D_MODEL), BF16),
                   jax.ShapeDtypeStruct((nb, N_EXPERTS, BLOCK), F32),
                   jax.ShapeDtypeStruct((nb, N_EXPERTS, BLOCK), F32),
                   jax.ShapeDtypeStruct((nb, N_EXPERTS, LANES), F32),
                   jax.ShapeDtypeStruct((nb, N_EXPERTS, LANES), F32)],
        compiler_params=pltpu.CompilerParams(dimension_semantics=("arbitrary",), vmem_limit_bytes=VMEM_LIMIT),
        name="moe_dispatch",
    )(h_p, h_s, wts["wrt"], wts["rbias"])


def _max_tiles(nb):
    return (nb * (BLOCK_ROWS_USED // SEG) + N_EXPERTS * (TILE_CHUNKS - 1)) // TILE_CHUNKS + 1


def _max_expert_chunks(nb):
    most = nb * (BLOCK // SEG) + TILE_CHUNKS - 1
    return -(-most // LANES) * LANES


def _plan(seg_lanes, nb):
    nt_max = _max_tiles(nb)
    max_chunks = _max_expert_chunks(nb)
    seg = seg_lanes[:nb, :, 0].astype(jnp.int32)
    start = jnp.cumsum(seg, axis=1) - seg
    nch = seg // SEG
    ends = jnp.cumsum(nch, axis=0)
    n_chunks = ends[-1]
    f = jnp.arange(nb, dtype=jnp.int32)[:, None] * (BLOCK_ROWS // SEG) + start // SEG - (ends - nch)
    q = jnp.arange(max_chunks, dtype=jnp.int32)
    passed = (ends[None, :-1, :] <= q[:, None, None]).astype(jnp.int32)
    chunk = q[:, None] + f[0][None, :] + jnp.sum(passed * (f[1:] - f[:-1])[None], axis=1)
    table = jnp.where(q[:, None] < n_chunks[None, :], chunk, ZERO_CHUNK).T
    n_tiles_e = -(-n_chunks // TILE_CHUNKS)
    tile_ends = jnp.cumsum(n_tiles_e)
    t = jnp.arange(nt_max, dtype=jnp.int32)
    done = (tile_ends[None, :] <= t[:, None]).astype(jnp.int32)
    tile_expert = jnp.minimum(jnp.sum(done, axis=1), N_EXPERTS - 1)
    tile_in_expert = t - jnp.sum(done * n_tiles_e[None, :], axis=1)
    return (table.astype(jnp.int32).reshape(N_EXPERTS, 1, max_chunks), tile_expert.astype(jnp.int32),
            tile_in_expert.astype(jnp.int32), n_chunks.astype(jnp.int32), tile_ends[-1:].astype(jnp.int32))


def _ffn(x_rows, nb, table, tile_expert, tile_in_expert, n_chunks, n_tiles, w_gate, w_up, w_down):
    nt_max = tile_expert.shape[0]
    max_chunks = table.shape[-1]
    tab_spec = lambda ahead: pl.BlockSpec(
        (1, 1, max_chunks), lambda i, te, *_: (te[jnp.clip(i + ahead, 0, nt_max - 1)], 0, 0),
        memory_space=pltpu.SMEM)
    w_spec = lambda shape: pl.BlockSpec((1,) + shape, lambda i, te, *_: (te[i], 0, 0))
    assert x_rows.shape[0] - nb * BLOCK_ROWS >= 2 * TILE_ROWS
    x_chunks = x_rows.reshape(x_rows.shape[0] // SEG, SEG, D_MODEL)
    return pl.pallas_call(
        functools.partial(_ffn_kernel, nb * BLOCK_ROWS // SEG, nt_max),
        grid_spec=pltpu.PrefetchScalarGridSpec(
            num_scalar_prefetch=4,
            grid=(nt_max,),
            in_specs=[tab_spec(0), tab_spec(1), tab_spec(2), pl.BlockSpec(memory_space=pl.ANY),
                      w_spec((D_MODEL, D_EXPERT)), w_spec((D_MODEL, D_EXPERT)), w_spec((D_EXPERT, D_MODEL))],
            out_specs=pl.BlockSpec(memory_space=pl.ANY),
            scratch_shapes=[pltpu.VMEM((IN_BUFS, TILE_CHUNKS, SEG, D_MODEL), BF16),
                            pltpu.VMEM((2, TILE_CHUNKS, SEG, D_MODEL), BF16),
                            pltpu.VMEM((D_MODEL, D_EXPERT), BF16),
                            pltpu.VMEM((D_MODEL, D_EXPERT), BF16),
                            pltpu.VMEM((D_EXPERT, D_MODEL), BF16),
                            pltpu.SemaphoreType.DMA((IN_BUFS,)),
                            pltpu.SemaphoreType.DMA((2,))]),
        out_shape=jax.ShapeDtypeStruct(x_chunks.shape, BF16),
        input_output_aliases={7: 0},
        compiler_params=pltpu.CompilerParams(dimension_semantics=("arbitrary",), vmem_limit_bytes=VMEM_LIMIT),
        name="moe_ffn",
    )(tile_expert, tile_in_expert, n_chunks, n_tiles, table, table, table, x_chunks, w_gate, w_up, w_down
      ).reshape(x_rows.shape)


def _combine(h, o_rows, routing, block_off, wts):
    nblk = h.shape[0] // BLOCK
    weights = [wts[k] for k in ("wsg", "wsu", "wsd", "ln2g", "ln2b")]
    tok_spec = pl.BlockSpec((1, N_EXPERTS, BLOCK), lambda i: (i + block_off, 0, 0))
    run_spec = pl.BlockSpec((1, N_EXPERTS, LANES), lambda i: (i + block_off, 0, 0))
    return pl.pallas_call(
        _combine_kernel,
        grid=(nblk,),
        in_specs=[pl.BlockSpec((BLOCK, D_MODEL), lambda i: (i, 0)),
                  pl.BlockSpec((BLOCK_ROWS, D_MODEL), lambda i: (i + block_off, 0)),
                  tok_spec, tok_spec, run_spec, run_spec] + [_full(w.shape) for w in weights],
        out_specs=pl.BlockSpec((BLOCK, D_MODEL), lambda i: (i, 0)),
        out_shape=jax.ShapeDtypeStruct(h.shape, F32),
        compiler_params=pltpu.CompilerParams(dimension_semantics=("arbitrary",), vmem_limit_bytes=VMEM_LIMIT),
        name="moe_combine",
    )(h, o_rows, *routing, *weights)


def _moe(h_p, h_s, w_gate, w_up, w_down, wts):
    nb = (h_p.shape[0] + h_s.shape[0]) // BLOCK
    x_rows, rank, comb, seg_lanes, start_lanes = _dispatch(h_p, h_s, wts)
    o_rows = _ffn(x_rows, nb, *_plan(seg_lanes, nb), w_gate, w_up, w_down)
    routing = (rank, comb, seg_lanes, start_lanes)
    y_p = _combine(h_p, o_rows, routing, 0, wts)
    y_s = _combine(h_s, o_rows, routing, h_p.shape[0] // BLOCK, wts)
    return y_p, y_s


def _prep_weights(seq_len, w_in, w_a_up, b_a, gla_norm_g, gmlp_ln_g, gmlp_ln_b, w_s, b_s, gmlp_norm_g, w_out,
                  ln1_g, ln1_b, w_router, router_bias, ws_gate, ws_up, ws_down, ln2_g, ln2_b):
    o1 = QK_WIDTH
    o2 = o1 + QK_WIDTH
    o3 = o2 + GLA_WIDTH
    o4 = o3 + GLA_WIDTH
    o5 = o4 + GLA_RANK
    o6 = o5 + GMLP_WIDTH
    wq, wk, wva, wg_, walr, wu_, wvb = jnp.split(w_in, [o1, o2, o3, o4, o5, o6], axis=-1)
    walr_p = jnp.pad(walr, ((0, 0), (0, LANES - GLA_RANK)))
    waup_p = jnp.pad(w_a_up, ((0, LANES - GLA_RANK), (0, 0)))
    row = lambda a: a.reshape(1, -1)
    reps = (SAMPLE_SEQS * seq_len) // seq_len
    ws_small = w_s[:, :seq_len, :seq_len]
    return {
        "win": jnp.concatenate([wq, wk, wva, wg_, wu_, wvb, walr_p], axis=-1).astype(BF16),
        "wkt": wk.T.astype(BF16),
        "walrt": walr_p.T.astype(BF16),
        "waup": waup_p.astype(BF16),
        "waupt": waup_p.T.astype(BF16),
        "ba": row(b_a), "bac": b_a.reshape(-1, 1),
        "glag": gla_norm_g, "lng": row(gmlp_ln_g), "lnb": row(gmlp_ln_b),
        "ws": w_s,
        "ws_sample": jnp.pad(ws_small, ((0, 0), (0, LANES - seq_len), (0, LANES - seq_len))),
        "bsf_prompt": jnp.repeat(b_s[:, :GMLP_CHUNK].T, GMLP_DH, axis=1),
        "bsf_sample": jnp.tile(jnp.repeat(b_s[:, :seq_len].T, GMLP_DH, axis=1), (reps, 1)),
        "gmg": gmlp_norm_g,
        "wout": w_out.astype(BF16),
        "ln1g": row(ln1_g), "ln1b": row(ln1_b),
        "wrt": w_router.T.astype(BF16), "rbias": router_bias.reshape(-1, 1),
        "wsg": ws_gate.astype(BF16), "wsu": ws_up.astype(BF16), "wsd": ws_down.astype(BF16),
        "ln2g": row(ln2_g), "ln2b": row(ln2_b),
    }


def kernel(x_prompt, x_sample, state_gla, w_in, w_a_up, b_a, gla_norm_g, gmlp_ln_g, gmlp_ln_b, w_s, b_s,
           gmlp_norm_g, w_out, ln1_g, ln1_b, w_router, router_bias, w_gate, w_up, w_down, ws_gate, ws_up,
           ws_down, ln2_g, ln2_b):
    assert x_prompt.shape[1] % PROMPT_TILE == 0 and x_sample.shape[0] % SAMPLE_SEQS == 0
    assert x_prompt.shape[0] % PROMPT_SEQS == 0
    assert x_sample.shape[1] <= GMLP_CHUNK and w_in.shape[0] == DEPTH
    assert (x_sample.shape[0] * x_sample.shape[1]) % BLOCK == 0 and PROMPT_TILE % BLOCK == 0
    bsz, seq, _ = x_prompt.shape
    dbsz, dseq, _ = x_sample.shape
    hp, hs = x_prompt, x_sample
    gla_p, gla_s, v_s = [], [], []
    for l in range(DEPTH):
        wts = _prep_weights(dseq, w_in[l], w_a_up[l], b_a[l], gla_norm_g[l], gmlp_ln_g[l], gmlp_ln_b[l], w_s[l],
                            b_s[l], gmlp_norm_g[l], w_out[l], ln1_g[l], ln1_b[l], w_router[l], router_bias[l],
                            ws_gate[l], ws_up[l], ws_down[l], ln2_g[l], ln2_b[l])
        h_p, sp = _mixer_prompt(hp, wts)
        h_s, ss, vrows = _mixer_sample(hs, state_gla[l], wts)
        y_p, y_s = _moe(h_p.reshape(bsz * seq, D_MODEL), h_s, w_gate[l], w_up[l], w_down[l], wts)
        hp = y_p.reshape(bsz, seq, D_MODEL)
        hs = y_s.reshape(dbsz, dseq, D_MODEL)
        gla_p.append(sp)
        gla_s.append(ss)
        v_s.append(vrows)
    return (hp, hs, jnp.stack(gla_p), jnp.stack(gla_s), jnp.stack(v_s))
```

```python
import functools
import math

import jax
import jax.numpy as jnp
from jax import lax
from jax.experimental import pallas as pl
from jax.experimental.pallas import tpu as pltpu

F32 = jnp.float32
BF16 = jnp.bfloat16

D_MODEL = 1024
DEPTH = 1
GLA_WIDTH = 512
GLA_HEADS = 4
GLA_DK = 64
GLA_DV = 128
GLA_RANK = 16
GLA_TAU = 16.0
GMLP_WIDTH = 512
GMLP_HEADS = 4
GMLP_DH = 128
GMLP_CHUNK = 128
QK_WIDTH = GLA_HEADS * GLA_DK
N_EXPERTS = 64
TOP_K = 8
N_GROUPS = 8
GROUP_SIZE = N_EXPERTS // N_GROUPS
TOPK_GROUPS = 4
D_EXPERT = 256
D_SHARED = 256
ROUTE_SCALE = 2.5
ALPHA = (2.0 * DEPTH) ** 0.25

LANES = 128

C_Q = 0
C_K = C_Q + QK_WIDTH
C_VA = C_K + QK_WIDTH
C_G = C_VA + GLA_WIDTH
C_U = C_G + GLA_WIDTH
C_VB = C_U + GMLP_WIDTH
C_ALR = C_VB + GMLP_WIDTH
N_PROJ = C_ALR + LANES

CHUNK = 128
GLA_BLOCK = 256
PROMPT_TILE = 512
PROMPT_SEQS = 2
SAMPLE_SEQS = 32
VMEM_LIMIT = 56 * 1024 * 1024

BLOCK = 256
SEG = 16
TILE_CHUNKS = 64
GROUP_CHUNKS = 16
TILE_ROWS = TILE_CHUNKS * SEG
GATHER_ROWS = 512
BLOCK_ROWS_USED = BLOCK * TOP_K + N_EXPERTS * (SEG - 1)
BLOCK_ROWS = -(-(BLOCK_ROWS_USED + SEG) // GATHER_ROWS) * GATHER_ROWS
DUMP_BLOCKS = -(-(2 * TILE_ROWS) // BLOCK_ROWS)
ZERO_CHUNK = BLOCK_ROWS_USED // SEG
IN_BUFS = 4


def _dot(a, b):
    return jnp.dot(a, b, preferred_element_type=F32)


def _dot_nt(a, b):
    return lax.dot_general(a, b, (((1,), (1,)), ((), ())), preferred_element_type=F32)


def _dot_tn(a, b):
    return lax.dot_general(a, b, (((0,), (0,)), ((), ())), preferred_element_type=F32)


def _shr(x, d):
    assert d & (d - 1) == 0
    return lax.shift_right_logical(x, d.bit_length() - 1)


def _split_dot(m01, x):
    hi = x.astype(BF16)
    lo = (x - hi.astype(F32)).astype(BF16)
    return _dot(m01, hi) + _dot(m01, lo)


def _split_dot_r(x, m01):
    hi = x.astype(BF16)
    lo = (x - hi.astype(F32)).astype(BF16)
    return _dot(hi, m01) + _dot(lo, m01)


def _sigmoid(x):
    return 1.0 / (1.0 + jnp.exp(-x))


def _silu(x):
    return x * _sigmoid(x)


def _gelu(x):
    c = math.sqrt(2.0 / math.pi)
    return x * (0.5 * (1.0 + jnp.tanh(c * (x + 0.044715 * (x * x * x)))))


def _log_sigmoid(x):
    return -(jnp.maximum(-x, 0.0) + jnp.log(1.0 + jnp.exp(-jnp.abs(x))))


def _layernorm(x, g, b, eps=1e-5):
    mu = jnp.mean(x, axis=-1, keepdims=True)
    xc = x - mu
    var = jnp.mean(xc * xc, axis=-1, keepdims=True)
    return xc * lax.rsqrt(var + eps) * g + b


def _rmsnorm(x, g, eps=1e-6):
    return x * lax.rsqrt(jnp.mean(x * x, axis=-1, keepdims=True) + eps) * g


def _gmlp_heads(u, vg, w_mix, bias, gmg_ref, merged_ref, rows):
    vgb = vg.astype(BF16)
    for h in range(GMLP_HEADS):
        cs = slice(h * GMLP_DH, (h + 1) * GMLP_DH)
        sgu = _dot(w_mix(h), vgb[:, cs]) + bias[:, cs]
        y = _rmsnorm(u[:, cs] * sgu, gmg_ref[h:h + 1, :])
        merged_ref[rows, GLA_WIDTH + h * GMLP_DH:GLA_WIDTH + (h + 1) * GMLP_DH] = y.astype(BF16)


def _out_proj_ln(x, merged_ref, wout_ref, ln1g_ref, ln1b_ref):
    m = _dot(merged_ref[...], wout_ref[...])
    return _layernorm(ALPHA * x + m, ln1g_ref[...], ln1b_ref[...])


def _mixer_prompt_kernel(x_ref, win_ref, waup_ref, ba_ref, glag_ref, lng_ref, lnb_ref, ws_ref, bsf_ref,
                         gmg_ref, wout_ref, ln1g_ref, ln1b_ref,
                         h_ref, state_ref,
                         z_ref, merged_ref, *st_refs):
    t = pl.program_id(1)

    @pl.when(t == 0)
    def _():
        for st_ref in st_refs:
            st_ref[...] = jnp.zeros_like(st_ref)

    x = x_ref[...].reshape(PROMPT_SEQS * PROMPT_TILE, D_MODEL)
    z_ref[...] = _dot(x.astype(BF16), win_ref[...])

    row_i = lax.broadcasted_iota(jnp.int32, (GLA_BLOCK, GLA_BLOCK), 0)
    col_i = lax.broadcasted_iota(jnp.int32, (GLA_BLOCK, GLA_BLOCK), 1)
    causal = row_i >= col_i
    tri = jnp.where(causal, 1.0, 0.0).astype(BF16)
    causal_mix = causal[:CHUNK, :CHUNK]
    lane = lax.broadcasted_iota(jnp.int32, (1, LANES), 1)
    head_lanes = [lane < GLA_DK, lane >= GLA_DK]
    mid = GLA_BLOCK // 2 - 1

    for c in range(PROMPT_SEQS * PROMPT_TILE // GLA_BLOCK):
        rows = slice(c * GLA_BLOCK, (c + 1) * GLA_BLOCK)
        st_ref = st_refs[c // (PROMPT_TILE // GLA_BLOCK)]
        a_pre = _dot(z_ref[rows, C_ALR:C_ALR + LANES].astype(BF16), waup_ref[...]) + ba_ref[...]
        log_a = _log_sigmoid(a_pre) * (1.0 / GLA_TAU)
        b = _split_dot(tri, log_a)
        b_mid = b[mid:mid + 1, :]
        b_last = b[GLA_BLOCK - 1:GLA_BLOCK, :]
        q = z_ref[rows, C_Q:C_Q + QK_WIDTH] * (GLA_DK ** -0.5)
        k = z_ref[rows, C_K:C_K + QK_WIDTH]
        q_in = (q * jnp.exp(b - b_mid)).astype(BF16)
        k_in = (k * jnp.exp(b_mid - b)).astype(BF16)
        q_st = (q * jnp.exp(b)).astype(BF16)
        k_st = (k * jnp.exp(b_last - b)).astype(BF16)
        d_last = jnp.exp(b_last)
        va = z_ref[rows, C_VA:C_VA + GLA_WIDTH].astype(BF16)
        for h in range(GLA_HEADS):
            ps = slice((h // 2) * LANES, (h // 2 + 1) * LANES)
            vs = slice(h * GLA_DV, (h + 1) * GLA_DV)
            hm = head_lanes[h % 2]
            zero = jnp.zeros((), BF16)
            a = _dot_nt(jnp.where(hm, q_in[:, ps], zero), k_in[:, ps])
            a = jnp.where(causal, a, 0.0).astype(BF16)
            st = st_ref[h]
            o = _dot(a, va[:, vs]) + _dot_nt(jnp.where(hm, q_st[:, ps], zero), st.astype(BF16))
            upd = _dot_tn(va[:, vs], jnp.where(hm, k_st[:, ps], zero))
            st_ref[h] = st * d_last[:, ps] + upd
            gate = z_ref[rows, C_G + h * GLA_DV:C_G + (h + 1) * GLA_DV]
            o = _rmsnorm(o, glag_ref[h:h + 1, :]) * _silu(gate)
            merged_ref[rows, vs] = o.astype(BF16)

    for c in range(PROMPT_SEQS * PROMPT_TILE // CHUNK):
        rows = slice(c * CHUNK, (c + 1) * CHUNK)
        u = _gelu(z_ref[rows, C_U:C_U + GMLP_WIDTH])
        vg = _layernorm(_gelu(z_ref[rows, C_VB:C_VB + GMLP_WIDTH]), lng_ref[...], lnb_ref[...])
        _gmlp_heads(u, vg, lambda h: jnp.where(causal_mix, ws_ref[h], 0.0).astype(BF16), bsf_ref[...],
                    gmg_ref, merged_ref, rows)

    h_ref[...] = _out_proj_ln(x, merged_ref, wout_ref, ln1g_ref, ln1b_ref).reshape(PROMPT_SEQS, PROMPT_TILE, D_MODEL)

    @pl.when(t == pl.num_programs(1) - 1)
    def _():
        for s, st_ref in enumerate(st_refs):
            for h in range(GLA_HEADS):
                lo = (h % 2) * GLA_DK
                state_ref[s, h] = st_ref[h].T[lo:lo + GLA_DK, :]


def _mixer_sample_kernel(seq_len, x_ref, s0_ref, win_ref, wkt_ref, walrt_ref, waup_ref, waupt_ref, ba_ref, bac_ref,
                         glag_ref, lng_ref, lnb_ref, wst_ref, bsf_ref, gmg_ref, wout_ref, ln1g_ref, ln1b_ref,
                         h_ref, snew_ref, vg_ref,
                         merged_ref):
    n = SAMPLE_SEQS * seq_len
    x = x_ref[...]
    xb = x.astype(BF16)
    z = _dot(xb, win_ref[...])

    ti = lax.broadcasted_iota(jnp.int32, (n, n), 0)
    tj = lax.broadcasted_iota(jnp.int32, (n, n), 1)
    same = _shr(ti, seq_len) == _shr(tj, seq_len)
    causal = jnp.logical_and(same, ti >= tj)
    tri = jnp.where(causal, 1.0, 0.0).astype(BF16)
    tri_t = jnp.where(jnp.logical_and(same, ti <= tj), 1.0, 0.0).astype(BF16)
    same01 = jnp.where(same, 1.0, 0.0).astype(BF16)

    a_pre = _dot(z[:, C_ALR:C_ALR + LANES].astype(BF16), waup_ref[...]) + ba_ref[...]
    log_a = _log_sigmoid(a_pre) * (1.0 / GLA_TAU)
    b = _split_dot(tri, log_a)
    q = z[:, C_Q:C_Q + QK_WIDTH] * (GLA_DK ** -0.5)
    k = z[:, C_K:C_K + QK_WIDTH]
    q_in = (q * jnp.exp(b)).astype(BF16)
    k_in = (k * jnp.exp(-b)).astype(BF16)
    va = z[:, C_VA:C_VA + GLA_WIDTH].astype(BF16)

    k_t = _dot_nt(wkt_ref[...], xb)
    alr_t = _dot_nt(walrt_ref[...], xb)
    a_pre_t = _dot(waupt_ref[...], alr_t.astype(BF16)) + bac_ref[...]
    log_a_t = _log_sigmoid(a_pre_t) * (1.0 / GLA_TAU)
    b_t = _split_dot_r(log_a_t, tri_t)
    tot_t = _split_dot_r(log_a_t, same01)
    k_st_t = k_t * jnp.exp(tot_t - b_t)
    d_t = jnp.exp(tot_t)

    lane = lax.broadcasted_iota(jnp.int32, (1, LANES), 1)
    head_lanes = [lane < GLA_DK, lane >= GLA_DK]
    nrow = SAMPLE_SEQS * GLA_DK
    r_seq = _shr(lax.broadcasted_iota(jnp.int32, (nrow, n), 0), GLA_DK)
    c_seq = _shr(lax.broadcasted_iota(jnp.int32, (nrow, n), 1), seq_len)
    c_first = (lax.broadcasted_iota(jnp.int32, (nrow, n), 1) & (seq_len - 1)) == 0
    blk = r_seq == c_seq
    blk_first = jnp.logical_and(blk, c_first)
    q_seq = _shr(lax.broadcasted_iota(jnp.int32, (n, nrow), 0), seq_len)
    q_col = _shr(lax.broadcasted_iota(jnp.int32, (n, nrow), 1), GLA_DK)
    blk_q = q_seq == q_col

    for h in range(GLA_HEADS):
        ps = slice((h // 2) * LANES, (h // 2 + 1) * LANES)
        ds_ = slice(h * GLA_DK, (h + 1) * GLA_DK)
        vs = slice(h * GLA_DV, (h + 1) * GLA_DV)
        hm = head_lanes[h % 2]
        zero = jnp.zeros((), BF16)
        a = _dot_nt(jnp.where(hm, q_in[:, ps], zero), k_in[:, ps])
        a = jnp.where(causal, a, 0.0).astype(BF16)
        s0 = s0_ref[:, h].reshape(nrow, GLA_DV)
        q_h = q_in[:, ds_]
        q_bd = jnp.where(blk_q, jnp.concatenate([q_h] * SAMPLE_SEQS, axis=1), zero)
        o = _dot(a, va[:, vs]) + _dot(q_bd, s0.astype(BF16))
        k_bd = jnp.where(blk, jnp.concatenate([k_st_t[ds_, :]] * SAMPLE_SEQS, axis=0), 0.0).astype(BF16)
        upd = _dot(k_bd, va[:, vs])
        d_bd = jnp.where(blk_first, jnp.concatenate([d_t[ds_, :]] * SAMPLE_SEQS, axis=0), 0.0)
        d_col = jnp.sum(d_bd, axis=1, keepdims=True)
        snew_ref[:, h] = (s0 * d_col + upd).reshape(SAMPLE_SEQS, GLA_DK, GLA_DV)
        gate = z[:, C_G + h * GLA_DV:C_G + (h + 1) * GLA_DV]
        o = _rmsnorm(o, glag_ref[h:h + 1, :]) * _silu(gate)
        merged_ref[:, vs] = o.astype(BF16)

    u = _gelu(z[:, C_U:C_U + GMLP_WIDTH])
    vg = _layernorm(_gelu(z[:, C_VB:C_VB + GMLP_WIDTH]), lng_ref[...], lnb_ref[...])
    vg_ref[...] = vg
    pi = lax.broadcasted_iota(jnp.int32, (n, LANES), 0)
    pj = lax.broadcasted_iota(jnp.int32, (n, LANES), 1)
    pos = jnp.where((pi & (seq_len - 1)) == pj, 1.0, 0.0).astype(BF16)

    def w_mix(h):
        rows = _dot(pos, wst_ref[h].astype(BF16)).astype(BF16)
        return jnp.where(causal, _dot_nt(rows, pos), 0.0).astype(BF16)

    _gmlp_heads(u, vg, w_mix, bsf_ref[...], gmg_ref, merged_ref, slice(None))
    h_ref[...] = _out_proj_ln(x, merged_ref, wout_ref, ln1g_ref, ln1b_ref)


def _route(hb, wrt_ref, rbias_ref):
    tm = hb.shape[0]
    s = _sigmoid(_dot_nt(wrt_ref[...], hb))
    sb = s + rbias_ref[...]
    neg = jnp.float32(-jnp.inf)
    sub = lax.broadcasted_iota(jnp.int32, (GROUP_SIZE, tm), 0)
    gscore = []
    for g in range(N_GROUPS):
        blk = sb[g * GROUP_SIZE:(g + 1) * GROUP_SIZE, :]
        m1 = jnp.max(blk, axis=0, keepdims=True)
        i1 = jnp.min(jnp.where(blk == m1, sub, GROUP_SIZE), axis=0, keepdims=True)
        m2 = jnp.max(jnp.where(sub == i1, neg, blk), axis=0, keepdims=True)
        gscore.append(m1 + m2)
    gsel = []
    for g in range(N_GROUPS):
        rank = jnp.zeros((1, tm), jnp.int32)
        for o in range(N_GROUPS):
            if o == g:
                continue
            ahead = (gscore[o] >= gscore[g]) if o < g else (gscore[o] > gscore[g])
            rank = rank + ahead.astype(jnp.int32)
        gsel.append(rank < TOPK_GROUPS)
    v = jnp.concatenate(
        [jnp.where(gsel[g], sb[g * GROUP_SIZE:(g + 1) * GROUP_SIZE, :], neg) for g in range(N_GROUPS)], axis=0)
    eidx = lax.broadcasted_iota(jnp.int32, (N_EXPERTS, tm), 0)
    sel = jnp.zeros((N_EXPERTS, tm), F32)
    for _ in range(TOP_K):
        m = jnp.max(v, axis=0, keepdims=True)
        first = jnp.min(jnp.where(v == m, eidx, N_EXPERTS), axis=0, keepdims=True)
        hit = eidx == first
        sel = jnp.where(hit, 1.0, sel)
        v = jnp.where(hit, neg, v)
    wsel = s * sel
    return wsel / jnp.sum(wsel, axis=0, keepdims=True) * ROUTE_SCALE, sel


def _per_row(chunk_vals):
    n, t = chunk_vals.shape
    return jnp.broadcast_to(chunk_vals[:, None, :], (n, SEG, t)).reshape(n * SEG, t)


def _row_lookup(seg_b, start_b, rank, tables):
    n_chunks = BLOCK_ROWS // SEG
    ei = lax.broadcasted_iota(jnp.int32, (N_EXPERTS, LANES), 0)
    ej = lax.broadcasted_iota(jnp.int32, (N_EXPERTS, LANES), 1)
    seg16 = (seg_b * (1.0 / SEG)).astype(BF16)
    start_row = _dot_tn(seg16, jnp.where(ei < ej, 1.0, 0.0).astype(BF16))[0:1, :]
    seg_row = _dot_tn(seg16, jnp.where(ei == ej, 1.0, 0.0).astype(BF16))[0:1, :]
    c = lax.broadcasted_iota(jnp.int32, (n_chunks, LANES), 0).astype(F32)
    owner = jnp.where(c >= start_row, jnp.where(c < start_row + seg_row, 1.0, 0.0), 0.0).astype(BF16)

    def lookup(tab):
        padded = jnp.concatenate([tab.astype(BF16), jnp.zeros((LANES - N_EXPERTS, tab.shape[1]), BF16)], axis=0)
        return _dot(owner, padded)

    first = SEG * (c - lookup(start_b * (1.0 / SEG)))[:, 0:1]
    rel = lookup(rank) - first
    sub = lax.broadcasted_iota(jnp.int32, (BLOCK_ROWS, rank.shape[1]), 0) & (SEG - 1)
    hits = _per_row(rel) == sub.astype(F32)
    return hits, [_per_row(lookup(tab)) for tab in tables]


def _dispatch_kernel(n_prompt_blocks, n_blocks, hp_ref, hs_ref, wrt_ref, rbias_ref, x_ref, rank_ref, comb_ref,
                     seg_ref, start_ref):
    i = pl.program_id(0)

    @pl.when(i < n_blocks)
    def _():
        _dispatch_block(i < n_prompt_blocks, hp_ref, hs_ref, wrt_ref, rbias_ref, x_ref, rank_ref, comb_ref, seg_ref,
                        start_ref)

    @pl.when(i >= n_blocks)
    def _():
        for ref in (x_ref, rank_ref, comb_ref, seg_ref, start_ref):
            ref[...] = jnp.zeros_like(ref)


def _dispatch_block(is_prompt, hp_ref, hs_ref, wrt_ref, rbias_ref, x_ref, rank_ref, comb_ref, seg_ref, start_ref):
    tm = BLOCK
    hb = jnp.where(is_prompt, hp_ref[...], hs_ref[...]).astype(BF16)
    comb, sel = _route(hb, wrt_ref, rbias_ref)
    ti = lax.broadcasted_iota(jnp.int32, (tm, tm), 0)
    tj = lax.broadcasted_iota(jnp.int32, (tm, tm), 1)
    before = jnp.where(ti < tj, 1.0, 0.0).astype(BF16)
    rank = _dot(sel.astype(BF16), before)
    rank = jnp.where(sel > 0.0, rank, -1.0)
    cnt = jnp.sum(sel, axis=1, keepdims=True)
    seg = jnp.floor((cnt + (SEG - 1.0)) * (1.0 / SEG)) * SEG
    ei = lax.broadcasted_iota(jnp.int32, (N_EXPERTS, N_EXPERTS), 0)
    ej = lax.broadcasted_iota(jnp.int32, (N_EXPERTS, N_EXPERTS), 1)
    below = jnp.where(ej < ei, 1.0, 0.0).astype(BF16)
    seg_b = jnp.broadcast_to(seg, (N_EXPERTS, LANES))
    start_b = _dot(below, seg_b.astype(BF16))
    hits, _ = _row_lookup(seg_b, start_b, rank, [])
    gather = jnp.where(hits, 1.0, 0.0).astype(BF16)
    for c in range(BLOCK_ROWS // GATHER_ROWS):
        rs = slice(c * GATHER_ROWS, (c + 1) * GATHER_ROWS)
        x_ref[rs, :] = _dot(gather[rs, :], hb).astype(BF16)
    rank_ref[0] = rank
    comb_ref[0] = comb
    seg_ref[0] = seg_b
    start_ref[0] = start_b


def _ffn_kernel(dump_base, nt_max, tile_expert_ref, tile_in_expert_ref, n_chunks_ref, n_tiles_ref,
                *refs):
    tabs = refs[:IN_BUFS]
    x_hbm, wg_ref, wu_ref, wd_ref, o_hbm, ibuf, obuf, wgb, wub, wdb, in_sem, out_sem = refs[IN_BUFS:]
    tab0_ref = tabs[0]
    i = pl.program_id(0)
    n_tiles = n_tiles_ref[0]

    def real_chunks(tile):
        t = jnp.clip(tile, 0, nt_max - 1)
        n = n_chunks_ref[tile_expert_ref[t]] - tile_in_expert_ref[t] * TILE_CHUNKS
        return jnp.where(jnp.logical_and(tile >= 0, tile < n_tiles), jnp.minimum(n, TILE_CHUNKS), 0)

    def for_groups(n_real, fn):
        for g in range(TILE_CHUNKS // GROUP_CHUNKS):
            @pl.when(g * GROUP_CHUNKS < n_real)
            def _():
                fn(range(g * GROUP_CHUNKS, (g + 1) * GROUP_CHUNKS))

    def start_in(ahead, slot):
        n_real = real_chunks(i + ahead)
        first = jnp.where(n_real > 0, tile_in_expert_ref[jnp.clip(i + ahead, 0, nt_max - 1)] * TILE_CHUNKS, 0)

        def issue(chunks):
            srcs = [tabs[ahead][0, 0, first + j] for j in chunks]
            for j, src in zip(chunks, srcs):
                pltpu.make_async_copy(x_hbm.at[src], ibuf.at[slot, j], in_sem.at[slot]).start()

        for_groups(n_real, issue)

    def start_out(slot):
        n_real = real_chunks(i)
        first = tile_in_expert_ref[i] * TILE_CHUNKS

        def issue(chunks):
            dsts = [jnp.where(j < n_real, tab0_ref[0, 0, first + j], dump_base + slot * TILE_CHUNKS + j)
                    for j in chunks]
            for j, dst in zip(chunks, dsts):
                pltpu.make_async_copy(obuf.at[slot, j], o_hbm.at[dst], out_sem.at[slot]).start()

        for_groups(n_real, issue)

    def in_wait(slot, n_real):
        def wait(chunks):
            for j in chunks:
                pltpu.make_async_copy(x_hbm.at[0], ibuf.at[slot, j], in_sem.at[slot]).wait()

        for_groups(n_real, wait)

    def out_wait(slot, n_real):
        def wait(chunks):
            for j in chunks:
                pltpu.make_async_copy(obuf.at[slot, j], o_hbm.at[0], out_sem.at[slot]).wait()

        for_groups(n_real, wait)

    @pl.when(i < n_tiles)
    def _():
        islot = lax.rem(i, IN_BUFS)
        nslot = jnp.where(islot == 0, IN_BUFS - 1, islot - 1)
        oslot = lax.rem(i, 2)

        @pl.when(i == 0)
        def _():
            ibuf[...] = jnp.zeros_like(ibuf)
            for ahead in range(IN_BUFS - 1):
                start_in(ahead, ahead)

        @pl.when(i >= 2)
        def _():
            out_wait(oslot, real_chunks(i - 2))

        in_wait(islot, real_chunks(i))

        @pl.when(tile_in_expert_ref[i] == 0)
        def _():
            wgb[...] = wg_ref[0].astype(BF16)
            wub[...] = wu_ref[0].astype(BF16)
            wdb[...] = wd_ref[0].astype(BF16)

        start_in(IN_BUFS - 1, nslot)
        x = ibuf[islot].reshape(TILE_ROWS, D_MODEL)
        hid = (_silu(_dot(x, wgb[...])) * _dot(x, wub[...])).astype(BF16)
        obuf[oslot] = _dot(hid, wdb[...]).astype(BF16).reshape(TILE_CHUNKS, SEG, D_MODEL)
        start_out(oslot)

        @pl.when(i == n_tiles - 1)
        def _():
            out_wait(oslot, real_chunks(i))
            out_wait(1 - oslot, real_chunks(i - 1))


def _combine_kernel(h_ref, o_ref, rank_ref, comb_ref, seg_ref, start_ref, wsg_ref, wsu_ref, wsd_ref, ln2g_ref,
                    ln2b_ref, y_ref):
    h = h_ref[...]
    hb = h.astype(BF16)
    hits, (comb_rows,) = _row_lookup(seg_ref[0], start_ref[0], rank_ref[0], [comb_ref[0]])
    scatter = jnp.where(hits, comb_rows, 0.0).astype(BF16)
    routed = _dot_tn(scatter, o_ref[...])
    shared = _dot((_silu(_dot(hb, wsg_ref[...])) * _dot(hb, wsu_ref[...])).astype(BF16), wsd_ref[...])
    y_ref[...] = _layernorm(ALPHA * h + (routed + shared), ln2g_ref[...], ln2b_ref[...])


def _full(shape):
    return pl.BlockSpec(shape, lambda *_: (0,) * len(shape))


def _const(shape):
    return pl.BlockSpec(shape, lambda *_: (0,) * len(shape), pipeline_mode=pl.Buffered(1))


def _mixer_prompt(x, wts):
    bsz, seq, _ = x.shape
    n_t = seq // PROMPT_TILE
    weights = [wts[k] for k in ("win", "waup", "ba", "glag", "lng", "lnb", "ws", "bsf_prompt", "gmg", "wout",
                                "ln1g", "ln1b")]
    return pl.pallas_call(
        _mixer_prompt_kernel,
        grid=(bsz // PROMPT_SEQS, n_t),
        in_specs=[pl.BlockSpec((PROMPT_SEQS, PROMPT_TILE, D_MODEL), lambda b, t: (b, t, 0))]
        + [_const(w.shape) for w in weights],
        out_specs=[pl.BlockSpec((PROMPT_SEQS, PROMPT_TILE, D_MODEL), lambda b, t: (b, t, 0)),
                   pl.BlockSpec((PROMPT_SEQS, GLA_HEADS, GLA_DK, GLA_DV), lambda b, t: (b, 0, 0, 0))],
        out_shape=[jax.ShapeDtypeStruct((bsz, seq, D_MODEL), F32),
                   jax.ShapeDtypeStruct((bsz, GLA_HEADS, GLA_DK, GLA_DV), F32)],
        scratch_shapes=[pltpu.VMEM((PROMPT_SEQS * PROMPT_TILE, N_PROJ), F32),
                        pltpu.VMEM((PROMPT_SEQS * PROMPT_TILE, D_MODEL), BF16)]
        + [pltpu.VMEM((GLA_HEADS, GLA_DV, LANES), F32)] * PROMPT_SEQS,
        compiler_params=pltpu.CompilerParams(dimension_semantics=("arbitrary", "arbitrary"),
                                             vmem_limit_bytes=VMEM_LIMIT),
        name="mixer_prompt",
    )(x, *weights)


def _mixer_sample(x, s0, wts):
    bsz, seq_len, _ = x.shape
    n = SAMPLE_SEQS * seq_len
    x2 = x.reshape(bsz * seq_len, D_MODEL)
    weights = [wts[k] for k in ("win", "wkt", "walrt", "waup", "waupt", "ba", "bac", "glag", "lng", "lnb",
                                "ws_sample", "bsf_sample", "gmg", "wout", "ln1g", "ln1b")]
    state_spec = pl.BlockSpec((SAMPLE_SEQS, GLA_HEADS, GLA_DK, GLA_DV), lambda i: (i, 0, 0, 0))
    h, s_new, vg = pl.pallas_call(
        functools.partial(_mixer_sample_kernel, seq_len),
        grid=(bsz // SAMPLE_SEQS,),
        in_specs=[pl.BlockSpec((n, D_MODEL), lambda i: (i, 0)), state_spec] + [_full(w.shape) for w in weights],
        out_specs=[pl.BlockSpec((n, D_MODEL), lambda i: (i, 0)), state_spec,
                   pl.BlockSpec((n, GMLP_WIDTH), lambda i: (i, 0))],
        out_shape=[jax.ShapeDtypeStruct((bsz * seq_len, D_MODEL), F32),
                   jax.ShapeDtypeStruct(s0.shape, F32),
                   jax.ShapeDtypeStruct((bsz * seq_len, GMLP_WIDTH), F32)],
        scratch_shapes=[pltpu.VMEM((n, D_MODEL), BF16)],
        compiler_params=pltpu.CompilerParams(dimension_semantics=("arbitrary",), vmem_limit_bytes=VMEM_LIMIT),
        name="mixer_sample",
    )(x2, s0, *weights)
    return h, s_new, vg.reshape(bsz, seq_len, GMLP_WIDTH)


def _dispatch(h_p, h_s, wts):
    npb, nsb = h_p.shape[0] // BLOCK, h_s.shape[0] // BLOCK
    nb_real = npb + nsb
    nb = nb_real + DUMP_BLOCKS
    tok_spec = pl.BlockSpec((1, N_EXPERTS, BLOCK), lambda i: (i, 0, 0))
    run_spec = pl.BlockSpec((1, N_EXPERTS, LANES), lambda i: (i, 0, 0))
    return pl.pallas_call(
        functools.partial(_dispatch_kernel, npb, nb_real),
        grid=(nb,),
        in_specs=[pl.BlockSpec((BLOCK, D_MODEL), lambda i: (jnp.minimum(i, npb - 1), 0)),
                  pl.BlockSpec((BLOCK, D_MODEL), lambda i: (jnp.clip(i - npb, 0, nsb - 1), 0)),
                  _full(wts["wrt"].shape), _full(wts["rbias"].shape)],
        out_specs=[pl.BlockSpec((BLOCK_ROWS, D_MODEL), lambda i: (i, 0)), tok_spec, tok_spec, run_spec, run_spec],
        out_shape=[jax.ShapeDtypeStruct((nb * BLOCK_ROWS, D_MODEL), BF16),
                   jax.ShapeDtypeStruct((nb, N_EXPERTS, BLOCK), F32),
                   jax.ShapeDtypeStruct((nb, N_EXPERTS, BLOCK), F32),
                   jax.ShapeDtypeStruct((nb, N_EXPERTS, LANES), F32),
                   jax.ShapeDtypeStruct((nb, N_EXPERTS, LANES), F32)],
        compiler_params=pltpu.CompilerParams(dimension_semantics=("arbitrary",), vmem_limit_bytes=VMEM_LIMIT),
        name="moe_dispatch",
    )(h_p, h_s, wts["wrt"], wts["rbias"])


def _max_tiles(nb):
    return (nb * (BLOCK_ROWS_USED // SEG) + N_EXPERTS * (TILE_CHUNKS - 1)) // TILE_CHUNKS + 1


def _max_expert_chunks(nb):
    most = nb * (BLOCK // SEG) + TILE_CHUNKS - 1
    return -(-most // LANES) * LANES


def _plan(seg_lanes, nb):
    nt_max = _max_tiles(nb)
    max_chunks = _max_expert_chunks(nb)
    seg = seg_lanes[:nb, :, 0].astype(jnp.int32)
    start = jnp.cumsum(seg, axis=1) - seg
    nch = seg // SEG
    ends = jnp.cumsum(nch, axis=0)
    n_chunks = ends[-1]
    f = jnp.arange(nb, dtype=jnp.int32)[:, None] * (BLOCK_ROWS // SEG) + start // SEG - (ends - nch)
    q = jnp.arange(max_chunks, dtype=jnp.int32)
    passed = (ends[None, :-1, :] <= q[:, None, None]).astype(jnp.int32)
    chunk = q[:, None] + f[0][None, :] + jnp.sum(passed * (f[1:] - f[:-1])[None], axis=1)
    table = jnp.where(q[:, None] < n_chunks[None, :], chunk, ZERO_CHUNK).T
    n_tiles_e = -(-n_chunks // TILE_CHUNKS)
    tile_ends = jnp.cumsum(n_tiles_e)
    t = jnp.arange(nt_max, dtype=jnp.int32)
    done = (tile_ends[None, :] <= t[:, None]).astype(jnp.int32)
    tile_expert = jnp.minimum(jnp.sum(done, axis=1), N_EXPERTS - 1)
    tile_in_expert = t - jnp.sum(done * n_tiles_e[None, :], axis=1)
    return (table.astype(jnp.int32).reshape(N_EXPERTS, 1, max_chunks), tile_expert.astype(jnp.int32),
            tile_in_expert.astype(jnp.int32), n_chunks.astype(jnp.int32), tile_ends[-1:].astype(jnp.int32))


def _ffn(x_rows, nb, table, tile_expert, tile_in_expert, n_chunks, n_tiles, w_gate, w_up, w_down):
    nt_max = tile_expert.shape[0]
    max_chunks = table.shape[-1]
    tab_spec = lambda ahead: pl.BlockSpec(
        (1, 1, max_chunks), lambda i, te, *_: (te[jnp.clip(i + ahead, 0, nt_max - 1)], 0, 0),
        memory_space=pltpu.SMEM)
    w_spec = lambda shape: pl.BlockSpec((1,) + shape, lambda i, te, *_: (te[i], 0, 0))
    assert x_rows.shape[0] - nb * BLOCK_ROWS >= 2 * TILE_ROWS
    x_chunks = x_rows.reshape(x_rows.shape[0] // SEG, SEG, D_MODEL)
    return pl.pallas_call(
        functools.partial(_ffn_kernel, nb * BLOCK_ROWS // SEG, nt_max),
        grid_spec=pltpu.PrefetchScalarGridSpec(
            num_scalar_prefetch=4,
            grid=(nt_max,),
            in_specs=[tab_spec(ahead) for ahead in range(IN_BUFS)] + [pl.BlockSpec(memory_space=pl.ANY),
                      w_spec((D_MODEL, D_EXPERT)), w_spec((D_MODEL, D_EXPERT)), w_spec((D_EXPERT, D_MODEL))],
            out_specs=pl.BlockSpec(memory_space=pl.ANY),
            scratch_shapes=[pltpu.VMEM((IN_BUFS, TILE_CHUNKS, SEG, D_MODEL), BF16),
                            pltpu.VMEM((2, TILE_CHUNKS, SEG, D_MODEL), BF16),
                            pltpu.VMEM((D_MODEL, D_EXPERT), BF16),
                            pltpu.VMEM((D_MODEL, D_EXPERT), BF16),
                            pltpu.VMEM((D_EXPERT, D_MODEL), BF16),
                            pltpu.SemaphoreType.DMA((IN_BUFS,)),
                            pltpu.SemaphoreType.DMA((2,))]),
        out_shape=jax.ShapeDtypeStruct(x_chunks.shape, BF16),
        input_output_aliases={4 + IN_BUFS: 0},
        compiler_params=pltpu.CompilerParams(dimension_semantics=("arbitrary",), vmem_limit_bytes=VMEM_LIMIT),
        name="moe_ffn",
    )(tile_expert, tile_in_expert, n_chunks, n_tiles, *([table] * IN_BUFS), x_chunks, w_gate, w_up, w_down
      ).reshape(x_rows.shape)


def _combine(h, o_rows, routing, block_off, wts):
    nblk = h.shape[0] // BLOCK
    weights = [wts[k] for k in ("wsg", "wsu", "wsd", "ln2g", "ln2b")]
    tok_spec = pl.BlockSpec((1, N_EXPERTS, BLOCK), lambda i: (i + block_off, 0, 0))
    run_spec = pl.BlockSpec((1, N_EXPERTS, LANES), lambda i: (i + block_off, 0, 0))
    return pl.pallas_call(
        _combine_kernel,
        grid=(nblk,),
        in_specs=[pl.BlockSpec((BLOCK, D_MODEL), lambda i: (i, 0)),
                  pl.BlockSpec((BLOCK_ROWS, D_MODEL), lambda i: (i + block_off, 0)),
                  tok_spec, tok_spec, run_spec, run_spec] + [_full(w.shape) for w in weights],
        out_specs=pl.BlockSpec((BLOCK, D_MODEL), lambda i: (i, 0)),
        out_shape=jax.ShapeDtypeStruct(h.shape, F32),
        compiler_params=pltpu.CompilerParams(dimension_semantics=("arbitrary",), vmem_limit_bytes=VMEM_LIMIT),
        name="moe_combine",
    )(h, o_rows, *routing, *weights)


def _moe(h_p, h_s, w_gate, w_up, w_down, wts):
    nb = (h_p.shape[0] + h_s.shape[0]) // BLOCK
    x_rows, rank, comb, seg_lanes, start_lanes = _dispatch(h_p, h_s, wts)
    o_rows = _ffn(x_rows, nb, *_plan(seg_lanes, nb), w_gate, w_up, w_down)
    routing = (rank, comb, seg_lanes, start_lanes)
    y_p = _combine(h_p, o_rows, routing, 0, wts)
    y_s = _combine(h_s, o_rows, routing, h_p.shape[0] // BLOCK, wts)
    return y_p, y_s


def _prep_weights(seq_len, w_in, w_a_up, b_a, gla_norm_g, gmlp_ln_g, gmlp_ln_b, w_s, b_s, gmlp_norm_g, w_out,
                  ln1_g, ln1_b, w_router, router_bias, ws_gate, ws_up, ws_down, ln2_g, ln2_b):
    o1 = QK_WIDTH
    o2 = o1 + QK_WIDTH
    o3 = o2 + GLA_WIDTH
    o4 = o3 + GLA_WIDTH
    o5 = o4 + GLA_RANK
    o6 = o5 + GMLP_WIDTH
    wq, wk, wva, wg_, walr, wu_, wvb = jnp.split(w_in, [o1, o2, o3, o4, o5, o6], axis=-1)
    walr_p = jnp.pad(walr, ((0, 0), (0, LANES - GLA_RANK)))
    waup_p = jnp.pad(w_a_up, ((0, LANES - GLA_RANK), (0, 0)))
    row = lambda a: a.reshape(1, -1)
    reps = (SAMPLE_SEQS * seq_len) // seq_len
    ws_small = w_s[:, :seq_len, :seq_len]
    return {
        "win": jnp.concatenate([wq, wk, wva, wg_, wu_, wvb, walr_p], axis=-1).astype(BF16),
        "wkt": wk.T.astype(BF16),
        "walrt": walr_p.T.astype(BF16),
        "waup": waup_p.astype(BF16),
        "waupt": waup_p.T.astype(BF16),
        "ba": row(b_a), "bac": b_a.reshape(-1, 1),
        "glag": gla_norm_g, "lng": row(gmlp_ln_g), "lnb": row(gmlp_ln_b),
        "ws": w_s,
        "ws_sample": jnp.pad(ws_small, ((0, 0), (0, LANES - seq_len), (0, LANES - seq_len))),
        "bsf_prompt": jnp.repeat(b_s[:, :GMLP_CHUNK].T, GMLP_DH, axis=1),
        "bsf_sample": jnp.tile(jnp.repeat(b_s[:, :seq_len].T, GMLP_DH, axis=1), (reps, 1)),
        "gmg": gmlp_norm_g,
        "wout": w_out.astype(BF16),
        "ln1g": row(ln1_g), "ln1b": row(ln1_b),
        "wrt": w_router.T.astype(BF16), "rbias": router_bias.reshape(-1, 1),
        "wsg": ws_gate.astype(BF16), "wsu": ws_up.astype(BF16), "wsd": ws_down.astype(BF16),
        "ln2g": row(ln2_g), "ln2b": row(ln2_b),
    }


def kernel(x_prompt, x_sample, state_gla, w_in, w_a_up, b_a, gla_norm_g, gmlp_ln_g, gmlp_ln_b, w_s, b_s,
           gmlp_norm_g, w_out, ln1_g, ln1_b, w_router, router_bias, w_gate, w_up, w_down, ws_gate, ws_up,
           ws_down, ln2_g, ln2_b):
    assert x_prompt.shape[1] % PROMPT_TILE == 0 and x_sample.shape[0] % SAMPLE_SEQS == 0
    assert x_prompt.shape[0] % PROMPT_SEQS == 0
    assert x_sample.shape[1] <= GMLP_CHUNK and w_in.shape[0] == DEPTH
    assert (x_sample.shape[0] * x_sample.shape[1]) % BLOCK == 0 and PROMPT_TILE % BLOCK == 0
    bsz, seq, _ = x_prompt.shape
    dbsz, dseq, _ = x_sample.shape
    hp, hs = x_prompt, x_sample
    gla_p, gla_s, v_s = [], [], []
    for l in range(DEPTH):
        wts = _prep_weights(dseq, w_in[l], w_a_up[l], b_a[l], gla_norm_g[l], gmlp_ln_g[l], gmlp_ln_b[l], w_s[l],
                            b_s[l], gmlp_norm_g[l], w_out[l], ln1_g[l], ln1_b[l], w_router[l], router_bias[l],
                            ws_gate[l], ws_up[l], ws_down[l], ln2_g[l], ln2_b[l])
        h_p, sp = _mixer_prompt(hp, wts)
        h_s, ss, vrows = _mixer_sample(hs, state_gla[l], wts)
        y_p, y_s = _moe(h_p.reshape(bsz * seq, D_MODEL), h_s, w_gate[l], w_up[l], w_down[l], wts)
        hp = y_p.reshape(bsz, seq, D_MODEL)
        hs = y_s.reshape(dbsz, dseq, D_MODEL)
        gla_p.append(sp)
        gla_s.append(ss)
        v_s.append(vrows)
    return (hp, hs, jnp.stack(gla_p), jnp.stack(gla_s), jnp.stack(v_s))
```

```python
import functools
import math

import jax
import jax.numpy as jnp
from jax import lax
from jax.experimental import pallas as pl
from jax.experimental.pallas import tpu as pltpu

F32 = jnp.float32
BF16 = jnp.bfloat16

D_MODEL = 1024
DEPTH = 1
GLA_WIDTH = 512
GLA_HEADS = 4
GLA_DK = 64
GLA_DV = 128
GLA_RANK = 16
GLA_TAU = 16.0
GMLP_WIDTH = 512
GMLP_HEADS = 4
GMLP_DH = 128
GMLP_CHUNK = 128
QK_WIDTH = GLA_HEADS * GLA_DK
N_EXPERTS = 64
TOP_K = 8
N_GROUPS = 8
GROUP_SIZE = N_EXPERTS // N_GROUPS
TOPK_GROUPS = 4
D_EXPERT = 256
D_SHARED = 256
ROUTE_SCALE = 2.5
ALPHA = (2.0 * DEPTH) ** 0.25

LANES = 128

C_Q = 0
C_K = C_Q + QK_WIDTH
C_VA = C_K + QK_WIDTH
C_G = C_VA + GLA_WIDTH
C_U = C_G + GLA_WIDTH
C_VB = C_U + GMLP_WIDTH
C_ALR = C_VB + GMLP_WIDTH
N_PROJ = C_ALR + LANES

CHUNK = 128
GLA_BLOCK = 256
PROMPT_TILE = 512
PROMPT_SEQS = 2
SAMPLE_SEQS = 32
VMEM_LIMIT = 56 * 1024 * 1024

BLOCK = 256
SEG = 16
TILE_CHUNKS = 64
TILE_ROWS = TILE_CHUNKS * SEG
GATHER_ROWS = 512
BLOCK_ROWS_USED = BLOCK * TOP_K + N_EXPERTS * (SEG - 1)
BLOCK_ROWS = -(-(BLOCK_ROWS_USED + SEG) // GATHER_ROWS) * GATHER_ROWS
DUMP_BLOCKS = -(-(2 * TILE_ROWS) // BLOCK_ROWS)
ZERO_CHUNK = BLOCK_ROWS_USED // SEG
IN_BUFS = 4


def _dot(a, b):
    return jnp.dot(a, b, preferred_element_type=F32)


def _dot_nt(a, b):
    return lax.dot_general(a, b, (((1,), (1,)), ((), ())), preferred_element_type=F32)


def _dot_tn(a, b):
    return lax.dot_general(a, b, (((0,), (0,)), ((), ())), preferred_element_type=F32)


def _shr(x, d):
    assert d & (d - 1) == 0
    return lax.shift_right_logical(x, d.bit_length() - 1)


def _split_dot(m01, x):
    hi = x.astype(BF16)
    lo = (x - hi.astype(F32)).astype(BF16)
    return _dot(m01, hi) + _dot(m01, lo)


def _split_dot_r(x, m01):
    hi = x.astype(BF16)
    lo = (x - hi.astype(F32)).astype(BF16)
    return _dot(hi, m01) + _dot(lo, m01)


def _sigmoid(x):
    return 1.0 / (1.0 + jnp.exp(-x))


def _silu(x):
    return x * _sigmoid(x)


def _gelu(x):
    c = math.sqrt(2.0 / math.pi)
    return x * (0.5 * (1.0 + jnp.tanh(c * (x + 0.044715 * (x * x * x)))))


def _log_sigmoid(x):
    return -(jnp.maximum(-x, 0.0) + jnp.log(1.0 + jnp.exp(-jnp.abs(x))))


def _layernorm(x, g, b, eps=1e-5):
    mu = jnp.mean(x, axis=-1, keepdims=True)
    xc = x - mu
    var = jnp.mean(xc * xc, axis=-1, keepdims=True)
    return xc * lax.rsqrt(var + eps) * g + b


def _rmsnorm(x, g, eps=1e-6):
    return x * lax.rsqrt(jnp.mean(x * x, axis=-1, keepdims=True) + eps) * g


def _gmlp_heads(u, vg, w_mix, bias, gmg_ref, merged_ref, rows):
    vgb = vg.astype(BF16)
    for h in range(GMLP_HEADS):
        cs = slice(h * GMLP_DH, (h + 1) * GMLP_DH)
        sgu = _dot(w_mix(h), vgb[:, cs]) + bias[:, cs]
        y = _rmsnorm(u[:, cs] * sgu, gmg_ref[h:h + 1, :])
        merged_ref[rows, GLA_WIDTH + h * GMLP_DH:GLA_WIDTH + (h + 1) * GMLP_DH] = y.astype(BF16)


def _out_proj_ln(x, merged_ref, wout_ref, ln1g_ref, ln1b_ref):
    m = _dot(merged_ref[...], wout_ref[...])
    return _layernorm(ALPHA * x + m, ln1g_ref[...], ln1b_ref[...])


def _mixer_prompt_kernel(x_ref, win_ref, waup_ref, ba_ref, glag_ref, lng_ref, lnb_ref, ws_ref, bsf_ref,
                         gmg_ref, wout_ref, ln1g_ref, ln1b_ref,
                         h_ref, state_ref,
                         z_ref, merged_ref, *st_refs):
    t = pl.program_id(1)

    @pl.when(t == 0)
    def _():
        for st_ref in st_refs:
            st_ref[...] = jnp.zeros_like(st_ref)

    x = x_ref[...].reshape(PROMPT_SEQS * PROMPT_TILE, D_MODEL)
    z_ref[...] = _dot(x.astype(BF16), win_ref[...])

    row_i = lax.broadcasted_iota(jnp.int32, (GLA_BLOCK, GLA_BLOCK), 0)
    col_i = lax.broadcasted_iota(jnp.int32, (GLA_BLOCK, GLA_BLOCK), 1)
    causal = row_i >= col_i
    tri = jnp.where(causal, 1.0, 0.0).astype(BF16)
    causal_mix = causal[:CHUNK, :CHUNK]
    lane = lax.broadcasted_iota(jnp.int32, (1, LANES), 1)
    head_lanes = [lane < GLA_DK, lane >= GLA_DK]
    mid = GLA_BLOCK // 2 - 1

    for c in range(PROMPT_SEQS * PROMPT_TILE // GLA_BLOCK):
        rows = slice(c * GLA_BLOCK, (c + 1) * GLA_BLOCK)
        st_ref = st_refs[c // (PROMPT_TILE // GLA_BLOCK)]
        a_pre = _dot(z_ref[rows, C_ALR:C_ALR + LANES].astype(BF16), waup_ref[...]) + ba_ref[...]
        log_a = _log_sigmoid(a_pre) * (1.0 / GLA_TAU)
        b = _split_dot(tri, log_a)
        b_mid = b[mid:mid + 1, :]
        b_last = b[GLA_BLOCK - 1:GLA_BLOCK, :]
        q = z_ref[rows, C_Q:C_Q + QK_WIDTH] * (GLA_DK ** -0.5)
        k = z_ref[rows, C_K:C_K + QK_WIDTH]
        q_in = (q * jnp.exp(b - b_mid)).astype(BF16)
        k_in = (k * jnp.exp(b_mid - b)).astype(BF16)
        q_st = (q * jnp.exp(b)).astype(BF16)
        k_st = (k * jnp.exp(b_last - b)).astype(BF16)
        d_last = jnp.exp(b_last)
        va = z_ref[rows, C_VA:C_VA + GLA_WIDTH].astype(BF16)
        for h in range(GLA_HEADS):
            ps = slice((h // 2) * LANES, (h // 2 + 1) * LANES)
            vs = slice(h * GLA_DV, (h + 1) * GLA_DV)
            hm = head_lanes[h % 2]
            zero = jnp.zeros((), BF16)
            a = _dot_nt(jnp.where(hm, q_in[:, ps], zero), k_in[:, ps])
            a = jnp.where(causal, a, 0.0).astype(BF16)
            st = st_ref[h]
            o = _dot(a, va[:, vs]) + _dot_nt(jnp.where(hm, q_st[:, ps], zero), st.astype(BF16))
            upd = _dot_tn(va[:, vs], jnp.where(hm, k_st[:, ps], zero))
            st_ref[h] = st * d_last[:, ps] + upd
            gate = z_ref[rows, C_G + h * GLA_DV:C_G + (h + 1) * GLA_DV]
            o = _rmsnorm(o, glag_ref[h:h + 1, :]) * _silu(gate)
            merged_ref[rows, vs] = o.astype(BF16)

    for c in range(PROMPT_SEQS * PROMPT_TILE // CHUNK):
        rows = slice(c * CHUNK, (c + 1) * CHUNK)
        u = _gelu(z_ref[rows, C_U:C_U + GMLP_WIDTH])
        vg = _layernorm(_gelu(z_ref[rows, C_VB:C_VB + GMLP_WIDTH]), lng_ref[...], lnb_ref[...])
        _gmlp_heads(u, vg, lambda h: jnp.where(causal_mix, ws_ref[h], 0.0).astype(BF16), bsf_ref[...],
                    gmg_ref, merged_ref, rows)

    h_ref[...] = _out_proj_ln(x, merged_ref, wout_ref, ln1g_ref, ln1b_ref).reshape(PROMPT_SEQS, PROMPT_TILE, D_MODEL)

    @pl.when(t == pl.num_programs(1) - 1)
    def _():
        for s, st_ref in enumerate(st_refs):
            for h in range(GLA_HEADS):
                lo = (h % 2) * GLA_DK
                state_ref[s, h] = st_ref[h].T[lo:lo + GLA_DK, :]


def _mixer_sample_kernel(seq_len, x_ref, s0_ref, win_ref, wkt_ref, walrt_ref, waup_ref, waupt_ref, ba_ref, bac_ref,
                         glag_ref, lng_ref, lnb_ref, wst_ref, bsf_ref, gmg_ref, wout_ref, ln1g_ref, ln1b_ref,
                         h_ref, snew_ref, vg_ref,
                         merged_ref):
    n = SAMPLE_SEQS * seq_len
    x = x_ref[...]
    xb = x.astype(BF16)
    z = _dot(xb, win_ref[...])

    ti = lax.broadcasted_iota(jnp.int32, (n, n), 0)
    tj = lax.broadcasted_iota(jnp.int32, (n, n), 1)
    same = _shr(ti, seq_len) == _shr(tj, seq_len)
    causal = jnp.logical_and(same, ti >= tj)
    tri = jnp.where(causal, 1.0, 0.0).astype(BF16)
    tri_t = jnp.where(jnp.logical_and(same, ti <= tj), 1.0, 0.0).astype(BF16)
    same01 = jnp.where(same, 1.0, 0.0).astype(BF16)

    a_pre = _dot(z[:, C_ALR:C_ALR + LANES].astype(BF16), waup_ref[...]) + ba_ref[...]
    log_a = _log_sigmoid(a_pre) * (1.0 / GLA_TAU)
    b = _split_dot(tri, log_a)
    q = z[:, C_Q:C_Q + QK_WIDTH] * (GLA_DK ** -0.5)
    k = z[:, C_K:C_K + QK_WIDTH]
    q_in = (q * jnp.exp(b)).astype(BF16)
    k_in = (k * jnp.exp(-b)).astype(BF16)
    va = z[:, C_VA:C_VA + GLA_WIDTH].astype(BF16)

    k_t = _dot_nt(wkt_ref[...], xb)
    alr_t = _dot_nt(walrt_ref[...], xb)
    a_pre_t = _dot(waupt_ref[...], alr_t.astype(BF16)) + bac_ref[...]
    log_a_t = _log_sigmoid(a_pre_t) * (1.0 / GLA_TAU)
    b_t = _split_dot_r(log_a_t, tri_t)
    tot_t = _split_dot_r(log_a_t, same01)
    k_st_t = k_t * jnp.exp(tot_t - b_t)
    d_t = jnp.exp(tot_t)

    lane = lax.broadcasted_iota(jnp.int32, (1, LANES), 1)
    head_lanes = [lane < GLA_DK, lane >= GLA_DK]
    nrow = SAMPLE_SEQS * GLA_DK
    r_seq = _shr(lax.broadcasted_iota(jnp.int32, (nrow, n), 0), GLA_DK)
    c_seq = _shr(lax.broadcasted_iota(jnp.int32, (nrow, n), 1), seq_len)
    c_first = (lax.broadcasted_iota(jnp.int32, (nrow, n), 1) & (seq_len - 1)) == 0
    blk = r_seq == c_seq
    blk_first = jnp.logical_and(blk, c_first)
    q_seq = _shr(lax.broadcasted_iota(jnp.int32, (n, nrow), 0), seq_len)
    q_col = _shr(lax.broadcasted_iota(jnp.int32, (n, nrow), 1), GLA_DK)
    blk_q = q_seq == q_col

    for h in range(GLA_HEADS):
        ps = slice((h // 2) * LANES, (h // 2 + 1) * LANES)
        ds_ = slice(h * GLA_DK, (h + 1) * GLA_DK)
        vs = slice(h * GLA_DV, (h + 1) * GLA_DV)
        hm = head_lanes[h % 2]
        zero = jnp.zeros((), BF16)
        a = _dot_nt(jnp.where(hm, q_in[:, ps], zero), k_in[:, ps])
        a = jnp.where(causal, a, 0.0).astype(BF16)
        s0 = s0_ref[:, h].reshape(nrow, GLA_DV)
        q_h = q_in[:, ds_]
        q_bd = jnp.where(blk_q, jnp.concatenate([q_h] * SAMPLE_SEQS, axis=1), zero)
        o = _dot(a, va[:, vs]) + _dot(q_bd, s0.astype(BF16))
        k_bd = jnp.where(blk, jnp.concatenate([k_st_t[ds_, :]] * SAMPLE_SEQS, axis=0), 0.0).astype(BF16)
        upd = _dot(k_bd, va[:, vs])
        d_bd = jnp.where(blk_first, jnp.concatenate([d_t[ds_, :]] * SAMPLE_SEQS, axis=0), 0.0)
        d_col = jnp.sum(d_bd, axis=1, keepdims=True)
        snew_ref[:, h] = (s0 * d_col + upd).reshape(SAMPLE_SEQS, GLA_DK, GLA_DV)
        gate = z[:, C_G + h * GLA_DV:C_G + (h + 1) * GLA_DV]
        o = _rmsnorm(o, glag_ref[h:h + 1, :]) * _silu(gate)
        merged_ref[:, vs] = o.astype(BF16)

    u = _gelu(z[:, C_U:C_U + GMLP_WIDTH])
    vg = _layernorm(_gelu(z[:, C_VB:C_VB + GMLP_WIDTH]), lng_ref[...], lnb_ref[...])
    vg_ref[...] = vg
    pi = lax.broadcasted_iota(jnp.int32, (n, LANES), 0)
    pj = lax.broadcasted_iota(jnp.int32, (n, LANES), 1)
    pos = jnp.where((pi & (seq_len - 1)) == pj, 1.0, 0.0).astype(BF16)

    def w_mix(h):
        rows = _dot(pos, wst_ref[h].astype(BF16)).astype(BF16)
        return jnp.where(causal, _dot_nt(rows, pos), 0.0).astype(BF16)

    _gmlp_heads(u, vg, w_mix, bsf_ref[...], gmg_ref, merged_ref, slice(None))
    h_ref[...] = _out_proj_ln(x, merged_ref, wout_ref, ln1g_ref, ln1b_ref)


def _route(hb, wrt_ref, rbias_ref):
    tm = hb.shape[0]
    s = _sigmoid(_dot_nt(wrt_ref[...], hb))
    sb = s + rbias_ref[...]
    neg = jnp.float32(-jnp.inf)
    sub = lax.broadcasted_iota(jnp.int32, (GROUP_SIZE, tm), 0)
    gscore = []
    for g in range(N_GROUPS):
        blk = sb[g * GROUP_SIZE:(g + 1) * GROUP_SIZE, :]
        m1 = jnp.max(blk, axis=0, keepdims=True)
        i1 = jnp.min(jnp.where(blk == m1, sub, GROUP_SIZE), axis=0, keepdims=True)
        m2 = jnp.max(jnp.where(sub == i1, neg, blk), axis=0, keepdims=True)
        gscore.append(m1 + m2)
    gsel = []
    for g in range(N_GROUPS):
        rank = jnp.zeros((1, tm), jnp.int32)
        for o in range(N_GROUPS):
            if o == g:
                continue
            ahead = (gscore[o] >= gscore[g]) if o < g else (gscore[o] > gscore[g])
            rank = rank + ahead.astype(jnp.int32)
        gsel.append(rank < TOPK_GROUPS)
    v = jnp.concatenate(
        [jnp.where(gsel[g], sb[g * GROUP_SIZE:(g + 1) * GROUP_SIZE, :], neg) for g in range(N_GROUPS)], axis=0)
    eidx = lax.broadcasted_iota(jnp.int32, (N_EXPERTS, tm), 0)
    sel = jnp.zeros((N_EXPERTS, tm), F32)
    for _ in range(TOP_K):
        m = jnp.max(v, axis=0, keepdims=True)
        first = jnp.min(jnp.where(v == m, eidx, N_EXPERTS), axis=0, keepdims=True)
        hit = eidx == first
        sel = jnp.where(hit, 1.0, sel)
        v = jnp.where(hit, neg, v)
    wsel = s * sel
    return wsel / jnp.sum(wsel, axis=0, keepdims=True) * ROUTE_SCALE, sel


def _per_row(chunk_vals):
    n, t = chunk_vals.shape
    return jnp.broadcast_to(chunk_vals[:, None, :], (n, SEG, t)).reshape(n * SEG, t)


def _row_lookup(seg_b, start_b, rank, tables):
    n_chunks = BLOCK_ROWS // SEG
    ei = lax.broadcasted_iota(jnp.int32, (N_EXPERTS, LANES), 0)
    ej = lax.broadcasted_iota(jnp.int32, (N_EXPERTS, LANES), 1)
    seg16 = (seg_b * (1.0 / SEG)).astype(BF16)
    start_row = _dot_tn(seg16, jnp.where(ei < ej, 1.0, 0.0).astype(BF16))[0:1, :]
    seg_row = _dot_tn(seg16, jnp.where(ei == ej, 1.0, 0.0).astype(BF16))[0:1, :]
    c = lax.broadcasted_iota(jnp.int32, (n_chunks, LANES), 0).astype(F32)
    owner = jnp.where(c >= start_row, jnp.where(c < start_row + seg_row, 1.0, 0.0), 0.0).astype(BF16)

    def lookup(tab):
        padded = jnp.concatenate([tab.astype(BF16), jnp.zeros((LANES - N_EXPERTS, tab.shape[1]), BF16)], axis=0)
        return _dot(owner, padded)

    first = SEG * (c - lookup(start_b * (1.0 / SEG)))[:, 0:1]
    rel = lookup(rank) - first
    sub = lax.broadcasted_iota(jnp.int32, (BLOCK_ROWS, rank.shape[1]), 0) & (SEG - 1)
    hits = _per_row(rel) == sub.astype(F32)
    return hits, [_per_row(lookup(tab)) for tab in tables]


def _dispatch_kernel(n_prompt_blocks, n_blocks, hp_ref, hs_ref, wrt_ref, rbias_ref, x_ref, rank_ref, comb_ref,
                     seg_ref, start_ref):
    i = pl.program_id(0)

    @pl.when(i < n_blocks)
    def _():
        _dispatch_block(i < n_prompt_blocks, hp_ref, hs_ref, wrt_ref, rbias_ref, x_ref, rank_ref, comb_ref, seg_ref,
                        start_ref)

    @pl.when(i >= n_blocks)
    def _():
        for ref in (x_ref, rank_ref, comb_ref, seg_ref, start_ref):
            ref[...] = jnp.zeros_like(ref)


def _dispatch_block(is_prompt, hp_ref, hs_ref, wrt_ref, rbias_ref, x_ref, rank_ref, comb_ref, seg_ref, start_ref):
    tm = BLOCK
    hb = jnp.where(is_prompt, hp_ref[...], hs_ref[...]).astype(BF16)
    comb, sel = _route(hb, wrt_ref, rbias_ref)
    ti = lax.broadcasted_iota(jnp.int32, (tm, tm), 0)
    tj = lax.broadcasted_iota(jnp.int32, (tm, tm), 1)
    before = jnp.where(ti < tj, 1.0, 0.0).astype(BF16)
    rank = _dot(sel.astype(BF16), before)
    rank = jnp.where(sel > 0.0, rank, -1.0)
    cnt = jnp.sum(sel, axis=1, keepdims=True)
    seg = jnp.floor((cnt + (SEG - 1.0)) * (1.0 / SEG)) * SEG
    ei = lax.broadcasted_iota(jnp.int32, (N_EXPERTS, N_EXPERTS), 0)
    ej = lax.broadcasted_iota(jnp.int32, (N_EXPERTS, N_EXPERTS), 1)
    below = jnp.where(ej < ei, 1.0, 0.0).astype(BF16)
    seg_b = jnp.broadcast_to(seg, (N_EXPERTS, LANES))
    start_b = _dot(below, seg_b.astype(BF16))
    hits, _ = _row_lookup(seg_b, start_b, rank, [])
    gather = jnp.where(hits, 1.0, 0.0).astype(BF16)
    for c in range(BLOCK_ROWS // GATHER_ROWS):
        rs = slice(c * GATHER_ROWS, (c + 1) * GATHER_ROWS)
        x_ref[rs, :] = _dot(gather[rs, :], hb).astype(BF16)
    rank_ref[0] = rank
    comb_ref[0] = comb
    seg_ref[0] = seg_b
    start_ref[0] = start_b


def _ffn_kernel(dump_base, nt_max, tile_expert_ref, tile_in_expert_ref, n_chunks_ref, n_tiles_ref,
                *refs):
    tabs = refs[:IN_BUFS]
    x_hbm, wg_ref, wu_ref, wd_ref, o_hbm, ibuf, obuf, wgb, wub, wdb, in_sem, out_sem = refs[IN_BUFS:]
    tab0_ref = tabs[0]
    i = pl.program_id(0)
    n_tiles = n_tiles_ref[0]

    def start_in(ahead, slot):
        exists = i + ahead < n_tiles
        first = jnp.where(exists, tile_in_expert_ref[jnp.minimum(i + ahead, nt_max - 1)] * TILE_CHUNKS, 0)
        srcs = [jnp.where(exists, tabs[ahead][0, 0, first + j], ZERO_CHUNK) for j in range(TILE_CHUNKS)]
        for j in range(TILE_CHUNKS):
            pltpu.make_async_copy(x_hbm.at[srcs[j]], ibuf.at[slot, j], in_sem.at[slot]).start()

    def start_out(slot):
        e = tile_expert_ref[i]
        first = tile_in_expert_ref[i] * TILE_CHUNKS
        n_real = n_chunks_ref[e] - first
        dsts = [jnp.where(j < n_real, tab0_ref[0, 0, first + j], dump_base + slot * TILE_CHUNKS + j)
                for j in range(TILE_CHUNKS)]
        for j in range(TILE_CHUNKS):
            pltpu.make_async_copy(obuf.at[slot, j], o_hbm.at[dsts[j]], out_sem.at[slot]).start()

    def in_wait(slot):
        for j in range(TILE_CHUNKS):
            pltpu.make_async_copy(x_hbm.at[0], ibuf.at[slot, j], in_sem.at[slot]).wait()

    def out_wait(slot):
        for j in range(TILE_CHUNKS):
            pltpu.make_async_copy(obuf.at[slot, j], o_hbm.at[0], out_sem.at[slot]).wait()

    @pl.when(i < n_tiles)
    def _():
        islot = lax.rem(i, IN_BUFS)
        nslot = jnp.where(islot == 0, IN_BUFS - 1, islot - 1)
        oslot = lax.rem(i, 2)

        @pl.when(i == 0)
        def _():
            for ahead in range(IN_BUFS - 1):
                start_in(ahead, ahead)

        @pl.when(i >= 2)
        def _():
            out_wait(oslot)

        in_wait(islot)

        @pl.when(tile_in_expert_ref[i] == 0)
        def _():
            wgb[...] = wg_ref[0].astype(BF16)
            wub[...] = wu_ref[0].astype(BF16)
            wdb[...] = wd_ref[0].astype(BF16)

        start_in(IN_BUFS - 1, nslot)
        x = ibuf[islot].reshape(TILE_ROWS, D_MODEL)
        hid = (_silu(_dot(x, wgb[...])) * _dot(x, wub[...])).astype(BF16)
        obuf[oslot] = _dot(hid, wdb[...]).astype(BF16).reshape(TILE_CHUNKS, SEG, D_MODEL)
        start_out(oslot)

        @pl.when(i == n_tiles - 1)
        def _():
            for ahead in range(1, IN_BUFS):
                in_wait(lax.rem(i + ahead, IN_BUFS))
            out_wait(oslot)

            @pl.when(i >= 1)
            def _():
                out_wait(1 - oslot)


def _combine_kernel(h_ref, o_ref, rank_ref, comb_ref, seg_ref, start_ref, wsg_ref, wsu_ref, wsd_ref, ln2g_ref,
                    ln2b_ref, y_ref):
    h = h_ref[...]
    hb = h.astype(BF16)
    hits, (comb_rows,) = _row_lookup(seg_ref[0], start_ref[0], rank_ref[0], [comb_ref[0]])
    scatter = jnp.where(hits, comb_rows, 0.0).astype(BF16)
    routed = _dot_tn(scatter, o_ref[...])
    shared = _dot((_silu(_dot(hb, wsg_ref[...])) * _dot(hb, wsu_ref[...])).astype(BF16), wsd_ref[...])
    y_ref[...] = _layernorm(ALPHA * h + (routed + shared), ln2g_ref[...], ln2b_ref[...])


def _full(shape):
    return pl.BlockSpec(shape, lambda *_: (0,) * len(shape))


def _const(shape):
    return pl.BlockSpec(shape, lambda *_: (0,) * len(shape), pipeline_mode=pl.Buffered(1))


def _mixer_prompt(x, wts):
    bsz, seq, _ = x.shape
    n_t = seq // PROMPT_TILE
    weights = [wts[k] for k in ("win", "waup", "ba", "glag", "lng", "lnb", "ws", "bsf_prompt", "gmg", "wout",
                                "ln1g", "ln1b")]
    return pl.pallas_call(
        _mixer_prompt_kernel,
        grid=(bsz // PROMPT_SEQS, n_t),
        in_specs=[pl.BlockSpec((PROMPT_SEQS, PROMPT_TILE, D_MODEL), lambda b, t: (b, t, 0))]
        + [_const(w.shape) for w in weights],
        out_specs=[pl.BlockSpec((PROMPT_SEQS, PROMPT_TILE, D_MODEL), lambda b, t: (b, t, 0)),
                   pl.BlockSpec((PROMPT_SEQS, GLA_HEADS, GLA_DK, GLA_DV), lambda b, t: (b, 0, 0, 0))],
        out_shape=[jax.ShapeDtypeStruct((bsz, seq, D_MODEL), F32),
                   jax.ShapeDtypeStruct((bsz, GLA_HEADS, GLA_DK, GLA_DV), F32)],
        scratch_shapes=[pltpu.VMEM((PROMPT_SEQS * PROMPT_TILE, N_PROJ), F32),
                        pltpu.VMEM((PROMPT_SEQS * PROMPT_TILE, D_MODEL), BF16)]
        + [pltpu.VMEM((GLA_HEADS, GLA_DV, LANES), F32)] * PROMPT_SEQS,
        compiler_params=pltpu.CompilerParams(dimension_semantics=("arbitrary", "arbitrary"),
                                             vmem_limit_bytes=VMEM_LIMIT),
        name="mixer_prompt",
    )(x, *weights)


def _mixer_sample(x, s0, wts):
    bsz, seq_len, _ = x.shape
    n = SAMPLE_SEQS * seq_len
    x2 = x.reshape(bsz * seq_len, D_MODEL)
    weights = [wts[k] for k in ("win", "wkt", "walrt", "waup", "waupt", "ba", "bac", "glag", "lng", "lnb",
                                "ws_sample", "bsf_sample", "gmg", "wout", "ln1g", "ln1b")]
    state_spec = pl.BlockSpec((SAMPLE_SEQS, GLA_HEADS, GLA_DK, GLA_DV), lambda i: (i, 0, 0, 0))
    h, s_new, vg = pl.pallas_call(
        functools.partial(_mixer_sample_kernel, seq_len),
        grid=(bsz // SAMPLE_SEQS,),
        in_specs=[pl.BlockSpec((n, D_MODEL), lambda i: (i, 0)), state_spec] + [_full(w.shape) for w in weights],
        out_specs=[pl.BlockSpec((n, D_MODEL), lambda i: (i, 0)), state_spec,
                   pl.BlockSpec((n, GMLP_WIDTH), lambda i: (i, 0))],
        out_shape=[jax.ShapeDtypeStruct((bsz * seq_len, D_MODEL), F32),
                   jax.ShapeDtypeStruct(s0.shape, F32),
                   jax.ShapeDtypeStruct((bsz * seq_len, GMLP_WIDTH), F32)],
        scratch_shapes=[pltpu.VMEM((n, D_MODEL), BF16)],
        compiler_params=pltpu.CompilerParams(dimension_semantics=("arbitrary",), vmem_limit_bytes=VMEM_LIMIT),
        name="mixer_sample",
    )(x2, s0, *weights)
    return h, s_new, vg.reshape(bsz, seq_len, GMLP_WIDTH)


def _dispatch(h_p, h_s, wts):
    npb, nsb = h_p.shape[0] // BLOCK, h_s.shape[0] // BLOCK
    nb_real = npb + nsb
    nb = nb_real + DUMP_BLOCKS
    tok_spec = pl.BlockSpec((1, N_EXPERTS, BLOCK), lambda i: (i, 0, 0))
    run_spec = pl.BlockSpec((1, N_EXPERTS, LANES), lambda i: (i, 0, 0))
    return pl.pallas_call(
        functools.partial(_dispatch_kernel, npb, nb_real),
        grid=(nb,),
        in_specs=[pl.BlockSpec((BLOCK, D_MODEL), lambda i: (jnp.minimum(i, npb - 1), 0)),
                  pl.BlockSpec((BLOCK, D_MODEL), lambda i: (jnp.clip(i - npb, 0, nsb - 1), 0)),
                  _full(wts["wrt"].shape), _full(wts["rbias"].shape)],
        out_specs=[pl.BlockSpec((BLOCK_ROWS, D_MODEL), lambda i: (i, 0)), tok_spec, tok_spec, run_spec, run_spec],
        out_shape=[jax.ShapeDtypeStruct((nb * BLOCK_ROWS, D_MODEL), BF16),
                   jax.ShapeDtypeStruct((nb, N_EXPERTS, BLOCK), F32),
                   jax.ShapeDtypeStruct((nb, N_EXPERTS, BLOCK), F32),
                   jax.ShapeDtypeStruct((nb, N_EXPERTS, LANES), F32),
                   jax.ShapeDtypeStruct((nb, N_EXPERTS, LANES), F32)],
        compiler_params=pltpu.CompilerParams(dimension_semantics=("arbitrary",), vmem_limit_bytes=VMEM_LIMIT),
        name="moe_dispatch",
    )(h_p, h_s, wts["wrt"], wts["rbias"])


def _max_tiles(nb):
    return (nb * (BLOCK_ROWS_USED // SEG) + N_EXPERTS * (TILE_CHUNKS - 1)) // TILE_CHUNKS + 1


def _max_expert_chunks(nb):
    most = nb * (BLOCK // SEG) + TILE_CHUNKS - 1
    return -(-most // LANES) * LANES


def _plan(seg_lanes, nb):
    nt_max = _max_tiles(nb)
    max_chunks = _max_expert_chunks(nb)
    seg = seg_lanes[:nb, :, 0].astype(jnp.int32)
    start = jnp.cumsum(seg, axis=1) - seg
    nch = seg // SEG
    ends = jnp.cumsum(nch, axis=0)
    n_chunks = ends[-1]
    f = jnp.arange(nb, dtype=jnp.int32)[:, None] * (BLOCK_ROWS // SEG) + start // SEG - (ends - nch)
    q = jnp.arange(max_chunks, dtype=jnp.int32)
    passed = (ends[None, :-1, :] <= q[:, None, None]).astype(jnp.int32)
    chunk = q[:, None] + f[0][None, :] + jnp.sum(passed * (f[1:] - f[:-1])[None], axis=1)
    table = jnp.where(q[:, None] < n_chunks[None, :], chunk, ZERO_CHUNK).T
    n_tiles_e = -(-n_chunks // TILE_CHUNKS)
    tile_ends = jnp.cumsum(n_tiles_e)
    t = jnp.arange(nt_max, dtype=jnp.int32)
    done = (tile_ends[None, :] <= t[:, None]).astype(jnp.int32)
    tile_expert = jnp.minimum(jnp.sum(done, axis=1), N_EXPERTS - 1)
    tile_in_expert = t - jnp.sum(done * n_tiles_e[None, :], axis=1)
    return (table.astype(jnp.int32).reshape(N_EXPERTS, 1, max_chunks), tile_expert.astype(jnp.int32),
            tile_in_expert.astype(jnp.int32), n_chunks.astype(jnp.int32), tile_ends[-1:].astype(jnp.int32))


def _ffn(x_rows, nb, table, tile_expert, tile_in_expert, n_chunks, n_tiles, w_gate, w_up, w_down):
    nt_max = tile_expert.shape[0]
    max_chunks = table.shape[-1]
    tab_spec = lambda ahead: pl.BlockSpec(
        (1, 1, max_chunks), lambda i, te, *_: (te[jnp.clip(i + ahead, 0, nt_max - 1)], 0, 0),
        memory_space=pltpu.SMEM)
    w_spec = lambda shape: pl.BlockSpec((1,) + shape, lambda i, te, *_: (te[i], 0, 0))
    assert x_rows.shape[0] - nb * BLOCK_ROWS >= 2 * TILE_ROWS
    x_chunks = x_rows.reshape(x_rows.shape[0] // SEG, SEG, D_MODEL)
    return pl.pallas_call(
        functools.partial(_ffn_kernel, nb * BLOCK_ROWS // SEG, nt_max),
        grid_spec=pltpu.PrefetchScalarGridSpec(
            num_scalar_prefetch=4,
            grid=(nt_max,),
            in_specs=[tab_spec(ahead) for ahead in range(IN_BUFS)] + [pl.BlockSpec(memory_space=pl.ANY),
                      w_spec((D_MODEL, D_EXPERT)), w_spec((D_MODEL, D_EXPERT)), w_spec((D_EXPERT, D_MODEL))],
            out_specs=pl.BlockSpec(memory_space=pl.ANY),
            scratch_shapes=[pltpu.VMEM((IN_BUFS, TILE_CHUNKS, SEG, D_MODEL), BF16),
                            pltpu.VMEM((2, TILE_CHUNKS, SEG, D_MODEL), BF16),
                            pltpu.VMEM((D_MODEL, D_EXPERT), BF16),
                            pltpu.VMEM((D_MODEL, D_EXPERT), BF16),
                            pltpu.VMEM((D_EXPERT, D_MODEL), BF16),
                            pltpu.SemaphoreType.DMA((IN_BUFS,)),
                            pltpu.SemaphoreType.DMA((2,))]),
        out_shape=jax.ShapeDtypeStruct(x_chunks.shape, BF16),
        input_output_aliases={4 + IN_BUFS: 0},
        compiler_params=pltpu.CompilerParams(dimension_semantics=("arbitrary",), vmem_limit_bytes=VMEM_LIMIT),
        name="moe_ffn",
    )(tile_expert, tile_in_expert, n_chunks, n_tiles, *([table] * IN_BUFS), x_chunks, w_gate, w_up, w_down
      ).reshape(x_rows.shape)


def _combine(h, o_rows, routing, block_off, wts):
    nblk = h.shape[0] // BLOCK
    weights = [wts[k] for k in ("wsg", "wsu", "wsd", "ln2g", "ln2b")]
    tok_spec = pl.BlockSpec((1, N_EXPERTS, BLOCK), lambda i: (i + block_off, 0, 0))
    run_spec = pl.BlockSpec((1, N_EXPERTS, LANES), lambda i: (i + block_off, 0, 0))
    return pl.pallas_call(
        _combine_kernel,
        grid=(nblk,),
        in_specs=[pl.BlockSpec((BLOCK, D_MODEL), lambda i: (i, 0)),
                  pl.BlockSpec((BLOCK_ROWS, D_MODEL), lambda i: (i + block_off, 0)),
                  tok_spec, tok_spec, run_spec, run_spec] + [_full(w.shape) for w in weights],
        out_specs=pl.BlockSpec((BLOCK, D_MODEL), lambda i: (i, 0)),
        out_shape=jax.ShapeDtypeStruct(h.shape, F32),
        compiler_params=pltpu.CompilerParams(dimension_semantics=("arbitrary",), vmem_limit_bytes=VMEM_LIMIT),
        name="moe_combine",
    )(h, o_rows, *routing, *weights)


def _moe(h_p, h_s, w_gate, w_up, w_down, wts):
    nb = (h_p.shape[0] + h_s.shape[0]) // BLOCK
    x_rows, rank, comb, seg_lanes, start_lanes = _dispatch(h_p, h_s, wts)
    o_rows = _ffn(x_rows, nb, *_plan(seg_lanes, nb), w_gate, w_up, w_down)
    routing = (rank, comb, seg_lanes, start_lanes)
    y_p = _combine(h_p, o_rows, routing, 0, wts)
    y_s = _combine(h_s, o_rows, routing, h_p.shape[0] // BLOCK, wts)
    return y_p, y_s


def _prep_weights(seq_len, w_in, w_a_up, b_a, gla_norm_g, gmlp_ln_g, gmlp_ln_b, w_s, b_s, gmlp_norm_g, w_out,
                  ln1_g, ln1_b, w_router, router_bias, ws_gate, ws_up, ws_down, ln2_g, ln2_b):
    o1 = QK_WIDTH
    o2 = o1 + QK_WIDTH
    o3 = o2 + GLA_WIDTH
    o4 = o3 + GLA_WIDTH
    o5 = o4 + GLA_RANK
    o6 = o5 + GMLP_WIDTH
    wq, wk, wva, wg_, walr, wu_, wvb = jnp.split(w_in, [o1, o2, o3, o4, o5, o6], axis=-1)
    walr_p = jnp.pad(walr, ((0, 0), (0, LANES - GLA_RANK)))
    waup_p = jnp.pad(w_a_up, ((0, LANES - GLA_RANK), (0, 0)))
    row = lambda a: a.reshape(1, -1)
    reps = (SAMPLE_SEQS * seq_len) // seq_len
    ws_small = w_s[:, :seq_len, :seq_len]
    return {
        "win": jnp.concatenate([wq, wk, wva, wg_, wu_, wvb, walr_p], axis=-1).astype(BF16),
        "wkt": wk.T.astype(BF16),
        "walrt": walr_p.T.astype(BF16),
        "waup": waup_p.astype(BF16),
        "waupt": waup_p.T.astype(BF16),
        "ba": row(b_a), "bac": b_a.reshape(-1, 1),
        "glag": gla_norm_g, "lng": row(gmlp_ln_g), "lnb": row(gmlp_ln_b),
        "ws": w_s,
        "ws_sample": jnp.pad(ws_small, ((0, 0), (0, LANES - seq_len), (0, LANES - seq_len))),
        "bsf_prompt": jnp.repeat(b_s[:, :GMLP_CHUNK].T, GMLP_DH, axis=1),
        "bsf_sample": jnp.tile(jnp.repeat(b_s[:, :seq_len].T, GMLP_DH, axis=1), (reps, 1)),
        "gmg": gmlp_norm_g,
        "wout": w_out.astype(BF16),
        "ln1g": row(ln1_g), "ln1b": row(ln1_b),
        "wrt": w_router.T.astype(BF16), "rbias": router_bias.reshape(-1, 1),
        "wsg": ws_gate.astype(BF16), "wsu": ws_up.astype(BF16), "wsd": ws_down.astype(BF16),
        "ln2g": row(ln2_g), "ln2b": row(ln2_b),
    }


def kernel(x_prompt, x_sample, state_gla, w_in, w_a_up, b_a, gla_norm_g, gmlp_ln_g, gmlp_ln_b, w_s, b_s,
           gmlp_norm_g, w_out, ln1_g, ln1_b, w_router, router_bias, w_gate, w_up, w_down, ws_gate, ws_up,
           ws_down, ln2_g, ln2_b):
    assert x_prompt.shape[1] % PROMPT_TILE == 0 and x_sample.shape[0] % SAMPLE_SEQS == 0
    assert x_prompt.shape[0] % PROMPT_SEQS == 0
    assert x_sample.shape[1] <= GMLP_CHUNK and w_in.shape[0] == DEPTH
    assert (x_sample.shape[0] * x_sample.shape[1]) % BLOCK == 0 and PROMPT_TILE % BLOCK == 0
    bsz, seq, _ = x_prompt.shape
    dbsz, dseq, _ = x_sample.shape
    hp, hs = x_prompt, x_sample
    gla_p, gla_s, v_s = [], [], []
    for l in range(DEPTH):
        wts = _prep_weights(dseq, w_in[l], w_a_up[l], b_a[l], gla_norm_g[l], gmlp_ln_g[l], gmlp_ln_b[l], w_s[l],
                            b_s[l], gmlp_norm_g[l], w_out[l], ln1_g[l], ln1_b[l], w_router[l], router_bias[l],
                            ws_gate[l], ws_up[l], ws_down[l], ln2_g[l], ln2_b[l])
        h_p, sp = _mixer_prompt(hp, wts)
        h_s, ss, vrows = _mixer_sample(hs, state_gla[l], wts)
        y_p, y_s = _moe(h_p.reshape(bsz * seq, D_MODEL), h_s, w_gate[l], w_up[l], w_down[l], wts)
        hp = y_p.reshape(bsz, seq, D_MODEL)
        hs = y_s.reshape(dbsz, dseq, D_MODEL)
        gla_p.append(sp)
        gla_s.append(ss)
        v_s.append(vrows)
    return (hp, hs, jnp.stack(gla_p), jnp.stack(gla_s), jnp.stack(v_s))
```

```python
import functools
import math

import jax
import jax.numpy as jnp
from jax import lax
from jax.experimental import pallas as pl
from jax.experimental.pallas import tpu as pltpu

F32 = jnp.float32
BF16 = jnp.bfloat16

D_MODEL = 1024
DEPTH = 1
GLA_WIDTH = 512
GLA_HEADS = 4
GLA_DK = 64
GLA_DV = 128
GLA_RANK = 16
GLA_TAU = 16.0
GMLP_WIDTH = 512
GMLP_HEADS = 4
GMLP_DH = 128
GMLP_CHUNK = 128
QK_WIDTH = GLA_HEADS * GLA_DK
N_EXPERTS = 64
TOP_K = 8
N_GROUPS = 8
GROUP_SIZE = N_EXPERTS // N_GROUPS
TOPK_GROUPS = 4
D_EXPERT = 256
D_SHARED = 256
ROUTE_SCALE = 2.5
ALPHA = (2.0 * DEPTH) ** 0.25

LANES = 128

C_Q = 0
C_K = C_Q + QK_WIDTH
C_VA = C_K + QK_WIDTH
C_G = C_VA + GLA_WIDTH
C_U = C_G + GLA_WIDTH
C_VB = C_U + GMLP_WIDTH
C_ALR = C_VB + GMLP_WIDTH
N_PROJ = C_ALR + LANES

CHUNK = 128
GLA_BLOCK = 256
PROMPT_TILE = 512
PROMPT_SEQS = 2
SAMPLE_SEQS = 32
VMEM_LIMIT = 56 * 1024 * 1024

BLOCK = 256
SEG = 16
TILE_CHUNKS = 64
TILE_ROWS = TILE_CHUNKS * SEG
GATHER_ROWS = 512
BLOCK_ROWS_USED = BLOCK * TOP_K + N_EXPERTS * (SEG - 1)
BLOCK_ROWS = -(-(BLOCK_ROWS_USED + SEG) // GATHER_ROWS) * GATHER_ROWS
DUMP_BLOCKS = -(-(2 * TILE_ROWS) // BLOCK_ROWS)
ZERO_CHUNK = BLOCK_ROWS_USED // SEG
IN_BUFS = 4


def _dot(a, b):
    return jnp.dot(a, b, preferred_element_type=F32)


def _dot_nt(a, b):
    return lax.dot_general(a, b, (((1,), (1,)), ((), ())), preferred_element_type=F32)


def _dot_tn(a, b):
    return lax.dot_general(a, b, (((0,), (0,)), ((), ())), preferred_element_type=F32)


def _shr(x, d):
    assert d & (d - 1) == 0
    return lax.shift_right_logical(x, d.bit_length() - 1)


def _split_dot(m01, x):
    hi = x.astype(BF16)
    lo = (x - hi.astype(F32)).astype(BF16)
    return _dot(m01, hi) + _dot(m01, lo)


def _split_dot_r(x, m01):
    hi = x.astype(BF16)
    lo = (x - hi.astype(F32)).astype(BF16)
    return _dot(hi, m01) + _dot(lo, m01)


def _sigmoid(x):
    return 1.0 / (1.0 + jnp.exp(-x))


def _silu(x):
    return x * _sigmoid(x)


def _gelu(x):
    c = math.sqrt(2.0 / math.pi)
    return x * (0.5 * (1.0 + jnp.tanh(c * (x + 0.044715 * (x * x * x)))))


def _log_sigmoid(x):
    return -(jnp.maximum(-x, 0.0) + jnp.log(1.0 + jnp.exp(-jnp.abs(x))))


def _layernorm(x, g, b, eps=1e-5):
    mu = jnp.mean(x, axis=-1, keepdims=True)
    xc = x - mu
    var = jnp.mean(xc * xc, axis=-1, keepdims=True)
    return xc * lax.rsqrt(var + eps) * g + b


def _rmsnorm(x, g, eps=1e-6):
    return x * lax.rsqrt(jnp.mean(x * x, axis=-1, keepdims=True) + eps) * g


def _gmlp_heads(u, vg, w_mix, bias, gmg_ref, merged_ref, rows):
    vgb = vg.astype(BF16)
    for h in range(GMLP_HEADS):
        cs = slice(h * GMLP_DH, (h + 1) * GMLP_DH)
        sgu = _dot(w_mix(h), vgb[:, cs]) + bias[:, cs]
        y = _rmsnorm(u[:, cs] * sgu, gmg_ref[h:h + 1, :])
        merged_ref[rows, GLA_WIDTH + h * GMLP_DH:GLA_WIDTH + (h + 1) * GMLP_DH] = y.astype(BF16)


def _out_proj_ln(x, merged_ref, wout_ref, ln1g_ref, ln1b_ref):
    m = _dot(merged_ref[...], wout_ref[...])
    return _layernorm(ALPHA * x + m, ln1g_ref[...], ln1b_ref[...])


def _mixer_prompt_kernel(x_ref, win_ref, waup_ref, ba_ref, glag_ref, lng_ref, lnb_ref, ws_ref, bsf_ref,
                         gmg_ref, wout_ref, ln1g_ref, ln1b_ref,
                         h_ref, state_ref,
                         z_ref, merged_ref, *st_refs):
    t = pl.program_id(1)

    @pl.when(t == 0)
    def _():
        for st_ref in st_refs:
            st_ref[...] = jnp.zeros_like(st_ref)

    x = x_ref[...].reshape(PROMPT_SEQS * PROMPT_TILE, D_MODEL)
    z_ref[...] = _dot(x.astype(BF16), win_ref[...])

    row_i = lax.broadcasted_iota(jnp.int32, (GLA_BLOCK, GLA_BLOCK), 0)
    col_i = lax.broadcasted_iota(jnp.int32, (GLA_BLOCK, GLA_BLOCK), 1)
    causal = row_i >= col_i
    tri = jnp.where(causal, 1.0, 0.0).astype(BF16)
    causal_mix = causal[:CHUNK, :CHUNK]
    lane = lax.broadcasted_iota(jnp.int32, (1, LANES), 1)
    head_lanes = [lane < GLA_DK, lane >= GLA_DK]
    mid = GLA_BLOCK // 2 - 1

    for c in range(PROMPT_SEQS * PROMPT_TILE // GLA_BLOCK):
        rows = slice(c * GLA_BLOCK, (c + 1) * GLA_BLOCK)
        st_ref = st_refs[c // (PROMPT_TILE // GLA_BLOCK)]
        a_pre = _dot(z_ref[rows, C_ALR:C_ALR + LANES].astype(BF16), waup_ref[...]) + ba_ref[...]
        log_a = _log_sigmoid(a_pre) * (1.0 / GLA_TAU)
        b = _split_dot(tri, log_a)
        b_mid = b[mid:mid + 1, :]
        b_last = b[GLA_BLOCK - 1:GLA_BLOCK, :]
        q = z_ref[rows, C_Q:C_Q + QK_WIDTH] * (GLA_DK ** -0.5)
        k = z_ref[rows, C_K:C_K + QK_WIDTH]
        q_in = (q * jnp.exp(b - b_mid)).astype(BF16)
        k_in = (k * jnp.exp(b_mid - b)).astype(BF16)
        q_st = (q * jnp.exp(b)).astype(BF16)
        k_st = (k * jnp.exp(b_last - b)).astype(BF16)
        d_last = jnp.exp(b_last)
        va = z_ref[rows, C_VA:C_VA + GLA_WIDTH].astype(BF16)
        for h in range(GLA_HEADS):
            ps = slice((h // 2) * LANES, (h // 2 + 1) * LANES)
            vs = slice(h * GLA_DV, (h + 1) * GLA_DV)
            hm = head_lanes[h % 2]
            zero = jnp.zeros((), BF16)
            a = _dot_nt(jnp.where(hm, q_in[:, ps], zero), k_in[:, ps])
            a = jnp.where(causal, a, 0.0).astype(BF16)
            st = st_ref[h]
            o = _dot(a, va[:, vs]) + _dot_nt(jnp.where(hm, q_st[:, ps], zero), st.astype(BF16))
            upd = _dot_tn(va[:, vs], jnp.where(hm, k_st[:, ps], zero))
            st_ref[h] = st * d_last[:, ps] + upd
            gate = z_ref[rows, C_G + h * GLA_DV:C_G + (h + 1) * GLA_DV]
            o = _rmsnorm(o, glag_ref[h:h + 1, :]) * _silu(gate)
            merged_ref[rows, vs] = o.astype(BF16)

    for c in range(PROMPT_SEQS * PROMPT_TILE // CHUNK):
        rows = slice(c * CHUNK, (c + 1) * CHUNK)
        u = _gelu(z_ref[rows, C_U:C_U + GMLP_WIDTH])
        vg = _layernorm(_gelu(z_ref[rows, C_VB:C_VB + GMLP_WIDTH]), lng_ref[...], lnb_ref[...])
        _gmlp_heads(u, vg, lambda h: jnp.where(causal_mix, ws_ref[h], 0.0).astype(BF16), bsf_ref[...],
                    gmg_ref, merged_ref, rows)

    h_ref[...] = _out_proj_ln(x, merged_ref, wout_ref, ln1g_ref, ln1b_ref).reshape(PROMPT_SEQS, PROMPT_TILE, D_MODEL)

    @pl.when(t == pl.num_programs(1) - 1)
    def _():
        for s, st_ref in enumerate(st_refs):
            for h in range(GLA_HEADS):
                lo = (h % 2) * GLA_DK
                state_ref[s, h] = st_ref[h].T[lo:lo + GLA_DK, :]


def _mixer_sample_kernel(seq_len, x_ref, s0_ref, win_ref, wkt_ref, walrt_ref, waup_ref, waupt_ref, ba_ref, bac_ref,
                         glag_ref, lng_ref, lnb_ref, wst_ref, bsf_ref, gmg_ref, wout_ref, ln1g_ref, ln1b_ref,
                         h_ref, snew_ref, vg_ref,
                         merged_ref):
    n = SAMPLE_SEQS * seq_len
    x = x_ref[...]
    xb = x.astype(BF16)
    z = _dot(xb, win_ref[...])

    ti = lax.broadcasted_iota(jnp.int32, (n, n), 0)
    tj = lax.broadcasted_iota(jnp.int32, (n, n), 1)
    same = _shr(ti, seq_len) == _shr(tj, seq_len)
    causal = jnp.logical_and(same, ti >= tj)
    tri = jnp.where(causal, 1.0, 0.0).astype(BF16)
    tri_t = jnp.where(jnp.logical_and(same, ti <= tj), 1.0, 0.0).astype(BF16)
    same01 = jnp.where(same, 1.0, 0.0).astype(BF16)

    a_pre = _dot(z[:, C_ALR:C_ALR + LANES].astype(BF16), waup_ref[...]) + ba_ref[...]
    log_a = _log_sigmoid(a_pre) * (1.0 / GLA_TAU)
    b = _split_dot(tri, log_a)
    q = z[:, C_Q:C_Q + QK_WIDTH] * (GLA_DK ** -0.5)
    k = z[:, C_K:C_K + QK_WIDTH]
    q_in = (q * jnp.exp(b)).astype(BF16)
    k_in = (k * jnp.exp(-b)).astype(BF16)
    va = z[:, C_VA:C_VA + GLA_WIDTH].astype(BF16)

    k_t = _dot_nt(wkt_ref[...], xb)
    alr_t = _dot_nt(walrt_ref[...], xb)
    a_pre_t = _dot(waupt_ref[...], alr_t.astype(BF16)) + bac_ref[...]
    log_a_t = _log_sigmoid(a_pre_t) * (1.0 / GLA_TAU)
    b_t = _split_dot_r(log_a_t, tri_t)
    tot_t = _split_dot_r(log_a_t, same01)
    k_st_t = k_t * jnp.exp(tot_t - b_t)
    d_t = jnp.exp(tot_t)

    lane = lax.broadcasted_iota(jnp.int32, (1, LANES), 1)
    head_lanes = [lane < GLA_DK, lane >= GLA_DK]
    nrow = SAMPLE_SEQS * GLA_DK
    r_seq = _shr(lax.broadcasted_iota(jnp.int32, (nrow, n), 0), GLA_DK)
    c_seq = _shr(lax.broadcasted_iota(jnp.int32, (nrow, n), 1), seq_len)
    c_first = (lax.broadcasted_iota(jnp.int32, (nrow, n), 1) & (seq_len - 1)) == 0
    blk = r_seq == c_seq
    blk_first = jnp.logical_and(blk, c_first)
    q_seq = _shr(lax.broadcasted_iota(jnp.int32, (n, nrow), 0), seq_len)
    q_col = _shr(lax.broadcasted_iota(jnp.int32, (n, nrow), 1), GLA_DK)
    blk_q = q_seq == q_col

    for h in range(GLA_HEADS):
        ps = slice((h // 2) * LANES, (h // 2 + 1) * LANES)
        ds_ = slice(h * GLA_DK, (h + 1) * GLA_DK)
        vs = slice(h * GLA_DV, (h + 1) * GLA_DV)
        hm = head_lanes[h % 2]
        zero = jnp.zeros((), BF16)
        a = _dot_nt(jnp.where(hm, q_in[:, ps], zero), k_in[:, ps])
        a = jnp.where(causal, a, 0.0).astype(BF16)
        s0 = s0_ref[:, h].reshape(nrow, GLA_DV)
        q_h = q_in[:, ds_]
        q_bd = jnp.where(blk_q, jnp.concatenate([q_h] * SAMPLE_SEQS, axis=1), zero)
        o = _dot(a, va[:, vs]) + _dot(q_bd, s0.astype(BF16))
        k_bd = jnp.where(blk, jnp.concatenate([k_st_t[ds_, :]] * SAMPLE_SEQS, axis=0), 0.0).astype(BF16)
        upd = _dot(k_bd, va[:, vs])
        d_bd = jnp.where(blk_first, jnp.concatenate([d_t[ds_, :]] * SAMPLE_SEQS, axis=0), 0.0)
        d_col = jnp.sum(d_bd, axis=1, keepdims=True)
        snew_ref[:, h] = (s0 * d_col + upd).reshape(SAMPLE_SEQS, GLA_DK, GLA_DV)
        gate = z[:, C_G + h * GLA_DV:C_G + (h + 1) * GLA_DV]
        o = _rmsnorm(o, glag_ref[h:h + 1, :]) * _silu(gate)
        merged_ref[:, vs] = o.astype(BF16)

    u = _gelu(z[:, C_U:C_U + GMLP_WIDTH])
    vg = _layernorm(_gelu(z[:, C_VB:C_VB + GMLP_WIDTH]), lng_ref[...], lnb_ref[...])
    vg_ref[...] = vg
    pi = lax.broadcasted_iota(jnp.int32, (n, LANES), 0)
    pj = lax.broadcasted_iota(jnp.int32, (n, LANES), 1)
    pos = jnp.where((pi & (seq_len - 1)) == pj, 1.0, 0.0).astype(BF16)

    def w_mix(h):
        rows = _dot(pos, wst_ref[h].astype(BF16)).astype(BF16)
        return jnp.where(causal, _dot_nt(rows, pos), 0.0).astype(BF16)

    _gmlp_heads(u, vg, w_mix, bsf_ref[...], gmg_ref, merged_ref, slice(None))
    h_ref[...] = _out_proj_ln(x, merged_ref, wout_ref, ln1g_ref, ln1b_ref)


def _route(hb, wrt_ref, rbias_ref):
    tm = hb.shape[0]
    s = _sigmoid(_dot_nt(wrt_ref[...], hb))
    sb = s + rbias_ref[...]
    neg = jnp.float32(-jnp.inf)
    sub = lax.broadcasted_iota(jnp.int32, (GROUP_SIZE, tm), 0)
    gscore = []
    for g in range(N_GROUPS):
        blk = sb[g * GROUP_SIZE:(g + 1) * GROUP_SIZE, :]
        m1 = jnp.max(blk, axis=0, keepdims=True)
        i1 = jnp.min(jnp.where(blk == m1, sub, GROUP_SIZE), axis=0, keepdims=True)
        m2 = jnp.max(jnp.where(sub == i1, neg, blk), axis=0, keepdims=True)
        gscore.append(m1 + m2)
    gsel = []
    for g in range(N_GROUPS):
        rank = jnp.zeros((1, tm), jnp.int32)
        for o in range(N_GROUPS):
            if o == g:
                continue
            ahead = (gscore[o] >= gscore[g]) if o < g else (gscore[o] > gscore[g])
            rank = rank + ahead.astype(jnp.int32)
        gsel.append(rank < TOPK_GROUPS)
    v = jnp.concatenate(
        [jnp.where(gsel[g], sb[g * GROUP_SIZE:(g + 1) * GROUP_SIZE, :], neg) for g in range(N_GROUPS)], axis=0)
    eidx = lax.broadcasted_iota(jnp.int32, (N_EXPERTS, tm), 0)
    sel = jnp.zeros((N_EXPERTS, tm), F32)
    for _ in range(TOP_K):
        m = jnp.max(v, axis=0, keepdims=True)
        first = jnp.min(jnp.where(v == m, eidx, N_EXPERTS), axis=0, keepdims=True)
        hit = eidx == first
        sel = jnp.where(hit, 1.0, sel)
        v = jnp.where(hit, neg, v)
    wsel = s * sel
    return wsel / jnp.sum(wsel, axis=0, keepdims=True) * ROUTE_SCALE, sel


def _per_row(chunk_vals):
    n, t = chunk_vals.shape
    return jnp.broadcast_to(chunk_vals[:, None, :], (n, SEG, t)).reshape(n * SEG, t)


def _row_lookup(seg_b, start_b, rank, tables):
    n_chunks = BLOCK_ROWS // SEG
    ei = lax.broadcasted_iota(jnp.int32, (N_EXPERTS, LANES), 0)
    ej = lax.broadcasted_iota(jnp.int32, (N_EXPERTS, LANES), 1)
    seg16 = (seg_b * (1.0 / SEG)).astype(BF16)
    start_row = _dot_tn(seg16, jnp.where(ei < ej, 1.0, 0.0).astype(BF16))[0:1, :]
    seg_row = _dot_tn(seg16, jnp.where(ei == ej, 1.0, 0.0).astype(BF16))[0:1, :]
    c = lax.broadcasted_iota(jnp.int32, (n_chunks, LANES), 0).astype(F32)
    owner = jnp.where(c >= start_row, jnp.where(c < start_row + seg_row, 1.0, 0.0), 0.0).astype(BF16)

    def lookup(tab):
        padded = jnp.concatenate([tab.astype(BF16), jnp.zeros((LANES - N_EXPERTS, tab.shape[1]), BF16)], axis=0)
        return _dot(owner, padded)

    first = SEG * (c - lookup(start_b * (1.0 / SEG)))[:, 0:1]
    rel = lookup(rank) - first
    sub = lax.broadcasted_iota(jnp.int32, (BLOCK_ROWS, rank.shape[1]), 0) & (SEG - 1)
    hits = _per_row(rel) == sub.astype(F32)
    return hits, [_per_row(lookup(tab)) for tab in tables]


def _dispatch_kernel(n_prompt_blocks, n_blocks, hp_ref, hs_ref, wrt_ref, rbias_ref, x_ref, rank_ref, comb_ref,
                     seg_ref, start_ref):
    i = pl.program_id(0)

    @pl.when(i < n_blocks)
    def _():
        _dispatch_block(i < n_prompt_blocks, hp_ref, hs_ref, wrt_ref, rbias_ref, x_ref, rank_ref, comb_ref, seg_ref,
                        start_ref)

    @pl.when(i >= n_blocks)
    def _():
        for ref in (x_ref, rank_ref, comb_ref, seg_ref, start_ref):
            ref[...] = jnp.zeros_like(ref)


def _dispatch_block(is_prompt, hp_ref, hs_ref, wrt_ref, rbias_ref, x_ref, rank_ref, comb_ref, seg_ref, start_ref):
    tm = BLOCK
    hb = jnp.where(is_prompt, hp_ref[...], hs_ref[...]).astype(BF16)
    comb, sel = _route(hb, wrt_ref, rbias_ref)
    ti = lax.broadcasted_iota(jnp.int32, (tm, tm), 0)
    tj = lax.broadcasted_iota(jnp.int32, (tm, tm), 1)
    before = jnp.where(ti < tj, 1.0, 0.0).astype(BF16)
    rank = _dot(sel.astype(BF16), before)
    rank = jnp.where(sel > 0.0, rank, -1.0)
    cnt = jnp.sum(sel, axis=1, keepdims=True)
    seg = jnp.floor((cnt + (SEG - 1.0)) * (1.0 / SEG)) * SEG
    ei = lax.broadcasted_iota(jnp.int32, (N_EXPERTS, N_EXPERTS), 0)
    ej = lax.broadcasted_iota(jnp.int32, (N_EXPERTS, N_EXPERTS), 1)
    below = jnp.where(ej < ei, 1.0, 0.0).astype(BF16)
    seg_b = jnp.broadcast_to(seg, (N_EXPERTS, LANES))
    start_b = _dot(below, seg_b.astype(BF16))
    hits, _ = _row_lookup(seg_b, start_b, rank, [])
    gather = jnp.where(hits, 1.0, 0.0).astype(BF16)
    for c in range(BLOCK_ROWS // GATHER_ROWS):
        rs = slice(c * GATHER_ROWS, (c + 1) * GATHER_ROWS)
        x_ref[rs, :] = _dot(gather[rs, :], hb).astype(BF16)
    rank_ref[0] = rank
    comb_ref[0] = comb
    seg_ref[0] = seg_b
    start_ref[0] = start_b


def _ffn_kernel(dump_base, nt_max, tile_expert_ref, tile_in_expert_ref, n_chunks_ref, n_tiles_ref,
                *refs):
    tabs = refs[:IN_BUFS]
    x_hbm, wg_ref, wu_ref, wd_ref, o_hbm, ibuf, obuf, wgb, wub, wdb, in_sem, out_sem = refs[IN_BUFS:]
    tab0_ref = tabs[0]
    i = pl.program_id(0)
    n_tiles = n_tiles_ref[0]

    def start_in(ahead, slot):
        exists = i + ahead < n_tiles
        first = jnp.where(exists, tile_in_expert_ref[jnp.minimum(i + ahead, nt_max - 1)] * TILE_CHUNKS, 0)
        srcs = [jnp.where(exists, tabs[ahead][0, 0, first + j], ZERO_CHUNK) for j in range(TILE_CHUNKS)]
        for j in range(TILE_CHUNKS):
            pltpu.make_async_copy(x_hbm.at[srcs[j]], ibuf.at[slot, j], in_sem.at[slot]).start()

    def start_out(slot):
        e = tile_expert_ref[i]
        first = tile_in_expert_ref[i] * TILE_CHUNKS
        n_real = n_chunks_ref[e] - first
        dsts = [jnp.where(j < n_real, tab0_ref[0, 0, first + j], dump_base + slot * TILE_CHUNKS + j)
                for j in range(TILE_CHUNKS)]
        for j in range(TILE_CHUNKS):
            pltpu.make_async_copy(obuf.at[slot, j], o_hbm.at[dsts[j]], out_sem.at[slot]).start()

    def in_wait(slot):
        for j in range(TILE_CHUNKS):
            pltpu.make_async_copy(x_hbm.at[0], ibuf.at[slot, j], in_sem.at[slot]).wait()

    def out_wait(slot):
        for j in range(TILE_CHUNKS):
            pltpu.make_async_copy(obuf.at[slot, j], o_hbm.at[0], out_sem.at[slot]).wait()

    @pl.when(i < n_tiles)
    def _():
        islot = lax.rem(i, IN_BUFS)
        nslot = jnp.where(islot == 0, IN_BUFS - 1, islot - 1)
        oslot = lax.rem(i, 2)

        @pl.when(i == 0)
        def _():
            for ahead in range(IN_BUFS - 1):
                start_in(ahead, ahead)

        @pl.when(i >= 2)
        def _():
            out_wait(oslot)

        in_wait(islot)

        @pl.when(tile_in_expert_ref[i] == 0)
        def _():
            wgb[...] = wg_ref[0].astype(BF16)
            wub[...] = wu_ref[0].astype(BF16)
            wdb[...] = wd_ref[0].astype(BF16)

        start_in(IN_BUFS - 1, nslot)
        x = ibuf[islot].reshape(TILE_ROWS, D_MODEL)
        hid = (_silu(_dot(x, wgb[...])) * _dot(x, wub[...])).astype(BF16)
        obuf[oslot] = _dot(hid, wdb[...]).astype(BF16).reshape(TILE_CHUNKS, SEG, D_MODEL)
        start_out(oslot)

        @pl.when(i == n_tiles - 1)
        def _():
            for ahead in range(1, IN_BUFS):
                in_wait(lax.rem(i + ahead, IN_BUFS))
            out_wait(oslot)

            @pl.when(i >= 1)
            def _():
                out_wait(1 - oslot)


def _combine_kernel(h_ref, o_ref, rank_ref, comb_ref, seg_ref, start_ref, wsg_ref, wsu_ref, wsd_ref, ln2g_ref,
                    ln2b_ref, y_ref):
    h = h_ref[...]
    hb = h.astype(BF16)
    hits, (comb_rows,) = _row_lookup(seg_ref[0], start_ref[0], rank_ref[0], [comb_ref[0]])
    scatter = jnp.where(hits, comb_rows, 0.0).astype(BF16)
    routed = _dot_tn(scatter, o_ref[...])
    shared = _dot((_silu(_dot(hb, wsg_ref[...])) * _dot(hb, wsu_ref[...])).astype(BF16), wsd_ref[...])
    y_ref[...] = _layernorm(ALPHA * h + (routed + shared), ln2g_ref[...], ln2b_ref[...])


def _full(shape):
    return pl.BlockSpec(shape, lambda *_: (0,) * len(shape))


def _const(shape):
    return pl.BlockSpec(shape, lambda *_: (0,) * len(shape), pipeline_mode=pl.Buffered(1))


def _mixer_prompt(x, wts):
    bsz, seq, _ = x.shape
    n_t = seq // PROMPT_TILE
    weights = [wts[k] for k in ("win", "waup", "ba", "glag", "lng", "lnb", "ws", "bsf_prompt", "gmg", "wout",
                                "ln1g", "ln1b")]
    return pl.pallas_call(
        _mixer_prompt_kernel,
        grid=(bsz // PROMPT_SEQS, n_t),
        in_specs=[pl.BlockSpec((PROMPT_SEQS, PROMPT_TILE, D_MODEL), lambda b, t: (b, t, 0))]
        + [_const(w.shape) for w in weights],
        out_specs=[pl.BlockSpec((PROMPT_SEQS, PROMPT_TILE, D_MODEL), lambda b, t: (b, t, 0)),
                   pl.BlockSpec((PROMPT_SEQS, GLA_HEADS, GLA_DK, GLA_DV), lambda b, t: (b, 0, 0, 0))],
        out_shape=[jax.ShapeDtypeStruct((bsz, seq, D_MODEL), F32),
                   jax.ShapeDtypeStruct((bsz, GLA_HEADS, GLA_DK, GLA_DV), F32)],
        scratch_shapes=[pltpu.VMEM((PROMPT_SEQS * PROMPT_TILE, N_PROJ), F32),
                        pltpu.VMEM((PROMPT_SEQS * PROMPT_TILE, D_MODEL), BF16)]
        + [pltpu.VMEM((GLA_HEADS, GLA_DV, LANES), F32)] * PROMPT_SEQS,
        compiler_params=pltpu.CompilerParams(dimension_semantics=("arbitrary", "arbitrary"),
                                             vmem_limit_bytes=VMEM_LIMIT),
        name="mixer_prompt",
    )(x, *weights)


def _mixer_sample(x, s0, wts):
    bsz, seq_len, _ = x.shape
    n = SAMPLE_SEQS * seq_len
    x2 = x.reshape(bsz * seq_len, D_MODEL)
    weights = [wts[k] for k in ("win", "wkt", "walrt", "waup", "waupt", "ba", "bac", "glag", "lng", "lnb",
                                "ws_sample", "bsf_sample", "gmg", "wout", "ln1g", "ln1b")]
    state_spec = pl.BlockSpec((SAMPLE_SEQS, GLA_HEADS, GLA_DK, GLA_DV), lambda i: (i, 0, 0, 0))
    h, s_new, vg = pl.pallas_call(
        functools.partial(_mixer_sample_kernel, seq_len),
        grid=(bsz // SAMPLE_SEQS,),
        in_specs=[pl.BlockSpec((n, D_MODEL), lambda i: (i, 0)), state_spec] + [_full(w.shape) for w in weights],
        out_specs=[pl.BlockSpec((n, D_MODEL), lambda i: (i, 0)), state_spec,
                   pl.BlockSpec((n, GMLP_WIDTH), lambda i: (i, 0))],
        out_shape=[jax.ShapeDtypeStruct((bsz * seq_len, D_MODEL), F32),
                   jax.ShapeDtypeStruct(s0.shape, F32),
                   jax.ShapeDtypeStruct((bsz * seq_len, GMLP_WIDTH), F32)],
        scratch_shapes=[pltpu.VMEM((n, D_MODEL), BF16)],
        compiler_params=pltpu.CompilerParams(dimension_semantics=("arbitrary",), vmem_limit_bytes=VMEM_LIMIT),
        name="mixer_sample",
    )(x2, s0, *weights)
    return h, s_new, vg.reshape(bsz, seq_len, GMLP_WIDTH)


def _dispatch(h_p, h_s, wts):
    npb, nsb = h_p.shape[0] // BLOCK, h_s.shape[0] // BLOCK
    nb_real = npb + nsb
    nb = nb_real + DUMP_BLOCKS
    tok_spec = pl.BlockSpec((1, N_EXPERTS, BLOCK), lambda i: (i, 0, 0))
    run_spec = pl.BlockSpec((1, N_EXPERTS, LANES), lambda i: (i, 0, 0))
    return pl.pallas_call(
        functools.partial(_dispatch_kernel, npb, nb_real),
        grid=(nb,),
        in_specs=[pl.BlockSpec((BLOCK, D_MODEL), lambda i: (jnp.minimum(i, npb - 1), 0)),
                  pl.BlockSpec((BLOCK, D_MODEL), lambda i: (jnp.clip(i - npb, 0, nsb - 1), 0)),
                  _full(wts["wrt"].shape), _full(wts["rbias"].shape)],
        out_specs=[pl.BlockSpec((BLOCK_ROWS, D_MODEL), lambda i: (i, 0)), tok_spec, tok_spec, run_spec, run_spec],
        out_shape=[jax.ShapeDtypeStruct((nb * BLOCK_ROWS, D_MODEL), BF16),
                   jax.ShapeDtypeStruct((nb, N_EXPERTS, BLOCK), F32),
                   jax.ShapeDtypeStruct((nb, N_EXPERTS, BLOCK), F32),
                   jax.ShapeDtypeStruct((nb, N_EXPERTS, LANES), F32),
                   jax.ShapeDtypeStruct((nb, N_EXPERTS, LANES), F32)],
        compiler_params=pltpu.CompilerParams(dimension_semantics=("arbitrary",), vmem_limit_bytes=VMEM_LIMIT),
        name="moe_dispatch",
    )(h_p, h_s, wts["wrt"], wts["rbias"])


def _max_tiles(nb):
    return (nb * (BLOCK_ROWS_USED // SEG) + N_EXPERTS * (TILE_CHUNKS - 1)) // TILE_CHUNKS + 1


def _max_expert_chunks(nb):
    most = nb * (BLOCK // SEG) + TILE_CHUNKS - 1
    return -(-most // LANES) * LANES


def _plan(seg_lanes, nb):
    nt_max = _max_tiles(nb)
    max_chunks = _max_expert_chunks(nb)
    seg = seg_lanes[:nb, :, 0].astype(jnp.int32)
    start = jnp.cumsum(seg, axis=1) - seg
    nch = seg // SEG
    ends = jnp.cumsum(nch, axis=0)
    n_chunks = ends[-1]
    f = jnp.arange(nb, dtype=jnp.int32)[:, None] * (BLOCK_ROWS // SEG) + start // SEG - (ends - nch)
    q = jnp.arange(max_chunks, dtype=jnp.int32)
    passed = (ends[None, :-1, :] <= q[:, None, None]).astype(jnp.int32)
    chunk = q[:, None] + f[0][None, :] + jnp.sum(passed * (f[1:] - f[:-1])[None], axis=1)
    table = jnp.where(q[:, None] < n_chunks[None, :], chunk, ZERO_CHUNK).T
    n_tiles_e = -(-n_chunks // TILE_CHUNKS)
    tile_ends = jnp.cumsum(n_tiles_e)
    t = jnp.arange(nt_max, dtype=jnp.int32)
    done = (tile_ends[None, :] <= t[:, None]).astype(jnp.int32)
    tile_expert = jnp.minimum(jnp.sum(done, axis=1), N_EXPERTS - 1)
    tile_in_expert = t - jnp.sum(done * n_tiles_e[None, :], axis=1)
    return (table.astype(jnp.int32).reshape(N_EXPERTS, 1, max_chunks), tile_expert.astype(jnp.int32),
            tile_in_expert.astype(jnp.int32), n_chunks.astype(jnp.int32), tile_ends[-1:].astype(jnp.int32))


def _ffn(x_rows, nb, table, tile_expert, tile_in_expert, n_chunks, n_tiles, w_gate, w_up, w_down):
    nt_max = tile_expert.shape[0]
    max_chunks = table.shape[-1]
    tab_spec = lambda ahead: pl.BlockSpec(
        (1, 1, max_chunks), lambda i, te, *_: (te[jnp.clip(i + ahead, 0, nt_max - 1)], 0, 0),
        memory_space=pltpu.SMEM)
    w_spec = lambda shape: pl.BlockSpec((1,) + shape, lambda i, te, *_: (te[i], 0, 0))
    assert x_rows.shape[0] - nb * BLOCK_ROWS >= 2 * TILE_ROWS
    x_chunks = x_rows.reshape(x_rows.shape[0] // SEG, SEG, D_MODEL)
    return pl.pallas_call(
        functools.partial(_ffn_kernel, nb * BLOCK_ROWS // SEG, nt_max),
        grid_spec=pltpu.PrefetchScalarGridSpec(
            num_scalar_prefetch=4,
            grid=(nt_max,),
            in_specs=[tab_spec(ahead) for ahead in range(IN_BUFS)] + [pl.BlockSpec(memory_space=pl.ANY),
                      w_spec((D_MODEL, D_EXPERT)), w_spec((D_MODEL, D_EXPERT)), w_spec((D_EXPERT, D_MODEL))],
            out_specs=pl.BlockSpec(memory_space=pl.ANY),
            scratch_shapes=[pltpu.VMEM((IN_BUFS, TILE_CHUNKS, SEG, D_MODEL), BF16),
                            pltpu.VMEM((2, TILE_CHUNKS, SEG, D_MODEL), BF16),
                            pltpu.VMEM((D_MODEL, D_EXPERT), BF16),
                            pltpu.VMEM((D_MODEL, D_EXPERT), BF16),
                            pltpu.VMEM((D_EXPERT, D_MODEL), BF16),
                            pltpu.SemaphoreType.DMA((IN_BUFS,)),
                            pltpu.SemaphoreType.DMA((2,))]),
        out_shape=jax.ShapeDtypeStruct(x_chunks.shape, BF16),
        input_output_aliases={4 + IN_BUFS: 0},
        compiler_params=pltpu.CompilerParams(dimension_semantics=("arbitrary",), vmem_limit_bytes=VMEM_LIMIT),
        name="moe_ffn",
    )(tile_expert, tile_in_expert, n_chunks, n_tiles, *([table] * IN_BUFS), x_chunks, w_gate, w_up, w_down
      ).reshape(x_rows.shape)


def _combine(h, o_rows, routing, block_off, wts):
    nblk = h.shape[0] // BLOCK
    weights = [wts[k] for k in ("wsg", "wsu", "wsd", "ln2g", "ln2b")]
    tok_spec = pl.BlockSpec((1, N_EXPERTS, BLOCK), lambda i: (i + block_off, 0, 0))
    run_spec = pl.BlockSpec((1, N_EXPERTS, LANES), lambda i: (i + block_off, 0, 0))
    return pl.pallas_call(
        _combine_kernel,
        grid=(nblk,),
        in_specs=[pl.BlockSpec((BLOCK, D_MODEL), lambda i: (i, 0)),
                  pl.BlockSpec((BLOCK_ROWS, D_MODEL), lambda i: (i + block_off, 0)),
                  tok_spec, tok_spec, run_spec, run_spec] + [_full(w.shape) for w in weights],
        out_specs=pl.BlockSpec((BLOCK, D_MODEL), lambda i: (i, 0)),
        out_shape=jax.ShapeDtypeStruct(h.shape, F32),
        compiler_params=pltpu.CompilerParams(dimension_semantics=("arbitrary",), vmem_limit_bytes=VMEM_LIMIT),
        name="moe_combine",
    )(h, o_rows, *routing, *weights)


def _moe(h_p, h_s, w_gate, w_up, w_down, wts):
    nb = (h_p.shape[0] + h_s.shape[0]) // BLOCK
    x_rows, rank, comb, seg_lanes, start_lanes = _dispatch(h_p, h_s, wts)
    o_rows = _ffn(x_rows, nb, *_plan(seg_lanes, nb), w_gate, w_up, w_down)
    routing = (rank, comb, seg_lanes, start_lanes)
    y_p = _combine(h_p, o_rows, routing, 0, wts)
    y_s = _combine(h_s, o_rows, routing, h_p.shape[0] // BLOCK, wts)
    return y_p, y_s


def _prep_weights(seq_len, w_in, w_a_up, b_a, gla_norm_g, gmlp_ln_g, gmlp_ln_b, w_s, b_s, gmlp_norm_g, w_out,
                  ln1_g, ln1_b, w_router, router_bias, ws_gate, ws_up, ws_down, ln2_g, ln2_b):
    o1 = QK_WIDTH
    o2 = o1 + QK_WIDTH
    o3 = o2 + GLA_WIDTH
    o4 = o3 + GLA_WIDTH
    o5 = o4 + GLA_RANK
    o6 = o5 + GMLP_WIDTH
    wq, wk, wva, wg_, walr, wu_, wvb = jnp.split(w_in, [o1, o2, o3, o4, o5, o6], axis=-1)
    walr_p = jnp.pad(walr, ((0, 0), (0, LANES - GLA_RANK)))
    waup_p = jnp.pad(w_a_up, ((0, LANES - GLA_RANK), (0, 0)))
    row = lambda a: a.reshape(1, -1)
    reps = (SAMPLE_SEQS * seq_len) // seq_len
    ws_small = w_s[:, :seq_len, :seq_len]
    return {
        "win": jnp.concatenate([wq, wk, wva, wg_, wu_, wvb, walr_p], axis=-1).astype(BF16),
        "wkt": wk.T.astype(BF16),
        "walrt": walr_p.T.astype(BF16),
        "waup": waup_p.astype(BF16),
        "waupt": waup_p.T.astype(BF16),
        "ba": row(b_a), "bac": b_a.reshape(-1, 1),
        "glag": gla_norm_g, "lng": row(gmlp_ln_g), "lnb": row(gmlp_ln_b),
        "ws": w_s,
        "ws_sample": jnp.pad(ws_small, ((0, 0), (0, LANES - seq_len), (0, LANES - seq_len))),
        "bsf_prompt": jnp.repeat(b_s[:, :GMLP_CHUNK].T, GMLP_DH, axis=1),
        "bsf_sample": jnp.tile(jnp.repeat(b_s[:, :seq_len].T, GMLP_DH, axis=1), (reps, 1)),
        "gmg": gmlp_norm_g,
        "wout": w_out.astype(BF16),
        "ln1g": row(ln1_g), "ln1b": row(ln1_b),
        "wrt": w_router.T.astype(BF16), "rbias": router_bias.reshape(-1, 1),
        "wsg": ws_gate.astype(BF16), "wsu": ws_up.astype(BF16), "wsd": ws_down.astype(BF16),
        "ln2g": row(ln2_g), "ln2b": row(ln2_b),
    }


def kernel(x_prompt, x_sample, state_gla, w_in, w_a_up, b_a, gla_norm_g, gmlp_ln_g, gmlp_ln_b, w_s, b_s,
           gmlp_norm_g, w_out, ln1_g, ln1_b, w_router, router_bias, w_gate, w_up, w_down, ws_gate, ws_up,
           ws_down, ln2_g, ln2_b):
    assert x_prompt.shape[1] % PROMPT_TILE == 0 and x_sample.shape[0] % SAMPLE_SEQS == 0
    assert x_prompt.shape[0] % PROMPT_SEQS == 0
    assert x_sample.shape[1] <= GMLP_CHUNK and w_in.shape[0] == DEPTH
    assert (x_sample.shape[0] * x_sample.shape[1]) % BLOCK == 0 and PROMPT_TILE % BLOCK == 0
    bsz, seq, _ = x_prompt.shape
    dbsz, dseq, _ = x_sample.shape
    hp, hs = x_prompt, x_sample
    gla_p, gla_s, v_s = [], [], []
    for l in range(DEPTH):
        wts = _prep_weights(dseq, w_in[l], w_a_up[l], b_a[l], gla_norm_g[l], gmlp_ln_g[l], gmlp_ln_b[l], w_s[l],
                            b_s[l], gmlp_norm_g[l], w_out[l], ln1_g[l], ln1_b[l], w_router[l], router_bias[l],
                            ws_gate[l], ws_up[l], ws_down[l], ln2_g[l], ln2_b[l])
        h_p, sp = _mixer_prompt(hp, wts)
        h_s, ss, vrows = _mixer_sample(hs, state_gla[l], wts)
        y_p, y_s = _moe(h_p.reshape(bsz * seq, D_MODEL), h_s, w_gate[l], w_up[l], w_down[l], wts)
        hp = y_p.reshape(bsz, seq, D_MODEL)
        hs = y_s.reshape(dbsz, dseq, D_MODEL)
        gla_p.append(sp)
        gla_s.append(ss)
        v_s.append(vrows)
    stack = lambda xs: xs[0][None] if len(xs) == 1 else jnp.stack(xs)
    return (hp, hs, stack(gla_p), stack(gla_s), stack(v_s))
```

```python
import functools
import math

import jax
import jax.numpy as jnp
from jax import lax
from jax.experimental import pallas as pl
from jax.experimental.pallas import tpu as pltpu

F32 = jnp.float32
BF16 = jnp.bfloat16

D_MODEL = 1024
DEPTH = 1
GLA_WIDTH = 512
GLA_HEADS = 4
GLA_DK = 64
GLA_DV = 128
GLA_RANK = 16
GLA_TAU = 16.0
GMLP_WIDTH = 512
GMLP_HEADS = 4
GMLP_DH = 128
GMLP_CHUNK = 128
QK_WIDTH = GLA_HEADS * GLA_DK
N_EXPERTS = 64
TOP_K = 8
N_GROUPS = 8
GROUP_SIZE = N_EXPERTS // N_GROUPS
TOPK_GROUPS = 4
D_EXPERT = 256
D_SHARED = 256
ROUTE_SCALE = 2.5
ALPHA = (2.0 * DEPTH) ** 0.25

LANES = 128

C_Q = 0
C_K = C_Q + QK_WIDTH
C_VA = C_K + QK_WIDTH
C_G = C_VA + GLA_WIDTH
C_U = C_G + GLA_WIDTH
C_VB = C_U + GMLP_WIDTH
C_ALR = C_VB + GMLP_WIDTH
N_PROJ = C_ALR + LANES

CHUNK = 128
GLA_BLOCK = 256
PROMPT_TILE = 512
PROMPT_SEQS = 2
SAMPLE_SEQS = 32
VMEM_LIMIT = 56 * 1024 * 1024

BLOCK = 256
SEG = 16
TILE_CHUNKS = 64
TILE_ROWS = TILE_CHUNKS * SEG
GATHER_ROWS = 512
BLOCK_ROWS_USED = BLOCK * TOP_K + N_EXPERTS * (SEG - 1)
BLOCK_ROWS = -(-(BLOCK_ROWS_USED + SEG) // GATHER_ROWS) * GATHER_ROWS
DUMP_BLOCKS = -(-(2 * TILE_ROWS) // BLOCK_ROWS)
ZERO_CHUNK = BLOCK_ROWS_USED // SEG
IN_BUFS = 4


def _dot(a, b):
    return jnp.dot(a, b, preferred_element_type=F32)


def _dot_nt(a, b):
    return lax.dot_general(a, b, (((1,), (1,)), ((), ())), preferred_element_type=F32)


def _dot_tn(a, b):
    return lax.dot_general(a, b, (((0,), (0,)), ((), ())), preferred_element_type=F32)


def _shr(x, d):
    assert d & (d - 1) == 0
    return lax.shift_right_logical(x, d.bit_length() - 1)


def _split_dot(m01, x):
    hi = x.astype(BF16)
    lo = (x - hi.astype(F32)).astype(BF16)
    return _dot(m01, hi) + _dot(m01, lo)


def _split_dot_r(x, m01):
    hi = x.astype(BF16)
    lo = (x - hi.astype(F32)).astype(BF16)
    return _dot(hi, m01) + _dot(lo, m01)


def _sigmoid(x):
    return 1.0 / (1.0 + jnp.exp(-x))


def _silu(x):
    return x * _sigmoid(x)


def _gelu(x):
    c = math.sqrt(2.0 / math.pi)
    return x * (0.5 * (1.0 + jnp.tanh(c * (x + 0.044715 * (x * x * x)))))


def _log_sigmoid(x):
    return -(jnp.maximum(-x, 0.0) + jnp.log(1.0 + jnp.exp(-jnp.abs(x))))


def _layernorm(x, g, b, eps=1e-5):
    mu = jnp.mean(x, axis=-1, keepdims=True)
    xc = x - mu
    var = jnp.mean(xc * xc, axis=-1, keepdims=True)
    return xc * lax.rsqrt(var + eps) * g + b


def _rmsnorm(x, g, eps=1e-6):
    return x * lax.rsqrt(jnp.mean(x * x, axis=-1, keepdims=True) + eps) * g


def _gmlp_heads(u, vg, w_mix, bias, gmg_ref, merged_ref, rows):
    vgb = vg.astype(BF16)
    for h in range(GMLP_HEADS):
        cs = slice(h * GMLP_DH, (h + 1) * GMLP_DH)
        sgu = _dot(w_mix(h), vgb[:, cs]) + bias[:, cs]
        y = _rmsnorm(u[:, cs] * sgu, gmg_ref[h:h + 1, :])
        merged_ref[rows, GLA_WIDTH + h * GMLP_DH:GLA_WIDTH + (h + 1) * GMLP_DH] = y.astype(BF16)


def _out_proj_ln(x, merged_ref, wout_ref, ln1g_ref, ln1b_ref):
    m = _dot(merged_ref[...], wout_ref[...])
    return _layernorm(ALPHA * x + m, ln1g_ref[...], ln1b_ref[...])


def _mixer_prompt_kernel(x_ref, win_ref, waup_ref, ba_ref, glag_ref, lng_ref, lnb_ref, ws_ref, bsf_ref,
                         gmg_ref, wout_ref, ln1g_ref, ln1b_ref,
                         h_ref, state_ref,
                         z_ref, merged_ref, *st_refs):
    t = pl.program_id(1)

    @pl.when(t == 0)
    def _():
        for st_ref in st_refs:
            st_ref[...] = jnp.zeros_like(st_ref)

    x = x_ref[...].reshape(PROMPT_SEQS * PROMPT_TILE, D_MODEL)
    z_ref[...] = _dot(x.astype(BF16), win_ref[...])

    row_i = lax.broadcasted_iota(jnp.int32, (GLA_BLOCK, GLA_BLOCK), 0)
    col_i = lax.broadcasted_iota(jnp.int32, (GLA_BLOCK, GLA_BLOCK), 1)
    causal = row_i >= col_i
    tri = jnp.where(causal, 1.0, 0.0).astype(BF16)
    causal_mix = causal[:CHUNK, :CHUNK]
    lane = lax.broadcasted_iota(jnp.int32, (1, LANES), 1)
    head_lanes = [lane < GLA_DK, lane >= GLA_DK]
    mid = GLA_BLOCK // 2 - 1

    for c in range(PROMPT_SEQS * PROMPT_TILE // GLA_BLOCK):
        rows = slice(c * GLA_BLOCK, (c + 1) * GLA_BLOCK)
        st_ref = st_refs[c // (PROMPT_TILE // GLA_BLOCK)]
        a_pre = _dot(z_ref[rows, C_ALR:C_ALR + LANES].astype(BF16), waup_ref[...]) + ba_ref[...]
        log_a = _log_sigmoid(a_pre) * (1.0 / GLA_TAU)
        b = _split_dot(tri, log_a)
        b_mid = b[mid:mid + 1, :]
        b_last = b[GLA_BLOCK - 1:GLA_BLOCK, :]
        q = z_ref[rows, C_Q:C_Q + QK_WIDTH] * (GLA_DK ** -0.5)
        k = z_ref[rows, C_K:C_K + QK_WIDTH]
        q_in = (q * jnp.exp(b - b_mid)).astype(BF16)
        k_in = (k * jnp.exp(b_mid - b)).astype(BF16)
        q_st = (q * jnp.exp(b)).astype(BF16)
        k_st = (k * jnp.exp(b_last - b)).astype(BF16)
        d_last = jnp.exp(b_last)
        va = z_ref[rows, C_VA:C_VA + GLA_WIDTH].astype(BF16)
        for h in range(GLA_HEADS):
            ps = slice((h // 2) * LANES, (h // 2 + 1) * LANES)
            vs = slice(h * GLA_DV, (h + 1) * GLA_DV)
            hm = head_lanes[h % 2]
            zero = jnp.zeros((), BF16)
            a = _dot_nt(jnp.where(hm, q_in[:, ps], zero), k_in[:, ps])
            a = jnp.where(causal, a, 0.0).astype(BF16)
            st = st_ref[h]
            o = _dot(a, va[:, vs]) + _dot_nt(jnp.where(hm, q_st[:, ps], zero), st.astype(BF16))
            upd = _dot_tn(va[:, vs], jnp.where(hm, k_st[:, ps], zero))
            st_ref[h] = st * d_last[:, ps] + upd
            gate = z_ref[rows, C_G + h * GLA_DV:C_G + (h + 1) * GLA_DV]
            o = _rmsnorm(o, glag_ref[h:h + 1, :]) * _silu(gate)
            merged_ref[rows, vs] = o.astype(BF16)

    for c in range(PROMPT_SEQS * PROMPT_TILE // CHUNK):
        rows = slice(c * CHUNK, (c + 1) * CHUNK)
        u = _gelu(z_ref[rows, C_U:C_U + GMLP_WIDTH])
        vg = _layernorm(_gelu(z_ref[rows, C_VB:C_VB + GMLP_WIDTH]), lng_ref[...], lnb_ref[...])
        _gmlp_heads(u, vg, lambda h: jnp.where(causal_mix, ws_ref[h], 0.0).astype(BF16), bsf_ref[...],
                    gmg_ref, merged_ref, rows)

    h_ref[...] = _out_proj_ln(x, merged_ref, wout_ref, ln1g_ref, ln1b_ref).reshape(PROMPT_SEQS, PROMPT_TILE, D_MODEL)

    @pl.when(t == pl.num_programs(1) - 1)
    def _():
        for s, st_ref in enumerate(st_refs):
            for h in range(GLA_HEADS):
                lo = (h % 2) * GLA_DK
                state_ref[s, h] = st_ref[h].T[lo:lo + GLA_DK, :]


def _mixer_sample_kernel(seq_len, x_ref, s0_ref, win_ref, wkt_ref, walrt_ref, waup_ref, waupt_ref, ba_ref, bac_ref,
                         glag_ref, lng_ref, lnb_ref, wst_ref, bsf_ref, gmg_ref, wout_ref, ln1g_ref, ln1b_ref,
                         h_ref, snew_ref, vg_ref,
                         merged_ref):
    n = SAMPLE_SEQS * seq_len
    x = x_ref[...]
    xb = x.astype(BF16)
    z = _dot(xb, win_ref[...])

    ti = lax.broadcasted_iota(jnp.int32, (n, n), 0)
    tj = lax.broadcasted_iota(jnp.int32, (n, n), 1)
    same = _shr(ti, seq_len) == _shr(tj, seq_len)
    causal = jnp.logical_and(same, ti >= tj)
    tri = jnp.where(causal, 1.0, 0.0).astype(BF16)
    tri_t = jnp.where(jnp.logical_and(same, ti <= tj), 1.0, 0.0).astype(BF16)
    same01 = jnp.where(same, 1.0, 0.0).astype(BF16)

    a_pre = _dot(z[:, C_ALR:C_ALR + LANES].astype(BF16), waup_ref[...]) + ba_ref[...]
    log_a = _log_sigmoid(a_pre) * (1.0 / GLA_TAU)
    b = _split_dot(tri, log_a)
    q = z[:, C_Q:C_Q + QK_WIDTH] * (GLA_DK ** -0.5)
    k = z[:, C_K:C_K + QK_WIDTH]
    q_in = (q * jnp.exp(b)).astype(BF16)
    k_in = (k * jnp.exp(-b)).astype(BF16)
    va = z[:, C_VA:C_VA + GLA_WIDTH].astype(BF16)

    k_t = _dot_nt(wkt_ref[...], xb)
    alr_t = _dot_nt(walrt_ref[...], xb)
    a_pre_t = _dot(waupt_ref[...], alr_t.astype(BF16)) + bac_ref[...]
    log_a_t = _log_sigmoid(a_pre_t) * (1.0 / GLA_TAU)
    b_t = _split_dot_r(log_a_t, tri_t)
    tot_t = _split_dot_r(log_a_t, same01)
    k_st_t = k_t * jnp.exp(tot_t - b_t)
    d_t = jnp.exp(tot_t)

    lane = lax.broadcasted_iota(jnp.int32, (1, LANES), 1)
    head_lanes = [lane < GLA_DK, lane >= GLA_DK]
    nrow = SAMPLE_SEQS * GLA_DK
    r_seq = _shr(lax.broadcasted_iota(jnp.int32, (nrow, n), 0), GLA_DK)
    c_seq = _shr(lax.broadcasted_iota(jnp.int32, (nrow, n), 1), seq_len)
    c_first = (lax.broadcasted_iota(jnp.int32, (nrow, n), 1) & (seq_len - 1)) == 0
    blk = r_seq == c_seq
    blk_first = jnp.logical_and(blk, c_first)
    q_seq = _shr(lax.broadcasted_iota(jnp.int32, (n, nrow), 0), seq_len)
    q_col = _shr(lax.broadcasted_iota(jnp.int32, (n, nrow), 1), GLA_DK)
    blk_q = q_seq == q_col

    for h in range(GLA_HEADS):
        ps = slice((h // 2) * LANES, (h // 2 + 1) * LANES)
        ds_ = slice(h * GLA_DK, (h + 1) * GLA_DK)
        vs = slice(h * GLA_DV, (h + 1) * GLA_DV)
        hm = head_lanes[h % 2]
        zero = jnp.zeros((), BF16)
        a = _dot_nt(jnp.where(hm, q_in[:, ps], zero), k_in[:, ps])
        a = jnp.where(causal, a, 0.0).astype(BF16)
        s0 = s0_ref[:, h].reshape(nrow, GLA_DV)
        q_h = q_in[:, ds_]
        q_bd = jnp.where(blk_q, jnp.concatenate([q_h] * SAMPLE_SEQS, axis=1), zero)
        o = _dot(a, va[:, vs]) + _dot(q_bd, s0.astype(BF16))
        k_bd = jnp.where(blk, jnp.concatenate([k_st_t[ds_, :]] * SAMPLE_SEQS, axis=0), 0.0).astype(BF16)
        upd = _dot(k_bd, va[:, vs])
        d_bd = jnp.where(blk_first, jnp.concatenate([d_t[ds_, :]] * SAMPLE_SEQS, axis=0), 0.0)
        d_col = jnp.sum(d_bd, axis=1, keepdims=True)
        snew_ref[:, h] = (s0 * d_col + upd).reshape(SAMPLE_SEQS, GLA_DK, GLA_DV)
        gate = z[:, C_G + h * GLA_DV:C_G + (h + 1) * GLA_DV]
        o = _rmsnorm(o, glag_ref[h:h + 1, :]) * _silu(gate)
        merged_ref[:, vs] = o.astype(BF16)

    u = _gelu(z[:, C_U:C_U + GMLP_WIDTH])
    vg = _layernorm(_gelu(z[:, C_VB:C_VB + GMLP_WIDTH]), lng_ref[...], lnb_ref[...])
    vg_ref[...] = vg
    pi = lax.broadcasted_iota(jnp.int32, (n, LANES), 0)
    pj = lax.broadcasted_iota(jnp.int32, (n, LANES), 1)
    pos = jnp.where((pi & (seq_len - 1)) == pj, 1.0, 0.0).astype(BF16)

    def w_mix(h):
        rows = _dot(pos, wst_ref[h].astype(BF16)).astype(BF16)
        return jnp.where(causal, _dot_nt(rows, pos), 0.0).astype(BF16)

    _gmlp_heads(u, vg, w_mix, bsf_ref[...], gmg_ref, merged_ref, slice(None))
    h_ref[...] = _out_proj_ln(x, merged_ref, wout_ref, ln1g_ref, ln1b_ref)


def _route(hb, wrt_ref, rbias_ref):
    tm = hb.shape[0]
    s = _sigmoid(_dot_nt(wrt_ref[...], hb))
    sb = s + rbias_ref[...]
    neg = jnp.float32(-jnp.inf)
    sub = lax.broadcasted_iota(jnp.int32, (GROUP_SIZE, tm), 0)
    gscore = []
    for g in range(N_GROUPS):
        blk = sb[g * GROUP_SIZE:(g + 1) * GROUP_SIZE, :]
        m1 = jnp.max(blk, axis=0, keepdims=True)
        i1 = jnp.min(jnp.where(blk == m1, sub, GROUP_SIZE), axis=0, keepdims=True)
        m2 = jnp.max(jnp.where(sub == i1, neg, blk), axis=0, keepdims=True)
        gscore.append(m1 + m2)
    gsel = []
    for g in range(N_GROUPS):
        rank = jnp.zeros((1, tm), jnp.int32)
        for o in range(N_GROUPS):
            if o == g:
                continue
            ahead = (gscore[o] >= gscore[g]) if o < g else (gscore[o] > gscore[g])
            rank = rank + ahead.astype(jnp.int32)
        gsel.append(rank < TOPK_GROUPS)
    v = jnp.concatenate(
        [jnp.where(gsel[g], sb[g * GROUP_SIZE:(g + 1) * GROUP_SIZE, :], neg) for g in range(N_GROUPS)], axis=0)
    eidx = lax.broadcasted_iota(jnp.int32, (N_EXPERTS, tm), 0)
    sel = jnp.zeros((N_EXPERTS, tm), F32)
    for _ in range(TOP_K):
        m = jnp.max(v, axis=0, keepdims=True)
        first = jnp.min(jnp.where(v == m, eidx, N_EXPERTS), axis=0, keepdims=True)
        hit = eidx == first
        sel = jnp.where(hit, 1.0, sel)
        v = jnp.where(hit, neg, v)
    wsel = s * sel
    return wsel / jnp.sum(wsel, axis=0, keepdims=True) * ROUTE_SCALE, sel


def _per_row(chunk_vals):
    n, t = chunk_vals.shape
    return jnp.broadcast_to(chunk_vals[:, None, :], (n, SEG, t)).reshape(n * SEG, t)


def _row_lookup(seg_b, start_b, rank, tables):
    n_chunks = BLOCK_ROWS // SEG
    ei = lax.broadcasted_iota(jnp.int32, (N_EXPERTS, LANES), 0)
    ej = lax.broadcasted_iota(jnp.int32, (N_EXPERTS, LANES), 1)
    seg16 = (seg_b * (1.0 / SEG)).astype(BF16)
    start_row = _dot_tn(seg16, jnp.where(ei < ej, 1.0, 0.0).astype(BF16))[0:1, :]
    seg_row = _dot_tn(seg16, jnp.where(ei == ej, 1.0, 0.0).astype(BF16))[0:1, :]
    c = lax.broadcasted_iota(jnp.int32, (n_chunks, LANES), 0).astype(F32)
    owner = jnp.where(c >= start_row, jnp.where(c < start_row + seg_row, 1.0, 0.0), 0.0).astype(BF16)

    def lookup(tab):
        padded = jnp.concatenate([tab.astype(BF16), jnp.zeros((LANES - N_EXPERTS, tab.shape[1]), BF16)], axis=0)
        return _dot(owner, padded)

    first = SEG * (c - lookup(start_b * (1.0 / SEG)))[:, 0:1]
    rel = lookup(rank) - first
    sub = lax.broadcasted_iota(jnp.int32, (BLOCK_ROWS, rank.shape[1]), 0) & (SEG - 1)
    hits = _per_row(rel) == sub.astype(F32)
    return hits, [_per_row(lookup(tab)) for tab in tables]


def _dispatch_kernel(n_prompt_blocks, n_blocks, hp_ref, hs_ref, wrt_ref, rbias_ref, x_ref, rank_ref, comb_ref,
                     seg_ref, start_ref):
    i = pl.program_id(0)

    @pl.when(i < n_blocks)
    def _():
        _dispatch_block(i < n_prompt_blocks, hp_ref, hs_ref, wrt_ref, rbias_ref, x_ref, rank_ref, comb_ref, seg_ref,
                        start_ref)

    @pl.when(i >= n_blocks)
    def _():
        for ref in (x_ref, rank_ref, comb_ref, seg_ref, start_ref):
            ref[...] = jnp.zeros_like(ref)


def _dispatch_block(is_prompt, hp_ref, hs_ref, wrt_ref, rbias_ref, x_ref, rank_ref, comb_ref, seg_ref, start_ref):
    tm = BLOCK
    hb = jnp.where(is_prompt, hp_ref[...], hs_ref[...]).astype(BF16)
    comb, sel = _route(hb, wrt_ref, rbias_ref)
    ti = lax.broadcasted_iota(jnp.int32, (tm, tm), 0)
    tj = lax.broadcasted_iota(jnp.int32, (tm, tm), 1)
    before = jnp.where(ti < tj, 1.0, 0.0).astype(BF16)
    rank = _dot(sel.astype(BF16), before)
    rank = jnp.where(sel > 0.0, rank, -1.0)
    cnt = jnp.sum(sel, axis=1, keepdims=True)
    seg = jnp.floor((cnt + (SEG - 1.0)) * (1.0 / SEG)) * SEG
    ei = lax.broadcasted_iota(jnp.int32, (N_EXPERTS, N_EXPERTS), 0)
    ej = lax.broadcasted_iota(jnp.int32, (N_EXPERTS, N_EXPERTS), 1)
    below = jnp.where(ej < ei, 1.0, 0.0).astype(BF16)
    seg_b = jnp.broadcast_to(seg, (N_EXPERTS, LANES))
    start_b = _dot(below, seg_b.astype(BF16))
    hits, _ = _row_lookup(seg_b, start_b, rank, [])
    gather = jnp.where(hits, 1.0, 0.0).astype(BF16)
    for c in range(BLOCK_ROWS // GATHER_ROWS):
        rs = slice(c * GATHER_ROWS, (c + 1) * GATHER_ROWS)
        x_ref[rs, :] = _dot(gather[rs, :], hb).astype(BF16)
    rank_ref[0] = rank
    comb_ref[0] = comb
    seg_ref[0] = seg_b
    start_ref[0] = start_b


def _ffn_kernel(dump_base, nt_max, tile_expert_ref, tile_in_expert_ref, n_chunks_ref, n_tiles_ref,
                *refs):
    tabs = refs[:IN_BUFS]
    x_hbm, wg_ref, wu_ref, wd_ref, o_hbm, ibuf, obuf, wgb, wub, wdb, in_sem, out_sem = refs[IN_BUFS:]
    tab0_ref = tabs[0]
    i = pl.program_id(0)
    n_tiles = n_tiles_ref[0]

    def start_in(ahead, slot):
        exists = i + ahead < n_tiles
        first = jnp.where(exists, tile_in_expert_ref[jnp.minimum(i + ahead, nt_max - 1)] * TILE_CHUNKS, 0)
        srcs = [jnp.where(exists, tabs[ahead][0, 0, first + j], ZERO_CHUNK) for j in range(TILE_CHUNKS)]
        for j in range(TILE_CHUNKS):
            pltpu.make_async_copy(x_hbm.at[srcs[j]], ibuf.at[slot, j], in_sem.at[slot]).start()

    def start_out(slot):
        e = tile_expert_ref[i]
        first = tile_in_expert_ref[i] * TILE_CHUNKS
        n_real = n_chunks_ref[e] - first
        dsts = [jnp.where(j < n_real, tab0_ref[0, 0, first + j], dump_base + slot * TILE_CHUNKS + j)
                for j in range(TILE_CHUNKS)]
        for j in range(TILE_CHUNKS):
            pltpu.make_async_copy(obuf.at[slot, j], o_hbm.at[dsts[j]], out_sem.at[slot]).start()

    def in_wait(slot):
        for j in range(TILE_CHUNKS):
            pltpu.make_async_copy(x_hbm.at[0], ibuf.at[slot, j], in_sem.at[slot]).wait()

    def out_wait(slot):
        for j in range(TILE_CHUNKS):
            pltpu.make_async_copy(obuf.at[slot, j], o_hbm.at[0], out_sem.at[slot]).wait()

    @pl.when(i < n_tiles)
    def _():
        islot = lax.rem(i, IN_BUFS)
        nslot = jnp.where(islot == 0, IN_BUFS - 1, islot - 1)
        oslot = lax.rem(i, 2)

        @pl.when(i == 0)
        def _():
            for ahead in range(IN_BUFS - 1):
                start_in(ahead, ahead)

        @pl.when(i >= 2)
        def _():
            out_wait(oslot)

        in_wait(islot)

        @pl.when(tile_in_expert_ref[i] == 0)
        def _():
            wgb[...] = wg_ref[0].astype(BF16)
            wub[...] = wu_ref[0].astype(BF16)
            wdb[...] = wd_ref[0].astype(BF16)

        start_in(IN_BUFS - 1, nslot)
        x = ibuf[islot].reshape(TILE_ROWS, D_MODEL)
        hid = (_silu(_dot(x, wgb[...])) * _dot(x, wub[...])).astype(BF16)
        obuf[oslot] = _dot(hid, wdb[...]).astype(BF16).reshape(TILE_CHUNKS, SEG, D_MODEL)
        start_out(oslot)

        @pl.when(i == n_tiles - 1)
        def _():
            for ahead in range(1, IN_BUFS):
                in_wait(lax.rem(i + ahead, IN_BUFS))
            out_wait(oslot)

            @pl.when(i >= 1)
            def _():
                out_wait(1 - oslot)


def _combine_kernel(h_ref, o_ref, rank_ref, comb_ref, seg_ref, start_ref, wsg_ref, wsu_ref, wsd_ref, ln2g_ref,
                    ln2b_ref, y_ref):
    h = h_ref[...]
    hb = h.astype(BF16)
    hits, (comb_rows,) = _row_lookup(seg_ref[0], start_ref[0], rank_ref[0], [comb_ref[0]])
    scatter = jnp.where(hits, comb_rows, 0.0).astype(BF16)
    routed = _dot_tn(scatter, o_ref[...])
    shared = _dot((_silu(_dot(hb, wsg_ref[...])) * _dot(hb, wsu_ref[...])).astype(BF16), wsd_ref[...])
    y_ref[...] = _layernorm(ALPHA * h + (routed + shared), ln2g_ref[...], ln2b_ref[...])


def _full(shape):
    return pl.BlockSpec(shape, lambda *_: (0,) * len(shape))


def _const(shape):
    return pl.BlockSpec(shape, lambda *_: (0,) * len(shape), pipeline_mode=pl.Buffered(1))


def _mixer_prompt(x, wts):
    bsz, seq, _ = x.shape
    n_t = seq // PROMPT_TILE
    weights = [wts[k] for k in ("win", "waup", "ba", "glag", "lng", "lnb", "ws", "bsf_prompt", "gmg", "wout",
                                "ln1g", "ln1b")]
    return pl.pallas_call(
        _mixer_prompt_kernel,
        grid=(bsz // PROMPT_SEQS, n_t),
        in_specs=[pl.BlockSpec((PROMPT_SEQS, PROMPT_TILE, D_MODEL), lambda b, t: (b, t, 0))]
        + [_const(w.shape) for w in weights],
        out_specs=[pl.BlockSpec((PROMPT_SEQS, PROMPT_TILE, D_MODEL), lambda b, t: (b, t, 0)),
                   pl.BlockSpec((PROMPT_SEQS, GLA_HEADS, GLA_DK, GLA_DV), lambda b, t: (b, 0, 0, 0))],
        out_shape=[jax.ShapeDtypeStruct((bsz, seq, D_MODEL), F32),
                   jax.ShapeDtypeStruct((bsz, GLA_HEADS, GLA_DK, GLA_DV), F32)],
        scratch_shapes=[pltpu.VMEM((PROMPT_SEQS * PROMPT_TILE, N_PROJ), F32),
                        pltpu.VMEM((PROMPT_SEQS * PROMPT_TILE, D_MODEL), BF16)]
        + [pltpu.VMEM((GLA_HEADS, GLA_DV, LANES), F32)] * PROMPT_SEQS,
        compiler_params=pltpu.CompilerParams(dimension_semantics=("arbitrary", "arbitrary"),
                                             vmem_limit_bytes=VMEM_LIMIT),
        name="mixer_prompt",
    )(x, *weights)


def _mixer_sample(x, s0, wts):
    bsz, seq_len, _ = x.shape
    n = SAMPLE_SEQS * seq_len
    x2 = x.reshape(bsz * seq_len, D_MODEL)
    weights = [wts[k] for k in ("win", "wkt", "walrt", "waup", "waupt", "ba", "bac", "glag", "lng", "lnb",
                                "ws_sample", "bsf_sample", "gmg", "wout", "ln1g", "ln1b")]
    state_spec = pl.BlockSpec((SAMPLE_SEQS, GLA_HEADS, GLA_DK, GLA_DV), lambda i: (i, 0, 0, 0))
    h, s_new, vg = pl.pallas_call(
        functools.partial(_mixer_sample_kernel, seq_len),
        grid=(bsz // SAMPLE_SEQS,),
        in_specs=[pl.BlockSpec((n, D_MODEL), lambda i: (i, 0)), state_spec] + [_full(w.shape) for w in weights],
        out_specs=[pl.BlockSpec((n, D_MODEL), lambda i: (i, 0)), state_spec,
                   pl.BlockSpec((n, GMLP_WIDTH), lambda i: (i, 0))],
        out_shape=[jax.ShapeDtypeStruct((bsz * seq_len, D_MODEL), F32),
                   jax.ShapeDtypeStruct(s0.shape, F32),
                   jax.ShapeDtypeStruct((bsz * seq_len, GMLP_WIDTH), F32)],
        scratch_shapes=[pltpu.VMEM((n, D_MODEL), BF16)],
        compiler_params=pltpu.CompilerParams(dimension_semantics=("arbitrary",), vmem_limit_bytes=VMEM_LIMIT),
        name="mixer_sample",
    )(x2, s0, *weights)
    return h, s_new, vg.reshape(bsz, seq_len, GMLP_WIDTH)


def _dispatch(h_p, h_s, wts):
    npb, nsb = h_p.shape[0] // BLOCK, h_s.shape[0] // BLOCK
    nb_real = npb + nsb
    nb = nb_real + DUMP_BLOCKS
    tok_spec = pl.BlockSpec((1, N_EXPERTS, BLOCK), lambda i: (i, 0, 0))
    run_spec = pl.BlockSpec((1, N_EXPERTS, LANES), lambda i: (i, 0, 0))
    return pl.pallas_call(
        functools.partial(_dispatch_kernel, npb, nb_real),
        grid=(nb,),
        in_specs=[pl.BlockSpec((BLOCK, D_MODEL), lambda i: (jnp.minimum(i, npb - 1), 0)),
                  pl.BlockSpec((BLOCK, D_MODEL), lambda i: (jnp.clip(i - npb, 0, nsb - 1), 0)),
                  _full(wts["wrt"].shape), _full(wts["rbias"].shape)],
        out_specs=[pl.BlockSpec((BLOCK_ROWS, D_MODEL), lambda i: (i, 0)), tok_spec, tok_spec, run_spec, run_spec],
        out_shape=[jax.ShapeDtypeStruct((nb * BLOCK_ROWS, D_MODEL), BF16),
                   jax.ShapeDtypeStruct((nb, N_EXPERTS, BLOCK), F32),
                   jax.ShapeDtypeStruct((nb, N_EXPERTS, BLOCK), F32),
                   jax.ShapeDtypeStruct((nb, N_EXPERTS, LANES), F32),
                   jax.ShapeDtypeStruct((nb, N_EXPERTS, LANES), F32)],
        compiler_params=pltpu.CompilerParams(dimension_semantics=("arbitrary",), vmem_limit_bytes=VMEM_LIMIT),
        name="moe_dispatch",
    )(h_p, h_s, wts["wrt"], wts["rbias"])


def _max_tiles(nb):
    return (nb * (BLOCK_ROWS_USED // SEG) + N_EXPERTS * (TILE_CHUNKS - 1)) // TILE_CHUNKS + 1


def _max_expert_chunks(nb):
    most = nb * (BLOCK // SEG) + TILE_CHUNKS - 1
    return -(-most // LANES) * LANES


def _plan(seg_lanes, nb):
    nt_max = _max_tiles(nb)
    max_chunks = _max_expert_chunks(nb)
    seg = seg_lanes[:nb, :, 0].astype(jnp.int32)
    start = jnp.cumsum(seg, axis=1) - seg
    nch = seg // SEG
    ends = jnp.cumsum(nch, axis=0)
    n_chunks = ends[-1]
    f = jnp.arange(nb, dtype=jnp.int32)[:, None] * (BLOCK_ROWS // SEG) + start // SEG - (ends - nch)
    q = jnp.arange(max_chunks, dtype=jnp.int32)
    passed = (ends[None, :-1, :] <= q[:, None, None]).astype(jnp.int32)
    chunk = q[:, None] + f[0][None, :] + jnp.sum(passed * (f[1:] - f[:-1])[None], axis=1)
    table = jnp.where(q[:, None] < n_chunks[None, :], chunk, ZERO_CHUNK).T
    n_tiles_e = -(-n_chunks // TILE_CHUNKS)
    tile_ends = jnp.cumsum(n_tiles_e)
    t = jnp.arange(nt_max, dtype=jnp.int32)
    done = (tile_ends[None, :] <= t[:, None]).astype(jnp.int32)
    tile_expert = jnp.minimum(jnp.sum(done, axis=1), N_EXPERTS - 1)
    tile_in_expert = t - jnp.sum(done * n_tiles_e[None, :], axis=1)
    return (table.astype(jnp.int32).reshape(N_EXPERTS, 1, max_chunks), tile_expert.astype(jnp.int32),
            tile_in_expert.astype(jnp.int32), n_chunks.astype(jnp.int32), tile_ends[-1:].astype(jnp.int32))


def _ffn(x_rows, nb, table, tile_expert, tile_in_expert, n_chunks, n_tiles, w_gate, w_up, w_down):
    nt_max = tile_expert.shape[0]
    max_chunks = table.shape[-1]
    tab_spec = lambda ahead: pl.BlockSpec(
        (1, 1, max_chunks), lambda i, te, *_: (te[jnp.clip(i + ahead, 0, nt_max - 1)], 0, 0),
        memory_space=pltpu.SMEM)
    w_spec = lambda shape: pl.BlockSpec((1,) + shape, lambda i, te, *_: (te[i], 0, 0))
    assert x_rows.shape[0] - nb * BLOCK_ROWS >= 2 * TILE_ROWS
    x_chunks = x_rows.reshape(x_rows.shape[0] // SEG, SEG, D_MODEL)
    return pl.pallas_call(
        functools.partial(_ffn_kernel, nb * BLOCK_ROWS // SEG, nt_max),
        grid_spec=pltpu.PrefetchScalarGridSpec(
            num_scalar_prefetch=4,
            grid=(nt_max,),
            in_specs=[tab_spec(ahead) for ahead in range(IN_BUFS)] + [pl.BlockSpec(memory_space=pl.ANY),
                      w_spec((D_MODEL, D_EXPERT)), w_spec((D_MODEL, D_EXPERT)), w_spec((D_EXPERT, D_MODEL))],
            out_specs=pl.BlockSpec(memory_space=pl.ANY),
            scratch_shapes=[pltpu.VMEM((IN_BUFS, TILE_CHUNKS, SEG, D_MODEL), BF16),
                            pltpu.VMEM((2, TILE_CHUNKS, SEG, D_MODEL), BF16),
                            pltpu.VMEM((D_MODEL, D_EXPERT), BF16),
                            pltpu.VMEM((D_MODEL, D_EXPERT), BF16),
                            pltpu.VMEM((D_EXPERT, D_MODEL), BF16),
                            pltpu.SemaphoreType.DMA((IN_BUFS,)),
                            pltpu.SemaphoreType.DMA((2,))]),
        out_shape=jax.ShapeDtypeStruct(x_chunks.shape, BF16),
        input_output_aliases={4 + IN_BUFS: 0},
        compiler_params=pltpu.CompilerParams(dimension_semantics=("arbitrary",), vmem_limit_bytes=VMEM_LIMIT),
        name="moe_ffn",
    )(tile_expert, tile_in_expert, n_chunks, n_tiles, *([table] * IN_BUFS), x_chunks, w_gate, w_up, w_down
      ).reshape(x_rows.shape)


def _combine(h, o_rows, routing, block_off, wts):
    nblk = h.shape[0] // BLOCK
    weights = [wts[k] for k in ("wsg", "wsu", "wsd", "ln2g", "ln2b")]
    tok_spec = pl.BlockSpec((1, N_EXPERTS, BLOCK), lambda i: (i + block_off, 0, 0))
    run_spec = pl.BlockSpec((1, N_EXPERTS, LANES), lambda i: (i + block_off, 0, 0))
    return pl.pallas_call(
        _combine_kernel,
        grid=(nblk,),
        in_specs=[pl.BlockSpec((BLOCK, D_MODEL), lambda i: (i, 0)),
                  pl.BlockSpec((BLOCK_ROWS, D_MODEL), lambda i: (i + block_off, 0)),
                  tok_spec, tok_spec, run_spec, run_spec] + [_full(w.shape) for w in weights],
        out_specs=pl.BlockSpec((BLOCK, D_MODEL), lambda i: (i, 0)),
        out_shape=jax.ShapeDtypeStruct(h.shape, F32),
        compiler_params=pltpu.CompilerParams(dimension_semantics=("arbitrary",), vmem_limit_bytes=VMEM_LIMIT),
        name="moe_combine",
    )(h, o_rows, *routing, *weights)


def _moe(h_p, h_s, w_gate, w_up, w_down, wts):
    nb = (h_p.shape[0] + h_s.shape[0]) // BLOCK
    x_rows, rank, comb, seg_lanes, start_lanes = _dispatch(h_p, h_s, wts)
    o_rows = _ffn(x_rows, nb, *_plan(seg_lanes, nb), w_gate, w_up, w_down)
    routing = (rank, comb, seg_lanes, start_lanes)
    y_p = _combine(h_p, o_rows, routing, 0, wts)
    y_s = _combine(h_s, o_rows, routing, h_p.shape[0] // BLOCK, wts)
    return y_p, y_s


def _prep_weights(seq_len, w_in, w_a_up, b_a, gla_norm_g, gmlp_ln_g, gmlp_ln_b, w_s, b_s, gmlp_norm_g, w_out,
                  ln1_g, ln1_b, w_router, router_bias, ws_gate, ws_up, ws_down, ln2_g, ln2_b):
    o1 = QK_WIDTH
    o2 = o1 + QK_WIDTH
    o3 = o2 + GLA_WIDTH
    o4 = o3 + GLA_WIDTH
    o5 = o4 + GLA_RANK
    o6 = o5 + GMLP_WIDTH
    wq, wk, wva, wg_, walr, wu_, wvb = jnp.split(w_in, [o1, o2, o3, o4, o5, o6], axis=-1)
    walr_p = jnp.pad(walr, ((0, 0), (0, LANES - GLA_RANK)))
    waup_p = jnp.pad(w_a_up, ((0, LANES - GLA_RANK), (0, 0)))
    row = lambda a: a.reshape(1, -1)
    reps = (SAMPLE_SEQS * seq_len) // seq_len
    ws_small = w_s[:, :seq_len, :seq_len]
    return {
        "win": jnp.concatenate([wq, wk, wva, wg_, wu_, wvb, walr_p], axis=-1).astype(BF16),
        "wkt": wk.T.astype(BF16),
        "walrt": walr_p.T.astype(BF16),
        "waup": waup_p.astype(BF16),
        "waupt": waup_p.T.astype(BF16),
        "ba": row(b_a), "bac": b_a.reshape(-1, 1),
        "glag": gla_norm_g, "lng": row(gmlp_ln_g), "lnb": row(gmlp_ln_b),
        "ws": w_s,
        "ws_sample": jnp.pad(ws_small, ((0, 0), (0, LANES - seq_len), (0, LANES - seq_len))),
        "bsf_prompt": jnp.repeat(b_s[:, :GMLP_CHUNK].T, GMLP_DH, axis=1),
        "bsf_sample": jnp.tile(jnp.repeat(b_s[:, :seq_len].T, GMLP_DH, axis=1), (reps, 1)),
        "gmg": gmlp_norm_g,
        "wout": w_out.astype(BF16),
        "ln1g": row(ln1_g), "ln1b": row(ln1_b),
        "wrt": w_router.T.astype(BF16), "rbias": router_bias.reshape(-1, 1),
        "wsg": ws_gate.astype(BF16), "wsu": ws_up.astype(BF16), "wsd": ws_down.astype(BF16),
        "ln2g": row(ln2_g), "ln2b": row(ln2_b),
    }


def kernel(x_prompt, x_sample, state_gla, w_in, w_a_up, b_a, gla_norm_g, gmlp_ln_g, gmlp_ln_b, w_s, b_s,
           gmlp_norm_g, w_out, ln1_g, ln1_b, w_router, router_bias, w_gate, w_up, w_down, ws_gate, ws_up,
           ws_down, ln2_g, ln2_b):
    assert x_prompt.shape[1] % PROMPT_TILE == 0 and x_sample.shape[0] % SAMPLE_SEQS == 0
    assert x_prompt.shape[0] % PROMPT_SEQS == 0
    assert x_sample.shape[1] <= GMLP_CHUNK and w_in.shape[0] == DEPTH
    assert (x_sample.shape[0] * x_sample.shape[1]) % BLOCK == 0 and PROMPT_TILE % BLOCK == 0
    bsz, seq, _ = x_prompt.shape
    dbsz, dseq, _ = x_sample.shape
    hp, hs = x_prompt, x_sample
    gla_p, gla_s, v_s = [], [], []
    for l in range(DEPTH):
        wts = _prep_weights(dseq, w_in[l], w_a_up[l], b_a[l], gla_norm_g[l], gmlp_ln_g[l], gmlp_ln_b[l], w_s[l],
                            b_s[l], gmlp_norm_g[l], w_out[l], ln1_g[l], ln1_b[l], w_router[l], router_bias[l],
                            ws_gate[l], ws_up[l], ws_down[l], ln2_g[l], ln2_b[l])
        h_p, sp = _mixer_prompt(hp, wts)
        s0 = state_gla.reshape(state_gla.shape[1:]) if DEPTH == 1 else state_gla[l]
        h_s, ss, vrows = _mixer_sample(hs, s0, wts)
        y_p, y_s = _moe(h_p.reshape(bsz * seq, D_MODEL), h_s, w_gate[l], w_up[l], w_down[l], wts)
        hp = y_p.reshape(bsz, seq, D_MODEL)
        hs = y_s.reshape(dbsz, dseq, D_MODEL)
        gla_p.append(sp)
        gla_s.append(ss)
        v_s.append(vrows)
    stack = lambda xs: xs[0][None] if len(xs) == 1 else jnp.stack(xs)
    return (hp, hs, stack(gla_p), stack(gla_s), stack(v_s))
```
